```python
import jax
import jax.numpy as jnp
from jax import lax
import numpy as np

D_MODEL = 1024
BATCH = 4
SEQ = 8192
DEPTH = 2

N_META = 16
CHUNK = 128
PAD_FRONT = CHUNK - N_META
ROPE_BASE = 10000.0
NEG_BIG = -1e30

RET_HEADS = 4
RET_DK = 64
RET_DV = 128
MLA_HEADS = 8
MLA_Q_RANK = 256
MLA_KV_RANK = 128
MLA_NOPE = 64
MLA_ROPE = 32
MLA_DV = 64
LRU_WIDTH = 512
LRU_BLOCKS = 8
LRU_BLOCK = LRU_WIDTH // LRU_BLOCKS
CONV_W = 4
LRU_C = 8.0
GLA_HEADS = 4
GLA_DK = 64
GLA_DV = 128
GLA_RANK = 16
GLA_TAU = 16.0
GLA_CHUNK = 16
N_BRANCH = 4
BRANCH_W = 512
N_EXPERTS = 32
TOP_K = 4
D_EXPERT = 1024
SWIGLU_LIMIT = 7.0
SWIGLU_ALPHA = 1.702
DN_ALPHA = (2.0 * DEPTH) ** 0.25
DN_BETA = (8.0 * DEPTH) ** -0.25

IN_SIZES = (RET_HEADS * RET_DK, RET_HEADS * RET_DK, RET_HEADS * RET_DV, RET_HEADS * RET_DV,
            MLA_Q_RANK, MLA_KV_RANK, MLA_ROPE,
            LRU_WIDTH, LRU_WIDTH,
            GLA_HEADS * GLA_DK, GLA_HEADS * GLA_DK, GLA_HEADS * GLA_DV, GLA_RANK,
            GLA_HEADS * GLA_DV)
D_IN = sum(IN_SIZES)
IN_OFFSETS = tuple(sum(IN_SIZES[:i + 1]) for i in range(len(IN_SIZES) - 1))

kernel_name = 'hybrid_gated_ret_mla_rglru_gla_moe'

F32 = jnp.float32


def _layer_norm(x, g, b, eps=1e-5):
    xf = x.astype(F32)
    mu = jnp.mean(xf, -1, keepdims=True)
    var = jnp.mean(jnp.square(xf - mu), -1, keepdims=True)
    return ((xf - mu) * lax.rsqrt(var + eps) * g.astype(F32) + b.astype(F32)).astype(x.dtype)


def _ln_noaffine(x, eps=1e-5):
    xf = x.astype(F32)
    mu = jnp.mean(xf, -1, keepdims=True)
    var = jnp.mean(jnp.square(xf - mu), -1, keepdims=True)
    return (xf - mu) * lax.rsqrt(var + eps)


def _rms_norm(x, g=None, eps=1e-6):
    xf = x.astype(F32)
    y = xf * lax.rsqrt(jnp.mean(xf * xf, -1, keepdims=True) + eps)
    if g is not None:
        y = y * g.astype(F32)
    return y


def _rope(x, pos):
    d = x.shape[-1]
    inv = ROPE_BASE ** (-jnp.arange(0, d, 2, dtype=F32) / d)
    ang = pos.astype(F32)[:, None] * inv[None, :]
    cos = jnp.cos(ang)[None, :, None, :]
    sin = jnp.sin(ang)[None, :, None, :]
    xf = x.astype(F32)
    x1, x2 = xf[..., :d // 2], xf[..., d // 2:]
    return jnp.concatenate([x1 * cos - x2 * sin, x2 * cos + x1 * sin], -1).astype(x.dtype)


def _pad_front(a):
    return jnp.pad(a, [(0, 0), (PAD_FRONT, 0)] + [(0, 0)] * (a.ndim - 2))


def _linear_combine(left, right):
    a_l, b_l = left
    a_r, b_r = right
    return a_l * a_r, a_r * b_l + b_r


def _retention(q, k, v, g):
    B, T = q.shape[:2]
    Tp = T + PAD_FRONT
    nc = Tp // CHUNK

    def chunks(a):
        return _pad_front(a).reshape(B, nc, CHUNK, RET_HEADS, a.shape[-1]).transpose(0, 3, 1, 2, 4).astype(F32)

    qc, kc, vc = chunks(q), chunks(k * (RET_DK ** -0.5)), chunks(v)
    log_gamma = jnp.log1p(-jnp.exp2(-5.0 - jnp.arange(RET_HEADS, dtype=F32)))
    idx = jnp.arange(CHUNK, dtype=F32)
    rel = idx[:, None] - idx[None, :]
    dmat = jnp.where(rel >= 0, jnp.exp(log_gamma[:, None, None] * jnp.maximum(rel, 0.0)), 0.0)
    scores = jnp.einsum('bhnid,bhnjd->bhnij', qc, kc) * dmat[None, :, None]
    o_inner = jnp.einsum('bhnij,bhnje->bhnie', scores, vc)
    k_dec = jnp.exp(log_gamma[:, None] * (CHUNK - 1.0 - idx))
    q_dec = jnp.exp(log_gamma[:, None] * (idx + 1.0))
    chunk_kv = jnp.einsum('bhnjd,bhnje->nbhde', kc * k_dec[None, :, None, :, None], vc)
    q_in = (qc * q_dec[None, :, None, :, None]).transpose(2, 0, 1, 3, 4)
    chunk_dec = jnp.exp(log_gamma * CHUNK)[None, :, None, None]

    def step(S, inp):
        q_n, kv_n = inp
        return S * chunk_dec + kv_n, jnp.einsum('bhid,bhde->bhie', q_n, S)

    S0 = jnp.zeros((B, RET_HEADS, RET_DK, RET_DV), F32)
    _, o_cross = lax.scan(step, S0, (q_in, chunk_kv))
    o = o_inner + o_cross.transpose(1, 2, 0, 3, 4)
    o = o.transpose(0, 2, 3, 1, 4).reshape(B, Tp, RET_HEADS, RET_DV)[:, PAD_FRONT:]
    o = _ln_noaffine(o).reshape(B, T, RET_HEADS * RET_DV)
    return (jax.nn.swish(g.astype(F32)) * o).astype(g.dtype)


def _causal_block_attention(q, k, v):
    B, T, H, _ = q.shape
    qp, kp, vp = _pad_front(q), _pad_front(k), _pad_front(v)
    Tp = T + PAD_FRONT
    nb = Tp // CHUNK
    key_pos = jnp.arange(Tp)
    key_valid = key_pos >= PAD_FRONT
    qb = qp.reshape(B, nb, CHUNK, H, q.shape[-1]).transpose(1, 0, 2, 3, 4)

    def block(args):
        q_blk, start = args
        s = jnp.einsum('bqhd,bkhd->bhqk', q_blk, kp).astype(F32)
        q_pos = start + jnp.arange(CHUNK)
        mask = (key_pos[None, :] <= q_pos[:, None]) & key_valid[None, :]
        p = jax.nn.softmax(jnp.where(mask, s, NEG_BIG), axis=-1).astype(vp.dtype)
        return jnp.einsum('bhqk,bkhd->bqhd', p, vp)

    o = lax.map(block, (qb, jnp.arange(nb) * CHUNK))
    return o.transpose(1, 0, 2, 3, 4).reshape(B, Tp, H, v.shape[-1])[:, PAD_FRONT:]


def _mla(c_q, c_kv, k_r, g_cq, g_ckv, w_uq, w_ukv, pos):
    B, T = c_q.shape[:2]
    dt = c_q.dtype
    q = (_rms_norm(c_q, g_cq).astype(dt) @ w_uq).reshape(B, T, MLA_HEADS, MLA_NOPE + MLA_ROPE)
    kv = (_rms_norm(c_kv, g_ckv).astype(dt) @ w_ukv).reshape(B, T, MLA_HEADS, MLA_NOPE + MLA_DV)
    q = jnp.concatenate([q[..., :MLA_NOPE], _rope(q[..., MLA_NOPE:], pos)], -1) * ((MLA_NOPE + MLA_ROPE) ** -0.5)
    k_rope = jnp.broadcast_to(_rope(k_r[:, :, None, :], pos), (B, T, MLA_HEADS, MLA_ROPE))
    k = jnp.concatenate([kv[..., :MLA_NOPE], k_rope.astype(kv.dtype)], -1)
    v = kv[..., MLA_NOPE:]
    return _causal_block_attention(q, k, v).reshape(B, T, MLA_HEADS * MLA_DV)


def _rglru(xb, gb, conv_w, conv_b, w_a, b_a, w_x, b_x, lam):
    B, T, W = xb.shape
    xc = lax.conv_general_dilated(xb, conv_w.astype(xb.dtype)[:, None, :], (1,), [(CONV_W - 1, 0)],
                                  dimension_numbers=('NWC', 'WIO', 'NWC'), feature_group_count=W) + conv_b
    xblk = xc.reshape(B, T, LRU_BLOCKS, LRU_BLOCK)
    r = jax.nn.sigmoid((jnp.einsum('btnc,ncd->btnd', xblk, w_a).reshape(B, T, W) + b_a).astype(F32))
    i = jax.nn.sigmoid((jnp.einsum('btnc,ncd->btnd', xblk, w_x).reshape(B, T, W) + b_x).astype(F32))
    log_a = -LRU_C * r * jax.nn.softplus(-lam.astype(F32))
    a = jnp.exp(log_a)
    u = jnp.sqrt(-jnp.expm1(2.0 * log_a)) * (i * xc.astype(F32))
    _, hs = lax.associative_scan(_linear_combine, (a, u), axis=1)
    return (hs * jax.nn.gelu(gb.astype(F32))).astype(xb.dtype)


def _gla(q, k, v, a_low, g, w_a2, b_a2):
    B, T = q.shape[:2]
    Tp = T + PAD_FRONT
    nc = Tp // GLA_CHUNK
    log_alpha = jax.nn.log_sigmoid((a_low @ w_a2 + b_a2).astype(F32)) / GLA_TAU
    log_alpha = log_alpha.reshape(B, T, GLA_HEADS, GLA_DK)

    def chunks(a):
        return _pad_front(a).reshape(B, nc, GLA_CHUNK, GLA_HEADS, a.shape[-1]).transpose(0, 3, 1, 2, 4).astype(F32)

    qc, kc, vc, lc = chunks(q), chunks(k * (GLA_DK ** -0.5)), chunks(v), chunks(log_alpha)
    b = jnp.cumsum(lc, axis=3)
    causal = jnp.tril(jnp.ones((GLA_CHUNK, GLA_CHUNK), bool))[:, :, None]
    diff = b[..., :, None, :] - b[..., None, :, :]
    decay = jnp.where(causal, jnp.exp(jnp.minimum(diff, 0.0)), 0.0)
    A = jnp.einsum('bhnid,bhnjd,bhnijd->bhnij', qc, kc, decay)
    o_intra = jnp.einsum('bhnij,bhnje->bhnie', A, vc)
    b_last = b[..., -1:, :]
    q_in = (qc * jnp.exp(b)).transpose(2, 0, 1, 3, 4)
    chunk_kv = jnp.einsum('bhnjd,bhnje->nbhde', kc * jnp.exp(b_last - b), vc)
    chunk_dec = jnp.exp(b_last[..., 0, :]).transpose(2, 0, 1, 3)

    def step(S, inp):
        q_n, kv_n, dec_n = inp
        return S * dec_n[..., None] + kv_n, jnp.einsum('bhid,bhde->bhie', q_n, S)

    S0 = jnp.zeros((B, GLA_HEADS, GLA_DK, GLA_DV), F32)
    _, o_cross = lax.scan(step, S0, (q_in, chunk_kv, chunk_dec))
    o = o_intra + o_cross.transpose(1, 2, 0, 3, 4)
    o = o.transpose(0, 2, 3, 1, 4).reshape(B, Tp, GLA_HEADS, GLA_DV)[:, PAD_FRONT:]
    o = _rms_norm(o).reshape(B, T, GLA_HEADS * GLA_DV)
    return (jax.nn.swish(g.astype(F32)) * o).astype(g.dtype)


def _mixer(h, pos, w_in, w_merge, b_merge, g_cq, g_ckv, w_uq, w_ukv, conv_w, conv_b, w_lru_a, b_lru_a,
           w_lru_x, b_lru_x, lru_lambda, w_gla_a2, b_gla_a2, w_branch, w_out):
    B, T, _ = h.shape
    (rq, rk, rv, rg, cq, ckv, kr, lx, lg, gq, gk, gv, ga, gg) = jnp.split(h @ w_in, IN_OFFSETS, axis=-1)
    y_ret = _retention(_rope(rq.reshape(B, T, RET_HEADS, RET_DK), pos),
                       _rope(rk.reshape(B, T, RET_HEADS, RET_DK), pos),
                       rv.reshape(B, T, RET_HEADS, RET_DV), rg)
    y_mla = _mla(cq, ckv, kr, g_cq, g_ckv, w_uq, w_ukv, pos)
    y_lru = _rglru(lx, lg, conv_w, conv_b, w_lru_a, b_lru_a, w_lru_x, b_lru_x, lru_lambda)
    y_gla = _gla(gq.reshape(B, T, GLA_HEADS, GLA_DK), gk.reshape(B, T, GLA_HEADS, GLA_DK),
                 gv.reshape(B, T, GLA_HEADS, GLA_DV), ga, gg, w_gla_a2, b_gla_a2)
    ys = jnp.stack([y_ret, y_mla, y_lru, y_gla], axis=2)
    branch = jnp.einsum('btnw,nwd->btnd', ys, w_branch)
    gates = jax.nn.sigmoid((h @ w_merge + b_merge).reshape(B, T, N_BRANCH, D_MODEL))
    return jnp.sum(gates * branch, axis=2) @ w_out


def _moe(h, w_router, b_router, w_eg, b_eg, w_eu, b_eu, w_ed, b_ed):
    B, T, D = h.shape
    x = h.reshape(B * T, D)
    logits = (x @ w_router + b_router).astype(F32)
    top_v, top_i = lax.top_k(logits, TOP_K)
    top_w = jax.nn.softmax(top_v, axis=-1)
    combine = jnp.sum(jax.nn.one_hot(top_i, N_EXPERTS, dtype=F32) * top_w[..., None], axis=1)
    y = jnp.zeros((B * T, D), F32)
    for e in range(N_EXPERTS):
        gate = jnp.minimum(x @ w_eg[e] + b_eg[e], SWIGLU_LIMIT)
        up = jnp.clip(x @ w_eu[e] + b_eu[e], -SWIGLU_LIMIT, SWIGLU_LIMIT)
        act = (up + 1.0) * gate * jax.nn.sigmoid(SWIGLU_ALPHA * gate)
        y = y + combine[:, e:e + 1] * (act @ w_ed[e] + b_ed[e]).astype(F32)
    return y.astype(h.dtype).reshape(B, T, D)


def setup_inputs(seed: int = 0) -> dict:
    key = jax.random.key(seed)
    ks = list(jax.random.split(key, 40))
    L, D, E, F = DEPTH, D_MODEL, N_EXPERTS, D_EXPERT

    def nrm(k, shape, scale):
        return jax.random.normal(k, shape, F32) * scale

    u = jax.random.uniform(ks[13], (L, LRU_WIDTH), F32, 0.9, 0.999)
    a = u ** (1.0 / LRU_C)
    lam = jnp.log(a) - jnp.log1p(-a)
    return {
        'x': nrm(ks[0], (BATCH, SEQ, D), 1.0),
        'meta_tokens': nrm(ks[1], (N_META, D), 1.0),
        'ln0_g': 1.0 + nrm(ks[2], (D,), 0.02),
        'ln0_b': nrm(ks[3], (D,), 0.02),
        'w_in': nrm(ks[4], (L, D, D_IN), D ** -0.5),
        'w_merge': nrm(ks[5], (L, D, N_BRANCH * D), D ** -0.5),
        'b_merge': nrm(ks[6], (L, N_BRANCH * D), 0.02),
        'g_cq': 1.0 + nrm(ks[7], (L, MLA_Q_RANK), 0.02),
        'g_ckv': 1.0 + nrm(ks[8], (L, MLA_KV_RANK), 0.02),
        'w_uq': nrm(ks[9], (L, MLA_Q_RANK, MLA_HEADS * (MLA_NOPE + MLA_ROPE)), MLA_Q_RANK ** -0.5),
        'w_ukv': nrm(ks[10], (L, MLA_KV_RANK, MLA_HEADS * (MLA_NOPE + MLA_DV)), MLA_KV_RANK ** -0.5),
        'conv_w': nrm(ks[11], (L, CONV_W, LRU_WIDTH), CONV_W ** -0.5),
        'conv_b': nrm(ks[12], (L, LRU_WIDTH), 0.02),
        'w_lru_a': nrm(ks[14], (L, LRU_BLOCKS, LRU_BLOCK, LRU_BLOCK), LRU_BLOCK ** -0.5),
        'b_lru_a': nrm(ks[15], (L, LRU_WIDTH), 0.02),
        'w_lru_x': nrm(ks[16], (L, LRU_BLOCKS, LRU_BLOCK, LRU_BLOCK), LRU_BLOCK ** -0.5),
        'b_lru_x': nrm(ks[17], (L, LRU_WIDTH), 0.02),
        'lru_lambda': lam,
        'w_gla_a2': nrm(ks[18], (L, GLA_RANK, GLA_HEADS * GLA_DK), GLA_RANK ** -0.5),
        'b_gla_a2': nrm(ks[19], (L, GLA_HEADS * GLA_DK), 0.02),
        'w_branch': nrm(ks[20], (L, N_BRANCH, BRANCH_W, D), BRANCH_W ** -0.5),
        'w_out': nrm(ks[21], (L, D, D), DN_BETA * D ** -0.5),
        'ln1_g': 1.0 + nrm(ks[22], (L, D), 0.02),
        'ln1_b': nrm(ks[23], (L, D), 0.02),
        'w_router': nrm(ks[24], (L, D, E), D ** -0.5),
        'b_router': nrm(ks[25], (L, E), 0.01),
        'w_exp_gate': nrm(ks[26], (L, E, D, F), D ** -0.5),
        'b_exp_gate': nrm(ks[27], (L, E, F), 0.02),
        'w_exp_up': nrm(ks[28], (L, E, D, F), D ** -0.5),
        'b_exp_up': nrm(ks[29], (L, E, F), 0.02),
        'w_exp_down': nrm(ks[30], (L, E, F, D), DN_BETA * F ** -0.5),
        'b_exp_down': nrm(ks[31], (L, E, D), 0.02),
        'ln2_g': 1.0 + nrm(ks[32], (L, D), 0.02),
        'ln2_b': nrm(ks[33], (L, D), 0.02),
    }


def reference(x, meta_tokens, ln0_g, ln0_b, w_in, w_merge, b_merge, g_cq, g_ckv, w_uq, w_ukv, conv_w, conv_b,
              w_lru_a, b_lru_a, w_lru_x, b_lru_x, lru_lambda, w_gla_a2, b_gla_a2, w_branch, w_out, ln1_g, ln1_b,
              w_router, b_router, w_exp_gate, b_exp_gate, w_exp_up, b_exp_up, w_exp_down, b_exp_down,
              ln2_g, ln2_b):
    B = x.shape[0]
    meta = jnp.broadcast_to(meta_tokens[None].astype(x.dtype), (B, N_META, D_MODEL))
    h = _layer_norm(jnp.concatenate([meta, x], axis=1), ln0_g, ln0_b)
    pos = jnp.arange(h.shape[1], dtype=jnp.int32)
    for l in range(DEPTH):
        mix = _mixer(h, pos, w_in[l], w_merge[l], b_merge[l], g_cq[l], g_ckv[l], w_uq[l], w_ukv[l],
                     conv_w[l], conv_b[l], w_lru_a[l], b_lru_a[l], w_lru_x[l], b_lru_x[l], lru_lambda[l],
                     w_gla_a2[l], b_gla_a2[l], w_branch[l], w_out[l])
        h = _layer_norm(DN_ALPHA * h + mix, ln1_g[l], ln1_b[l])
        ffn = _moe(h, w_router[l], b_router[l], w_exp_gate[l], b_exp_gate[l], w_exp_up[l], b_exp_up[l],
                   w_exp_down[l], b_exp_down[l])
        h = _layer_norm(DN_ALPHA * h + ffn, ln2_g[l], ln2_b[l])
    return h[:, N_META:]
```

```python
import functools

import jax
import jax.numpy as jnp
from jax import lax
from jax.experimental import pallas as pl
from jax.experimental.pallas import tpu as pltpu

F32 = jnp.float32
BF16 = jnp.bfloat16

D_MODEL = 1024
N_META = 16
CHUNK = 128
PAD_FRONT = CHUNK - N_META
ROPE_BASE = 10000.0
NEG_BIG = -1e30

RET_HEADS, RET_DK, RET_DV = 4, 64, 128
MLA_HEADS, MLA_Q_RANK, MLA_KV_RANK, MLA_NOPE, MLA_ROPE, MLA_DV = 8, 256, 128, 64, 32, 64
LRU_WIDTH, LRU_BLOCKS, CONV_W, LRU_C = 512, 8, 4, 8.0
LRU_BLOCK = LRU_WIDTH // LRU_BLOCKS
GLA_HEADS, GLA_DK, GLA_DV, GLA_RANK, GLA_TAU, GLA_CHUNK = 4, 64, 128, 16, 16.0, 16
N_BRANCH, BRANCH_W = 4, 512
N_EXPERTS, TOP_K, D_EXPERT = 32, 4, 1024
SWIGLU_LIMIT, SWIGLU_ALPHA = 7.0, 1.702

LANES = 128
VMEM_LIMIT = 56 * 1024 * 1024

_O_RQ, _O_RK, _O_RV, _O_RG = 0, 256, 512, 1024
_O_CQ, _O_CKV, _O_KR = 1536, 1792, 1920
_O_LX, _O_LG = 1952, 2464
_O_GQ, _O_GK, _O_GV, _O_GA, _O_GG = 2976, 3232, 3488, 4000, 4016

P16_W = 3200
P32_W = 2176


def _cparams(sem):
    return pltpu.CompilerParams(dimension_semantics=sem, vmem_limit_bytes=VMEM_LIMIT)


def _const_spec(shape):
    nd = len(shape)
    return pl.BlockSpec(shape, lambda *_: (0,) * nd, pipeline_mode=pl.Buffered(1))


def _ln_rows(x, g, b, eps=1e-5):
    mu = jnp.mean(x, -1, keepdims=True)
    xc = x - mu
    var = jnp.mean(xc * xc, -1, keepdims=True)
    return xc * lax.rsqrt(var + eps) * g + b


def _dot(a, b):
    return jnp.dot(a, b, preferred_element_type=F32)


def _dot_nt(a, b):
    return lax.dot_general(a, b, (((1,), (1,)), ((), ())), preferred_element_type=F32)


def _ln0_kernel(x_ref, meta_ref, g_ref, b_ref, o_ref):
    i = pl.program_id(1)
    g = g_ref[...]
    b = b_ref[...]

    @pl.when(i == 0)
    def _():
        o_ref[0, :PAD_FRONT, :] = jnp.zeros((PAD_FRONT, D_MODEL), F32)
        o_ref[0, PAD_FRONT:, :] = _ln_rows(meta_ref[...], g, b)

    @pl.when(i > 0)
    def _():
        o_ref[0] = _ln_rows(x_ref[0], g, b)


def _ln0(x, meta, g, b):
    B, S, D = x.shape
    nc = S // CHUNK + 1
    return pl.pallas_call(
        _ln0_kernel,
        grid=(B, nc),
        in_specs=[pl.BlockSpec((1, CHUNK, D), lambda bb, i: (bb, jnp.maximum(i - 1, 0), 0)),
                  pl.BlockSpec((N_META, D), lambda bb, i: (0, 0)),
                  pl.BlockSpec((1, D), lambda bb, i: (0, 0)),
                  pl.BlockSpec((1, D), lambda bb, i: (0, 0))],
        out_specs=pl.BlockSpec((1, CHUNK, D), lambda bb, i: (bb, i, 0)),
        out_shape=jax.ShapeDtypeStruct((B, nc * CHUNK, D), F32),
        name="ln0",
        compiler_params=_cparams(("parallel", "arbitrary")),
    )(x, meta, g.reshape(1, D), b.reshape(1, D))


def _proj_kernel(h_ref, w_ref, o_ref, *, col_step):
    h = h_ref[...].astype(BF16)
    n = w_ref.shape[1]
    for c0 in range(0, n, col_step):
        c1 = min(c0 + col_step, n)
        o_ref[:, c0:c1] = _dot(h, w_ref[:, c0:c1]).astype(o_ref.dtype)


def _proj(h, w, out_dtype, tm):
    n_rows, d = h.shape
    n = w.shape[1]
    return pl.pallas_call(
        functools.partial(_proj_kernel, col_step=512),
        grid=(n_rows // tm,),
        in_specs=[pl.BlockSpec((tm, d), lambda i: (i, 0)), _const_spec((d, n))],
        out_specs=pl.BlockSpec((tm, n), lambda i: (i, 0)),
        out_shape=jax.ShapeDtypeStruct((n_rows, n), out_dtype),
        name="in_proj",
        compiler_params=_cparams(("parallel",)),
    )(h, w)


def _half_masks(dtype):
    lane = lax.broadcasted_iota(jnp.int32, (1, LANES), 1)
    lo = (lane < LANES // 2).astype(dtype)
    return lo, (1 - lo).astype(dtype)


def _ret_kernel(cdec_ref, qk_ref, rot_ref, v_ref, g_ref, cos_ref, sin_ref, dec_ref, dmat_ref, o_ref, s_ref):
    c = pl.program_id(1)

    @pl.when(c == 0)
    def _():
        s_ref[...] = jnp.zeros_like(s_ref)

    r = qk_ref[...].astype(F32) * cos_ref[...] + rot_ref[...].astype(F32) * sin_ref[...]
    rd = r * dec_ref[...]
    masks = _half_masks(F32)
    for hd in range(RET_HEADS):
        pair, half = divmod(hd, 2)
        m = masks[half]
        q_lo, k_lo = pair * LANES, 2 * LANES + pair * LANES
        qp = (r[:, q_lo:q_lo + LANES] * m).astype(BF16)
        kp = r[:, k_lo:k_lo + LANES].astype(BF16)
        vh = v_ref[:, hd * RET_DV:(hd + 1) * RET_DV]
        scores = _dot_nt(qp, kp) * dmat_ref[hd]
        o = _dot(scores.astype(BF16), vh)
        qin = (rd[:, q_lo:q_lo + LANES] * m).astype(BF16)
        s_old = s_ref[hd]
        o = o + _dot(qin, s_old.astype(BF16))
        kdec_t = (rd[:, k_lo:k_lo + LANES] * m).T.astype(BF16)
        s_ref[hd] = s_old * cdec_ref[hd] + _dot(kdec_t, vh)
        mu = jnp.mean(o, -1, keepdims=True)
        oc = o - mu
        var = jnp.mean(oc * oc, -1, keepdims=True)
        on = oc * lax.rsqrt(var + 1e-5)
        gate = g_ref[:, hd * RET_DV:(hd + 1) * RET_DV]
        o_ref[:, hd * RET_DV:(hd + 1) * RET_DV] = (gate * jax.nn.sigmoid(gate) * on).astype(o_ref.dtype)


def _retention(p16, p32, tabs, B, nc):
    n_rows = p16.shape[0]
    row = lambda b, c: b * nc + c
    return pl.pallas_call(
        _ret_kernel,
        grid=(B, nc),
        in_specs=[pl.BlockSpec(memory_space=pltpu.SMEM),
                  pl.BlockSpec((CHUNK, 512), lambda b, c: (row(b, c), 0)),
                  pl.BlockSpec((CHUNK, 512), lambda b, c: (row(b, c), 1)),
                  pl.BlockSpec((CHUNK, 512), lambda b, c: (row(b, c), 2)),
                  pl.BlockSpec((CHUNK, 512), lambda b, c: (row(b, c), 0)),
                  pl.BlockSpec((CHUNK, 512), lambda b, c: (c, 0)),
                  pl.BlockSpec((CHUNK, 512), lambda b, c: (c, 0)),
                  _const_spec((CHUNK, 512)),
                  _const_spec((RET_HEADS, CHUNK, CHUNK))],
        out_specs=pl.BlockSpec((CHUNK, 512), lambda b, c: (row(b, c), 0)),
        out_shape=jax.ShapeDtypeStruct((n_rows, 512), BF16),
        scratch_shapes=[pltpu.VMEM((RET_HEADS, LANES, RET_DV), F32)],
        name="retention",
        compiler_params=_cparams(("parallel", "arbitrary")),
    )(tabs["ret_cdec"], p16, p16, p16, p32, tabs["ret_cos"], tabs["ret_sin"], tabs["ret_dec"], tabs["ret_dmat"])


def _mla_prep_kernel(lat_ref, krot_ref, cos_ref, sin_ref, gq_ref, gkv_ref, wq_ref, wqr_ref, wkv_ref,
                     q_ref, k_ref, v_ref):
    lat = lat_ref[...].astype(F32)
    cq = lat[:, :MLA_Q_RANK]
    ckv = lat[:, MLA_Q_RANK:MLA_Q_RANK + MLA_KV_RANK]
    kr = lat[:, MLA_Q_RANK + MLA_KV_RANK:]
    cos = cos_ref[...]
    sin = sin_ref[...]
    cqn = (cq * lax.rsqrt(jnp.mean(cq * cq, -1, keepdims=True) + 1e-6) * gq_ref[...]).astype(BF16)
    ckvn = (ckv * lax.rsqrt(jnp.mean(ckv * ckv, -1, keepdims=True) + 1e-6) * gkv_ref[...]).astype(BF16)
    krr = kr * cos + krot_ref[...].astype(F32) * sin
    scale = (MLA_NOPE + MLA_ROPE) ** -0.5
    lane = lax.broadcasted_iota(jnp.int32, (1, LANES), 1)
    is_nope = lane < MLA_NOPE
    for hd in range(MLA_HEADS):
        sl = slice(hd * LANES, (hd + 1) * LANES)
        q = _dot(cqn, wq_ref[:, sl]) * cos + _dot(cqn, wqr_ref[:, sl]) * sin
        q_ref[0, hd] = (q * scale).astype(q_ref.dtype)
        kv = _dot(ckvn, wkv_ref[:, sl])
        k_ref[0, hd] = jnp.where(is_nope, kv, krr).astype(k_ref.dtype)
        v_ref[0, hd] = kv.astype(v_ref.dtype)


def _mla_prep(p16, tabs, wts, B, tp, tr):
    nt = tp // tr
    row = lambda b, i: b * nt + i
    hm = jax.ShapeDtypeStruct((B, MLA_HEADS, tp, LANES), BF16)
    hspec = pl.BlockSpec((1, MLA_HEADS, tr, LANES), lambda b, i: (b, 0, i, 0))
    return pl.pallas_call(
        _mla_prep_kernel,
        grid=(B, nt),
        in_specs=[pl.BlockSpec((tr, 512), lambda b, i: (row(b, i), 3)),
                  pl.BlockSpec((tr, LANES), lambda b, i: (row(b, i), 3072 // LANES)),
                  pl.BlockSpec((tr, LANES), lambda b, i: (i, 0)),
                  pl.BlockSpec((tr, LANES), lambda b, i: (i, 0)),
                  _const_spec((1, MLA_Q_RANK)), _const_spec((1, MLA_KV_RANK)),
                  _const_spec((MLA_Q_RANK, MLA_HEADS * LANES)), _const_spec((MLA_Q_RANK, MLA_HEADS * LANES)),
                  _const_spec((MLA_KV_RANK, MLA_HEADS * LANES))],
        out_specs=[hspec, hspec, hspec],
        out_shape=[hm, hm, hm],
        name="mla_prep",
        compiler_params=_cparams(("parallel", "arbitrary")),
    )(p16, p16, tabs["mla_cos"], tabs["mla_sin"], wts["g_cq"], wts["g_ckv"], wts["w_uq"], wts["w_uq_rot"],
      wts["w_ukv"])


def _flash_kernel(q_ref, k_ref, v_ref, o_ref, *, tq, tk):
    tp = q_ref.shape[2]
    nq = tp // tq
    outs = []
    for hh in range(2):
        def q_block(qi, _):
            q0 = pl.multiple_of(qi * tq, tq)
            q = q_ref[0, hh, pl.ds(q0, tq), :]
            qpos = q0 + lax.broadcasted_iota(jnp.int32, (tq, tk), 0)

            def kv_block(kj, carry):
                m, l, acc = carry
                k0 = pl.multiple_of(kj * tk, tk)
                k = k_ref[0, hh, pl.ds(k0, tk), :]
                v = v_ref[0, hh, pl.ds(k0, tk), :]
                s = _dot_nt(q, k)
                kpos = k0 + lax.broadcasted_iota(jnp.int32, (tq, tk), 1)
                s = jnp.where((kpos <= qpos) & (kpos >= PAD_FRONT), s, NEG_BIG)
                m_new = jnp.maximum(m, jnp.max(s, -1, keepdims=True))
                alpha = jnp.exp(m - m_new)
                p = jnp.exp(s - m_new)
                l_new = alpha * l + jnp.sum(p, -1, keepdims=True)
                acc_new = alpha * acc + _dot(p.astype(BF16), v)
                return m_new, l_new, acc_new

            init = (jnp.full((tq, 1), NEG_BIG, F32), jnp.zeros((tq, 1), F32), jnp.zeros((tq, LANES), F32))
            n_kv = (q0 + tq + tk - 1) // tk
            m, l, acc = lax.fori_loop(0, n_kv, kv_block, init)
            o = acc / l
            if hh == 0:
                o_ref[pl.ds(q0, tq), :] = pltpu.roll(o, LANES // 2, axis=1)
            else:
                lane = lax.broadcasted_iota(jnp.int32, (tq, LANES), 1)
                o_ref[pl.ds(q0, tq), :] = jnp.where(lane < LANES // 2, o_ref[pl.ds(q0, tq), :], o)
            return 0

        lax.fori_loop(0, nq, q_block, 0)


def _flash(qh, kh, vh, tq, tk):
    B, H, tp, _ = qh.shape
    npair = H // 2
    spec = pl.BlockSpec((1, 2, tp, LANES), lambda b, j: (b, j, 0, 0))
    return pl.pallas_call(
        functools.partial(_flash_kernel, tq=tq, tk=tk),
        grid=(B, npair),
        in_specs=[spec, spec, spec],
        out_specs=pl.BlockSpec((tp, LANES), lambda b, j: (b, j)),
        out_shape=jax.ShapeDtypeStruct((B * tp, npair * LANES), F32),
        name="mla_flash",
        compiler_params=_cparams(("parallel", "arbitrary")),
    )(qh, kh, vh)


def _lru_kernel(x_ref, g_ref, cw_ref, cb_ref, wab_ref, bab_ref, ncsp_ref, o_ref, hist_ref, h_ref):
    c = pl.program_id(1)
    tr = x_ref.shape[0]

    @pl.when(c == 0)
    def _():
        hist_ref[...] = jnp.zeros_like(hist_ref)
        h_ref[...] = jnp.zeros_like(h_ref)

    x = x_ref[...]
    xcat = jnp.concatenate([hist_ref[...], x], axis=0)
    hist_ref[...] = x[tr - 8:, :]
    xc = cb_ref[...] + cw_ref[CONV_W - 1:CONV_W, :] * x
    for j in range(CONV_W - 1):
        off = 8 - (CONV_W - 1) + j
        xc = xc + cw_ref[j:j + 1, :] * xcat[off:off + tr, :]
    z = _dot(xc.astype(BF16), wab_ref[...]) + bab_ref[...]
    r = jax.nn.sigmoid(z[:, :LRU_WIDTH])
    i = jax.nn.sigmoid(z[:, LRU_WIDTH:])
    log_a = ncsp_ref[...] * r
    a = jnp.exp(log_a)
    u = jnp.sqrt(1.0 - jnp.exp(2.0 * log_a)) * (i * xc)
    row = lax.broadcasted_iota(jnp.int32, (tr, LRU_WIDTH), 0)
    u = jnp.where((c == 0) & (row < PAD_FRONT), 0.0, u)
    s = 1
    while s < tr:
        keep = row >= s
        a_sh = jnp.where(keep, pltpu.roll(a, s, axis=0), 1.0)
        u_sh = jnp.where(keep, pltpu.roll(u, s, axis=0), 0.0)
        u = a * u_sh + u
        a = a * a_sh
        s *= 2
    hs = u + a * h_ref[...]
    h_ref[...] = hs[tr - 1:tr, :]
    o_ref[...] = (hs * jax.nn.gelu(g_ref[...])).astype(o_ref.dtype)


def _lru(p32, wts, B, nc):
    n_rows = p32.shape[0]
    row = lambda b, c: b * nc + c
    return pl.pallas_call(
        _lru_kernel,
        grid=(B, nc),
        in_specs=[pl.BlockSpec((CHUNK, 512), lambda b, c: (row(b, c), 1)),
                  pl.BlockSpec((CHUNK, 512), lambda b, c: (row(b, c), 2)),
                  _const_spec((CONV_W, LRU_WIDTH)), _const_spec((1, LRU_WIDTH)),
                  _const_spec((LRU_WIDTH, 2 * LRU_WIDTH)), _const_spec((1, 2 * LRU_WIDTH)),
                  _const_spec((1, LRU_WIDTH))],
        out_specs=pl.BlockSpec((CHUNK, 512), lambda b, c: (row(b, c), 0)),
        out_shape=jax.ShapeDtypeStruct((n_rows, 512), BF16),
        scratch_shapes=[pltpu.VMEM((8, LRU_WIDTH), F32), pltpu.VMEM((1, LRU_WIDTH), F32)],
        name="rglru",
        compiler_params=_cparams(("parallel", "arbitrary")),
    )(p32, p32, wts["conv_w"], wts["conv_b"], wts["w_lru_ab"], wts["b_lru_ab"], wts["lru_ncsp"])


def _gla_kernel(qk_ref, v_ref, a_ref, g_ref, wa2_ref, ba2_ref, o_ref, st_ref):
    c = pl.program_id(1)
    tr = qk_ref.shape[0]
    ncs = tr // GLA_CHUNK
    hk = GLA_HEADS * GLA_DK

    @pl.when(c == 0)
    def _():
        st_ref[...] = jnp.zeros_like(st_ref)

    x = _dot(a_ref[...].astype(BF16), wa2_ref[...]) + ba2_ref[...]
    la = (jnp.minimum(x, 0.0) - jnp.log(1.0 + jnp.exp(-jnp.abs(x)))) * (1.0 / GLA_TAU)
    row = lax.broadcasted_iota(jnp.int32, (tr, hk), 0)
    rin = row & (GLA_CHUNK - 1)
    b = la
    sfx = la
    s = 1
    while s < GLA_CHUNK:
        b = b + jnp.where(rin >= s, pltpu.roll(b, s, axis=0), 0.0)
        sfx = sfx + jnp.where(rin < GLA_CHUNK - s, pltpu.roll(sfx, tr - s, axis=0), 0.0)
        s *= 2
    blast = b + sfx - la
    qk = qk_ref[...].astype(F32)
    q = qk[:, :hk]
    k = qk[:, hk:]
    qe = q * jnp.exp(b)
    ke = (k * jnp.exp(-b)).astype(BF16)
    kl = (k * jnp.exp(blast - b)).astype(BF16)
    cdec = jnp.exp(blast)
    masks = _half_masks(F32)
    ri = lax.broadcasted_iota(jnp.int32, (tr, tr), 0)
    ci = lax.broadcasted_iota(jnp.int32, (tr, tr), 1)
    sh = GLA_CHUNK.bit_length() - 1
    cchunk = ci >> sh
    intra = ((ri >> sh) == cchunk) & (ci <= ri)
    for hd in range(GLA_HEADS):
        pair, half = divmod(hd, 2)
        lo = pair * LANES
        qm = (qe[:, lo:lo + LANES] * masks[half]).astype(BF16)
        amat = jnp.where(intra, _dot_nt(qm, ke[:, lo:lo + LANES]), 0.0)
        vh = v_ref[:, hd * GLA_DV:(hd + 1) * GLA_DV]
        o = _dot(amat.astype(BF16), vh)
        vt = vh.astype(F32).T
        st = st_ref[hd]
        cross = []
        for n in range(ncs):
            r0 = n * GLA_CHUNK
            cross.append(_dot_nt(qm[r0:r0 + GLA_CHUNK, :], st.astype(BF16)))
            vt_n = jnp.where(cchunk == n, vt, 0.0).astype(BF16)
            st = st * cdec[r0:r0 + 1, lo:lo + LANES] + _dot(vt_n, kl[:, lo:lo + LANES])
        st_ref[hd] = st
        o = o + jnp.concatenate(cross, axis=0)
        on = o * lax.rsqrt(jnp.mean(o * o, -1, keepdims=True) + 1e-6)
        gate = g_ref[:, hd * GLA_DV:(hd + 1) * GLA_DV]
        o_ref[:, hd * GLA_DV:(hd + 1) * GLA_DV] = (gate * jax.nn.sigmoid(gate) * on).astype(o_ref.dtype)


def _gla(p16, p32, wts, B, nc):
    n_rows = p16.shape[0]
    row = lambda b, c: b * nc + c
    return pl.pallas_call(
        _gla_kernel,
        grid=(B, nc),
        in_specs=[pl.BlockSpec((CHUNK, 512), lambda b, c: (row(b, c), 4)),
                  pl.BlockSpec((CHUNK, 512), lambda b, c: (row(b, c), 5)),
                  pl.BlockSpec((CHUNK, LANES), lambda b, c: (row(b, c), 2048 // LANES)),
                  pl.BlockSpec((CHUNK, 512), lambda b, c: (row(b, c), 3)),
                  _const_spec((LANES, GLA_HEADS * GLA_DK)), _const_spec((1, GLA_HEADS * GLA_DK))],
        out_specs=pl.BlockSpec((CHUNK, 512), lambda b, c: (row(b, c), 0)),
        out_shape=jax.ShapeDtypeStruct((n_rows, 512), BF16),
        scratch_shapes=[pltpu.VMEM((GLA_HEADS, GLA_DV, LANES), F32)],
        name="gla",
        compiler_params=_cparams(("parallel", "arbitrary")),
    )(p16, p16, p32, p32, wts["w_gla_a2"], wts["b_gla_a2"])


def _merge_kernel(h_ref, y0_ref, y1_ref, y2_ref, y3_ref, wm_ref, bm_ref, wb_ref, wo_ref, g_ref, b_ref, wr_ref, br_ref,
                  h1_ref, comb_ref, *, alpha):
    h = h_ref[...]
    h16 = h.astype(BF16)
    ys = (y0_ref[...], y1_ref[...].astype(BF16), y2_ref[...], y3_ref[...])
    mixed = None
    for n in range(N_BRANCH):
        sl = slice(n * D_MODEL, (n + 1) * D_MODEL)
        gate = jax.nn.sigmoid(_dot(h16, wm_ref[:, sl]) + bm_ref[:, sl])
        term = gate * _dot(ys[n], wb_ref[n])
        mixed = term if mixed is None else mixed + term
    mix = _dot(mixed.astype(BF16), wo_ref[...])
    h1 = _ln_rows(alpha * h + mix, g_ref[...], b_ref[...])
    h1_ref[...] = h1
    logits = _dot(h1.astype(BF16), wr_ref[...]) + br_ref[...]
    lane = lax.broadcasted_iota(jnp.int32, logits.shape, 1)
    vals, sels = [], []
    cur = logits
    for _ in range(TOP_K):
        mx = jnp.max(cur, -1, keepdims=True)
        idx = jnp.min(jnp.where(cur == mx, lane, LANES), -1, keepdims=True)
        sel = lane == idx
        vals.append(mx)
        sels.append(sel)
        cur = jnp.where(sel, -jnp.inf, cur)
    es = [jnp.exp(v - vals[0]) for v in vals]
    den = es[0] + es[1] + es[2] + es[3]
    comb = jnp.zeros_like(logits)
    for e, sel in zip(es, sels):
        comb = comb + jnp.where(sel, e / den, 0.0)
    comb_ref[...] = comb


def _merge(h, ys, wts, alpha, tm):
    n_rows = h.shape[0]
    rspec = lambda w: pl.BlockSpec((tm, w), lambda i: (i, 0))
    return pl.pallas_call(
        functools.partial(_merge_kernel, alpha=alpha),
        grid=(n_rows // tm,),
        in_specs=[rspec(D_MODEL), rspec(512), rspec(512), rspec(512), rspec(512),
                  _const_spec((D_MODEL, N_BRANCH * D_MODEL)), _const_spec((1, N_BRANCH * D_MODEL)),
                  _const_spec((N_BRANCH, BRANCH_W, D_MODEL)), _const_spec((D_MODEL, D_MODEL)),
                  _const_spec((1, D_MODEL)), _const_spec((1, D_MODEL)),
                  _const_spec((D_MODEL, LANES)), _const_spec((1, LANES))],
        out_specs=[rspec(D_MODEL), rspec(LANES)],
        out_shape=[jax.ShapeDtypeStruct((n_rows, D_MODEL), F32), jax.ShapeDtypeStruct((n_rows, LANES), F32)],
        name="merge_ln1_router",
        compiler_params=_cparams(("parallel",)),
    )(h, *ys, wts["w_merge"], wts["b_merge"], wts["w_branch"], wts["w_out"], wts["ln1_g"], wts["ln1_b"],
      wts["w_router"], wts["b_router"])


def _moe_kernel(x_ref, comb_ref, wg_ref, bg_ref, wu_ref, bu_ref, wd_ref, bd_ref, g_ref, b_ref, o_ref, acc_ref, *, alpha,
                tiles_per_seq):
    e = pl.program_id(1)

    @pl.when(e == 0)
    def _():
        acc_ref[...] = jnp.zeros_like(acc_ref)

    x16 = x_ref[...].astype(BF16)
    gate = jnp.minimum(_dot(x16, wg_ref[0].astype(BF16)) + bg_ref[0], SWIGLU_LIMIT)
    up = jnp.clip(_dot(x16, wu_ref[0].astype(BF16)) + bu_ref[0], -SWIGLU_LIMIT, SWIGLU_LIMIT)
    act = (up + 1.0) * gate * jax.nn.sigmoid(SWIGLU_ALPHA * gate)
    y = _dot(act.astype(BF16), wd_ref[0].astype(BF16)) + bd_ref[0]
    lane = lax.broadcasted_iota(jnp.int32, comb_ref.shape, 1)
    w = jnp.sum(jnp.where(lane == e, comb_ref[...], 0.0), -1, keepdims=True)
    acc_ref[...] += w * y

    @pl.when(e == pl.num_programs(1) - 1)
    def _():
        out = _ln_rows(alpha * x_ref[...] + acc_ref[...], g_ref[...], b_ref[...])
        row = lax.broadcasted_iota(jnp.int32, out.shape, 0)
        first_tile = lax.rem(pl.program_id(0), tiles_per_seq) == 0
        o_ref[...] = jnp.where(first_tile & (row < PAD_FRONT), 0.0, out)


def _moe(h1, comb, wts, alpha, tm, tp):
    n_rows = h1.shape[0]
    assert tp % tm == 0 and tm >= PAD_FRONT
    wspec = lambda: pl.BlockSpec((1, D_MODEL, D_EXPERT), lambda i, e: (e, 0, 0))
    bspec = lambda w: pl.BlockSpec((1, 1, w), lambda i, e: (e, 0, 0))
    return pl.pallas_call(
        functools.partial(_moe_kernel, alpha=alpha, tiles_per_seq=tp // tm),
        grid=(n_rows // tm, N_EXPERTS),
        in_specs=[pl.BlockSpec((tm, D_MODEL), lambda i, e: (i, 0)),
                  pl.BlockSpec((tm, LANES), lambda i, e: (i, 0)),
                  wspec(), bspec(D_EXPERT), wspec(), bspec(D_EXPERT),
                  pl.BlockSpec((1, D_EXPERT, D_MODEL), lambda i, e: (e, 0, 0)), bspec(D_MODEL),
                  pl.BlockSpec((1, D_MODEL), lambda i, e: (0, 0)), pl.BlockSpec((1, D_MODEL), lambda i, e: (0, 0))],
        out_specs=pl.BlockSpec((tm, D_MODEL), lambda i, e: (i, 0)),
        out_shape=jax.ShapeDtypeStruct((n_rows, D_MODEL), F32),
        scratch_shapes=[pltpu.VMEM((tm, D_MODEL), F32)],
        name="moe_dense_ln2",
        compiler_params=_cparams(("parallel", "arbitrary")),
    )(h1, comb, wts["w_exp_gate"], wts["b_exp_gate"], wts["w_exp_up"], wts["b_exp_up"], wts["w_exp_down"],
      wts["b_exp_down"], wts["ln2_g"], wts["ln2_b"])


def _rot_half_cols(w, d):
    n = w.shape[1] // d
    w3 = w.reshape(w.shape[0], n, d)
    return jnp.concatenate([-w3[..., d // 2:], w3[..., :d // 2]], -1).reshape(w.shape)


def _tables(tp):
    pos = jnp.maximum(jnp.arange(tp, dtype=jnp.int32) - PAD_FRONT, 0).astype(F32)

    def cs(d):
        inv = ROPE_BASE ** (-jnp.arange(0, d, 2, dtype=F32) / d)
        ang = pos[:, None] * inv[None, :]
        return jnp.cos(ang), jnp.sin(ang)

    rc, rs = cs(RET_DK)
    ret_cos = jnp.tile(jnp.concatenate([rc, rc], -1), (1, 2 * RET_HEADS))
    ret_sin = jnp.tile(jnp.concatenate([rs, rs], -1), (1, 2 * RET_HEADS))
    mc, ms = cs(MLA_ROPE)
    ones = jnp.ones((tp, MLA_NOPE), F32)
    tail1 = jnp.ones((tp, LANES - MLA_NOPE - MLA_ROPE), F32)
    mla_cos = jnp.concatenate([ones, mc, mc, tail1], -1)
    mla_sin = jnp.concatenate([0 * ones, ms, ms, 0 * tail1], -1)
    log_gamma = jnp.log1p(-jnp.exp2(-5.0 - jnp.arange(RET_HEADS, dtype=F32)))
    idx = jnp.arange(CHUNK, dtype=F32)
    rel = idx[:, None] - idx[None, :]
    dmat = jnp.where(rel >= 0, jnp.exp(log_gamma[:, None, None] * jnp.maximum(rel, 0.0)), 0.0)
    k_dec = jnp.exp(log_gamma[:, None] * (CHUNK - 1.0 - idx))
    q_dec = jnp.exp(log_gamma[:, None] * (idx + 1.0))
    rep = lambda t: jnp.repeat(t.T, RET_DK, axis=1)
    ret_dec = jnp.concatenate([rep(q_dec), rep(k_dec)], -1)
    ret_cdec = jnp.exp(log_gamma * CHUNK)
    return dict(ret_cos=ret_cos, ret_sin=ret_sin, mla_cos=mla_cos, mla_sin=mla_sin, ret_dmat=dmat, ret_dec=ret_dec,
                ret_cdec=ret_cdec)


def _layer_weights(w_in, w_merge, b_merge, g_cq, g_ckv, w_uq, w_ukv, conv_w, conv_b, w_lru_a, b_lru_a, w_lru_x,
                   b_lru_x, lru_lambda, w_gla_a2, b_gla_a2, w_branch, w_out, ln1_g, ln1_b, w_router, b_router,
                   w_exp_gate, b_exp_gate, w_exp_up, b_exp_up, w_exp_down, b_exp_down, ln2_g, ln2_b):
    D = D_MODEL
    z = lambda n: jnp.zeros((D, n), F32)
    rqk = jnp.concatenate([w_in[:, _O_RQ:_O_RK], w_in[:, _O_RK:_O_RV] * (RET_DK ** -0.5)], -1)
    kr = w_in[:, _O_KR:_O_LX]
    kr_slot = jnp.concatenate([z(MLA_NOPE), kr, z(LANES - MLA_NOPE - MLA_ROPE)], -1)
    kr_rot_slot = jnp.concatenate([z(MLA_NOPE), _rot_half_cols(kr, MLA_ROPE), z(LANES - MLA_NOPE - MLA_ROPE)], -1)
    gqk = jnp.concatenate([w_in[:, _O_GQ:_O_GK], w_in[:, _O_GK:_O_GV] * (GLA_DK ** -0.5)], -1)
    w16 = jnp.concatenate([rqk, _rot_half_cols(rqk, RET_DK), w_in[:, _O_RV:_O_RG],
                           w_in[:, _O_CQ:_O_KR], kr_slot, gqk, w_in[:, _O_GV:_O_GA], kr_rot_slot], -1).astype(BF16)
    w32 = jnp.concatenate([w_in[:, _O_RG:_O_CQ], w_in[:, _O_LX:_O_LG], w_in[:, _O_LG:_O_GQ], w_in[:, _O_GG:],
                           w_in[:, _O_GA:_O_GG], z(LANES - GLA_RANK)], -1).astype(BF16)
    hq = MLA_NOPE + MLA_ROPE
    uq3 = w_uq.reshape(MLA_Q_RANK, MLA_HEADS, hq)
    zq = jnp.zeros((MLA_Q_RANK, MLA_HEADS, LANES - hq), F32)
    uq = jnp.concatenate([uq3, zq], -1).reshape(MLA_Q_RANK, MLA_HEADS * LANES)
    uq_rope_rot = _rot_half_cols(uq3[..., MLA_NOPE:].reshape(MLA_Q_RANK, -1), MLA_ROPE).reshape(MLA_Q_RANK, MLA_HEADS, MLA_ROPE)
    uqr = jnp.concatenate([jnp.zeros((MLA_Q_RANK, MLA_HEADS, MLA_NOPE), F32), uq_rope_rot, zq], -1)
    uqr = uqr.reshape(MLA_Q_RANK, MLA_HEADS * LANES)
    eye = jnp.eye(LRU_BLOCKS, dtype=F32)
    bd = lambda w: jnp.einsum("ncd,nm->ncmd", w, eye).reshape(LRU_WIDTH, LRU_WIDTH)
    w_ab = jnp.concatenate([bd(w_lru_a), bd(w_lru_x)], -1).astype(BF16)
    wa2 = jnp.concatenate([w_gla_a2, jnp.zeros((LANES - GLA_RANK, GLA_HEADS * GLA_DK), F32)], 0).astype(BF16)
    wr = jnp.concatenate([w_router, jnp.zeros((D, LANES - N_EXPERTS), F32)], -1).astype(BF16)
    br = jnp.concatenate([b_router, jnp.full((LANES - N_EXPERTS,), -jnp.inf, F32)]).reshape(1, LANES)
    row = lambda v: v.reshape(1, -1).astype(F32)
    return dict(
        w16=w16, w32=w32,
        g_cq=row(g_cq), g_ckv=row(g_ckv), w_uq=uq.astype(BF16), w_uq_rot=uqr.astype(BF16), w_ukv=w_ukv.astype(BF16),
        conv_w=conv_w, conv_b=row(conv_b), w_lru_ab=w_ab, b_lru_ab=row(jnp.concatenate([b_lru_a, b_lru_x])),
        lru_ncsp=row(-LRU_C * jax.nn.softplus(-lru_lambda)),
        w_gla_a2=wa2, b_gla_a2=row(b_gla_a2),
        w_merge=w_merge.astype(BF16), b_merge=row(b_merge), w_branch=w_branch.astype(BF16), w_out=w_out.astype(BF16),
        ln1_g=row(ln1_g), ln1_b=row(ln1_b), w_router=wr, b_router=br,
        w_exp_gate=w_exp_gate, b_exp_gate=b_exp_gate.reshape(N_EXPERTS, 1, D_EXPERT),
        w_exp_up=w_exp_up, b_exp_up=b_exp_up.reshape(N_EXPERTS, 1, D_EXPERT),
        w_exp_down=w_exp_down, b_exp_down=b_exp_down.reshape(N_EXPERTS, 1, D_MODEL),
        ln2_g=row(ln2_g), ln2_b=row(ln2_b))


def _row_tile(n_rows, want):
    best = CHUNK
    for t in range(CHUNK, want + 1, CHUNK):
        if n_rows % t == 0:
            best = t
    return best


def kernel(x, meta_tokens, ln0_g, ln0_b, w_in, w_merge, b_merge, g_cq, g_ckv, w_uq, w_ukv, conv_w, conv_b, w_lru_a, b_lru_a, w_lru_x, b_lru_x, lru_lambda, w_gla_a2, b_gla_a2, w_branch, w_out, ln1_g, ln1_b, w_router, b_router, w_exp_gate, b_exp_gate, w_exp_up, b_exp_up, w_exp_down, b_exp_down, ln2_g, ln2_b):
    B, S, D = x.shape
    depth = w_in.shape[0]
    alpha = (2.0 * depth) ** 0.25
    tp = S + CHUNK
    nc = tp // CHUNK
    n_rows = B * tp
    tabs = _tables(tp)
    per_layer = (w_in, w_merge, b_merge, g_cq, g_ckv, w_uq, w_ukv, conv_w, conv_b, w_lru_a, b_lru_a, w_lru_x, b_lru_x,
                 lru_lambda, w_gla_a2, b_gla_a2, w_branch, w_out, ln1_g, ln1_b, w_router, b_router, w_exp_gate,
                 b_exp_gate, w_exp_up, b_exp_up, w_exp_down, b_exp_down, ln2_g, ln2_b)
    tm = _row_tile(tp, 640)
    t_att = tm
    h = _ln0(x, meta_tokens.astype(x.dtype), ln0_g, ln0_b).reshape(n_rows, D)
    for l in range(depth):
        wts = _layer_weights(*(p[l] for p in per_layer))
        p16 = _proj(h, wts["w16"], BF16, tm)
        p32 = _proj(h, wts["w32"], F32, tm)
        y_ret = _retention(p16, p32, tabs, B, nc)
        qh, kh, vh = _mla_prep(p16, tabs, wts, B, tp, t_att)
        y_mla = _flash(qh, kh, vh, t_att, t_att)
        y_lru = _lru(p32, wts, B, nc)
        y_gla = _gla(p16, p32, wts, B, nc)
        h1, comb = _merge(h, (y_ret, y_mla, y_lru, y_gla), wts, alpha, tm)
        h = _moe(h1, comb, wts, alpha, tm, tp)
    return h.reshape(B, tp, D)[:, CHUNK:]
```

```python
import functools

import jax
import jax.numpy as jnp
from jax import lax
from jax.experimental import pallas as pl
from jax.experimental.pallas import tpu as pltpu

F32 = jnp.float32
BF16 = jnp.bfloat16
U32 = jnp.uint32
I32 = jnp.int32

D_MODEL = 1024
N_META = 16
CHUNK = 128
PAD_FRONT = CHUNK - N_META
ROPE_BASE = 10000.0
NEG_BIG = -1e30

RET_HEADS, RET_DK, RET_DV = 4, 64, 128
MLA_HEADS, MLA_Q_RANK, MLA_KV_RANK, MLA_NOPE, MLA_ROPE, MLA_DV = 8, 256, 128, 64, 32, 64
LRU_WIDTH, LRU_BLOCKS, CONV_W, LRU_C = 512, 8, 4, 8.0
LRU_BLOCK = LRU_WIDTH // LRU_BLOCKS
GLA_HEADS, GLA_DK, GLA_DV, GLA_RANK, GLA_TAU, GLA_CHUNK = 4, 64, 128, 16, 16.0, 16
N_BRANCH, BRANCH_W = 4, 512
N_EXPERTS, TOP_K, D_EXPERT = 32, 4, 1024
SWIGLU_LIMIT, SWIGLU_ALPHA = 7.0, 1.702

LANES = 128
VMEM_LIMIT = 56 * 1024 * 1024
EXPERT_TILE = 512

_O_RQ, _O_RK, _O_RV, _O_RG = 0, 256, 512, 1024
_O_CQ, _O_CKV, _O_KR = 1536, 1792, 1920
_O_LX, _O_LG = 1952, 2464
_O_GQ, _O_GK, _O_GV, _O_GA, _O_GG = 2976, 3232, 3488, 4000, 4016


def _cparams(sem):
    return pltpu.CompilerParams(dimension_semantics=sem, vmem_limit_bytes=VMEM_LIMIT)


def _const_spec(shape):
    nd = len(shape)
    return pl.BlockSpec(shape, lambda *_: (0,) * nd, pipeline_mode=pl.Buffered(1))


def _ln_rows(x, g, b, eps=1e-5):
    mu = jnp.mean(x, -1, keepdims=True)
    xc = x - mu
    var = jnp.mean(xc * xc, -1, keepdims=True)
    return xc * lax.rsqrt(var + eps) * g + b


def _dot(a, b):
    return jnp.dot(a, b, preferred_element_type=F32)


def _dot_nt(a, b):
    return lax.dot_general(a, b, (((1,), (1,)), ((), ())), preferred_element_type=F32)


def _pack_bf16_pairs(x):
    n = x.shape[1] // 2
    bits = lax.bitcast_convert_type(x.astype(BF16).astype(F32), U32)
    return (bits[:, :n] & U32(0xFFFF0000)) | (bits[:, n:] >> 16)


def _unpack_bf16_pairs(w):
    hi = lax.bitcast_convert_type(w & U32(0xFFFF0000), F32)
    lo = lax.bitcast_convert_type(w << 16, F32)
    return jnp.concatenate([hi, lo], axis=1)


def _ln0_kernel(x_ref, meta_ref, g_ref, b_ref, o_ref):
    i = pl.program_id(1)
    g = g_ref[...]
    b = b_ref[...]

    @pl.when(i == 0)
    def _():
        o_ref[0, :PAD_FRONT, :] = jnp.zeros((PAD_FRONT, D_MODEL), F32)
        o_ref[0, PAD_FRONT:, :] = _ln_rows(meta_ref[...], g, b)

    @pl.when(i > 0)
    def _():
        o_ref[0] = _ln_rows(x_ref[0], g, b)


def _ln0(x, meta, g, b):
    B, S, D = x.shape
    nc = S // CHUNK + 1
    return pl.pallas_call(
        _ln0_kernel,
        grid=(B, nc),
        in_specs=[pl.BlockSpec((1, CHUNK, D), lambda bb, i: (bb, jnp.maximum(i - 1, 0), 0)),
                  pl.BlockSpec((N_META, D), lambda bb, i: (0, 0)),
                  pl.BlockSpec((1, D), lambda bb, i: (0, 0)),
                  pl.BlockSpec((1, D), lambda bb, i: (0, 0))],
        out_specs=pl.BlockSpec((1, CHUNK, D), lambda bb, i: (bb, i, 0)),
        out_shape=jax.ShapeDtypeStruct((B, nc * CHUNK, D), F32),
        name="ln0",
        compiler_params=_cparams(("parallel", "arbitrary")),
    )(x, meta, g.reshape(1, D), b.reshape(1, D))


def _proj_kernel(h_ref, w_ref, o_ref, *, col_step):
    h = h_ref[...].astype(BF16)
    n = w_ref.shape[1]
    for c0 in range(0, n, col_step):
        c1 = min(c0 + col_step, n)
        o_ref[:, c0:c1] = _dot(h, w_ref[:, c0:c1]).astype(o_ref.dtype)


def _proj(h, w, out_dtype, tm):
    n_rows, d = h.shape
    n = w.shape[1]
    return pl.pallas_call(
        functools.partial(_proj_kernel, col_step=512),
        grid=(n_rows // tm,),
        in_specs=[pl.BlockSpec((tm, d), lambda i: (i, 0)), _const_spec((d, n))],
        out_specs=pl.BlockSpec((tm, n), lambda i: (i, 0)),
        out_shape=jax.ShapeDtypeStruct((n_rows, n), out_dtype),
        name="in_proj",
        compiler_params=_cparams(("parallel",)),
    )(h, w)


def _half_masks(dtype):
    lane = lax.broadcasted_iota(I32, (1, LANES), 1)
    lo = (lane < LANES // 2).astype(dtype)
    return lo, (1 - lo).astype(dtype)


def _ret_kernel(cdec_ref, qk_ref, rot_ref, v_ref, g_ref, cos_ref, sin_ref, dec_ref, dmat_ref, o_ref, s_ref):
    c = pl.program_id(1)

    @pl.when(c == 0)
    def _():
        s_ref[...] = jnp.zeros_like(s_ref)

    r = qk_ref[...].astype(F32) * cos_ref[...] + rot_ref[...].astype(F32) * sin_ref[...]
    rd = r * dec_ref[...]
    masks = _half_masks(F32)
    for hd in range(RET_HEADS):
        pair, half = divmod(hd, 2)
        m = masks[half]
        q_lo, k_lo = pair * LANES, 2 * LANES + pair * LANES
        qp = (r[:, q_lo:q_lo + LANES] * m).astype(BF16)
        kp = r[:, k_lo:k_lo + LANES].astype(BF16)
        vh = v_ref[:, hd * RET_DV:(hd + 1) * RET_DV]
        scores = _dot_nt(qp, kp) * dmat_ref[hd]
        o = _dot(scores.astype(BF16), vh)
        qin = (rd[:, q_lo:q_lo + LANES] * m).astype(BF16)
        s_old = s_ref[hd]
        o = o + _dot(qin, s_old.astype(BF16))
        kdec_t = (rd[:, k_lo:k_lo + LANES] * m).T.astype(BF16)
        s_ref[hd] = s_old * cdec_ref[hd] + _dot(kdec_t, vh)
        mu = jnp.mean(o, -1, keepdims=True)
        oc = o - mu
        var = jnp.mean(oc * oc, -1, keepdims=True)
        on = oc * lax.rsqrt(var + 1e-5)
        gate = g_ref[:, hd * RET_DV:(hd + 1) * RET_DV]
        o_ref[:, hd * RET_DV:(hd + 1) * RET_DV] = (gate * jax.nn.sigmoid(gate) * on).astype(o_ref.dtype)


def _retention(p16, p32, tabs, B, nc):
    n_rows = p16.shape[0]
    row = lambda b, c: b * nc + c
    return pl.pallas_call(
        _ret_kernel,
        grid=(B, nc),
        in_specs=[pl.BlockSpec(memory_space=pltpu.SMEM),
                  pl.BlockSpec((CHUNK, 512), lambda b, c: (row(b, c), 0)),
                  pl.BlockSpec((CHUNK, 512), lambda b, c: (row(b, c), 1)),
                  pl.BlockSpec((CHUNK, 512), lambda b, c: (row(b, c), 2)),
                  pl.BlockSpec((CHUNK, 512), lambda b, c: (row(b, c), 0)),
                  pl.BlockSpec((CHUNK, 512), lambda b, c: (c, 0)),
                  pl.BlockSpec((CHUNK, 512), lambda b, c: (c, 0)),
                  _const_spec((CHUNK, 512)),
                  _const_spec((RET_HEADS, CHUNK, CHUNK))],
        out_specs=pl.BlockSpec((CHUNK, 512), lambda b, c: (row(b, c), 0)),
        out_shape=jax.ShapeDtypeStruct((n_rows, 512), BF16),
        scratch_shapes=[pltpu.VMEM((RET_HEADS, LANES, RET_DV), F32)],
        name="retention",
        compiler_params=_cparams(("parallel", "arbitrary")),
    )(tabs["ret_cdec"], p16, p16, p16, p32, tabs["ret_cos"], tabs["ret_sin"], tabs["ret_dec"], tabs["ret_dmat"])


MLA_BIAS_LANE = MLA_NOPE + MLA_ROPE
M_INIT = -3.0e38


def _mla_prep_kernel(lat_ref, krot_ref, cos_ref, sin_ref, gq_ref, gkv_ref, wq_ref, wqr_ref, wkv_ref,
                     qt_ref, k_ref, vt_ref):
    i = pl.program_id(1)
    tr = lat_ref.shape[0]
    lat = lat_ref[...].astype(F32)
    cq = lat[:, :MLA_Q_RANK]
    ckv = lat[:, MLA_Q_RANK:MLA_Q_RANK + MLA_KV_RANK]
    kr = lat[:, MLA_Q_RANK + MLA_KV_RANK:]
    cos = cos_ref[...]
    sin = sin_ref[...]
    cqn = (cq * lax.rsqrt(jnp.mean(cq * cq, -1, keepdims=True) + 1e-6) * gq_ref[...]).astype(BF16)
    ckvn = (ckv * lax.rsqrt(jnp.mean(ckv * ckv, -1, keepdims=True) + 1e-6) * gkv_ref[...]).astype(BF16)
    krr = kr * cos + krot_ref[...].astype(F32) * sin
    scale = (MLA_NOPE + MLA_ROPE) ** -0.5
    lane = lax.broadcasted_iota(I32, (tr, LANES), 1)
    row = lax.broadcasted_iota(I32, (tr, LANES), 0)
    is_nope = lane < MLA_NOPE
    is_bias = lane == MLA_BIAS_LANE
    kbias = jnp.where(is_bias & (i == 0) & (row < PAD_FRONT), NEG_BIG, 0.0)
    for hd in range(MLA_HEADS):
        sl = slice(hd * LANES, (hd + 1) * LANES)
        q = (_dot(cqn, wq_ref[:, sl]) * cos + _dot(cqn, wqr_ref[:, sl]) * sin) * scale
        qt_ref[0, hd] = jnp.where(is_bias, 1.0, q).T.astype(qt_ref.dtype)
        kv = _dot(ckvn, wkv_ref[:, sl])
        k_ref[0, hd] = (jnp.where(is_nope, kv, krr) + kbias).astype(k_ref.dtype)
        vt_ref[0, hd] = kv.T.astype(vt_ref.dtype)


def _mla_prep(p16, tabs, wts, B, tp, tr):
    nt = tp // tr
    assert tr >= PAD_FRONT
    row = lambda b, i: b * nt + i
    hm = jax.ShapeDtypeStruct((B, MLA_HEADS, tp, LANES), BF16)
    hm_t = jax.ShapeDtypeStruct((B, MLA_HEADS, LANES, tp), BF16)
    hspec = pl.BlockSpec((1, MLA_HEADS, tr, LANES), lambda b, i: (b, 0, i, 0))
    hspec_t = pl.BlockSpec((1, MLA_HEADS, LANES, tr), lambda b, i: (b, 0, 0, i))
    return pl.pallas_call(
        _mla_prep_kernel,
        grid=(B, nt),
        in_specs=[pl.BlockSpec((tr, 512), lambda b, i: (row(b, i), 3)),
                  pl.BlockSpec((tr, LANES), lambda b, i: (row(b, i), 3072 // LANES)),
                  pl.BlockSpec((tr, LANES), lambda b, i: (i, 0)),
                  pl.BlockSpec((tr, LANES), lambda b, i: (i, 0)),
                  _const_spec((1, MLA_Q_RANK)), _const_spec((1, MLA_KV_RANK)),
                  _const_spec((MLA_Q_RANK, MLA_HEADS * LANES)), _const_spec((MLA_Q_RANK, MLA_HEADS * LANES)),
                  _const_spec((MLA_KV_RANK, MLA_HEADS * LANES))],
        out_specs=[hspec_t, hspec, hspec_t],
        out_shape=[hm_t, hm, hm_t],
        name="mla_prep",
        compiler_params=_cparams(("parallel", "arbitrary")),
    )(p16, p16, tabs["mla_cos"], tabs["mla_sin"], wts["g_cq"], wts["g_ckv"], wts["w_uq"], wts["w_uq_rot"],
      wts["w_ukv"])


def _flash_kernel(qt_ref, k_ref, vt_ref, o_ref, *, tb):
    tp = k_ref.shape[2]
    nb = tp // tb
    ng = tb // LANES
    key_i = lax.broadcasted_iota(I32, (tb, LANES), 0)
    qry_i = lax.broadcasted_iota(I32, (tb, LANES), 1)
    lane = lax.broadcasted_iota(I32, (LANES, LANES), 1)
    for hh in range(2):
        def step(qts, j, carry, diag):
            k0 = pl.multiple_of(j * tb, tb)
            k = k_ref[0, hh, pl.ds(k0, tb), :]
            vt = vt_ref[0, hh, :, pl.ds(k0, tb)]
            new = []
            for g in range(ng):
                m, l, acc = carry[g]
                s = _dot(k, qts[g])
                if diag:
                    s = jnp.where(key_i <= qry_i + g * LANES, s, NEG_BIG)
                m_new = jnp.maximum(m, jnp.max(s, 0, keepdims=True))
                alpha = jnp.exp(m - m_new)
                p = jnp.exp(s - m_new)
                l_new = alpha * l + jnp.sum(p, 0, keepdims=True)
                acc_new = alpha * acc + _dot(vt, p.astype(BF16))
                new.append((m_new, l_new, acc_new))
            return tuple(new)

        def q_block(i, _):
            q0 = i * tb
            qts = [qt_ref[0, hh, :, pl.ds(pl.multiple_of(q0 + g * LANES, LANES), LANES)] for g in range(ng)]
            init = tuple((jnp.full((1, LANES), M_INIT, F32), jnp.zeros((1, LANES), F32),
                          jnp.zeros((LANES, LANES), F32)) for _ in range(ng))
            carry = lax.fori_loop(0, i, lambda j, c: step(qts, j, c, False), init)
            carry = step(qts, i, carry, True)
            for g in range(ng):
                m, l, acc = carry[g]
                o = (acc / l).T
                rows = pl.ds(pl.multiple_of(q0 + g * LANES, LANES), LANES)
                if hh == 0:
                    o_ref[rows, :] = pltpu.roll(o, LANES // 2, axis=1)
                else:
                    o_ref[rows, :] = jnp.where(lane < LANES // 2, o_ref[rows, :], o)
            return 0

        lax.fori_loop(0, nb, q_block, 0)


def _flash(qt, kh, vt, tb):
    B, H, tp, _ = kh.shape
    npair = H // 2
    spec = pl.BlockSpec((1, 2, tp, LANES), lambda b, j: (b, j, 0, 0))
    spec_t = pl.BlockSpec((1, 2, LANES, tp), lambda b, j: (b, j, 0, 0))
    return pl.pallas_call(
        functools.partial(_flash_kernel, tb=tb),
        grid=(B, npair),
        in_specs=[spec_t, spec, spec_t],
        out_specs=pl.BlockSpec((tp, LANES), lambda b, j: (b, j)),
        out_shape=jax.ShapeDtypeStruct((B * tp, npair * LANES), F32),
        name="mla_flash",
        compiler_params=_cparams(("parallel", "arbitrary")),
    )(qt, kh, vt)


def _lru_kernel(x_ref, g_ref, cw_ref, cb_ref, wab_ref, bab_ref, ncsp_ref, o_ref, hist_ref, h_ref):
    c = pl.program_id(1)
    tr = x_ref.shape[0]

    @pl.when(c == 0)
    def _():
        hist_ref[...] = jnp.zeros_like(hist_ref)
        h_ref[...] = jnp.zeros_like(h_ref)

    x = x_ref[...]
    xcat = jnp.concatenate([hist_ref[...], x], axis=0)
    hist_ref[...] = x[tr - 8:, :]
    xc = cb_ref[...] + cw_ref[CONV_W - 1:CONV_W, :] * x
    for j in range(CONV_W - 1):
        off = 8 - (CONV_W - 1) + j
        xc = xc + cw_ref[j:j + 1, :] * xcat[off:off + tr, :]
    z = _dot(xc.astype(BF16), wab_ref[...]) + bab_ref[...]
    r = jax.nn.sigmoid(z[:, :LRU_WIDTH])
    i = jax.nn.sigmoid(z[:, LRU_WIDTH:])
    log_a = ncsp_ref[...] * r
    a = jnp.exp(log_a)
    u = jnp.sqrt(1.0 - jnp.exp(2.0 * log_a)) * (i * xc)
    row = lax.broadcasted_iota(I32, (tr, LRU_WIDTH), 0)
    u = jnp.where((c == 0) & (row < PAD_FRONT), 0.0, u)
    s = 1
    while s < tr:
        keep = row >= s
        a_sh = jnp.where(keep, pltpu.roll(a, s, axis=0), 1.0)
        u_sh = jnp.where(keep, pltpu.roll(u, s, axis=0), 0.0)
        u = a * u_sh + u
        a = a * a_sh
        s *= 2
    hs = u + a * h_ref[...]
    h_ref[...] = hs[tr - 1:tr, :]
    o_ref[...] = (hs * jax.nn.gelu(g_ref[...])).astype(o_ref.dtype)


def _lru(p32, wts, B, nc):
    n_rows = p32.shape[0]
    row = lambda b, c: b * nc + c
    return pl.pallas_call(
        _lru_kernel,
        grid=(B, nc),
        in_specs=[pl.BlockSpec((CHUNK, 512), lambda b, c: (row(b, c), 1)),
                  pl.BlockSpec((CHUNK, 512), lambda b, c: (row(b, c), 2)),
                  _const_spec((CONV_W, LRU_WIDTH)), _const_spec((1, LRU_WIDTH)),
                  _const_spec((LRU_WIDTH, 2 * LRU_WIDTH)), _const_spec((1, 2 * LRU_WIDTH)),
                  _const_spec((1, LRU_WIDTH))],
        out_specs=pl.BlockSpec((CHUNK, 512), lambda b, c: (row(b, c), 0)),
        out_shape=jax.ShapeDtypeStruct((n_rows, 512), BF16),
        scratch_shapes=[pltpu.VMEM((8, LRU_WIDTH), F32), pltpu.VMEM((1, LRU_WIDTH), F32)],
        name="rglru",
        compiler_params=_cparams(("parallel", "arbitrary")),
    )(p32, p32, wts["conv_w"], wts["conv_b"], wts["w_lru_ab"], wts["b_lru_ab"], wts["lru_ncsp"])


def _gla_kernel(qk_ref, v_ref, a_ref, g_ref, wa2_ref, ba2_ref, o_ref, st_ref):
    c = pl.program_id(1)
    tr = qk_ref.shape[0]
    ncs = tr // GLA_CHUNK
    hk = GLA_HEADS * GLA_DK

    @pl.when(c == 0)
    def _():
        st_ref[...] = jnp.zeros_like(st_ref)

    x = _dot(a_ref[...].astype(BF16), wa2_ref[...]) + ba2_ref[...]
    la = (jnp.minimum(x, 0.0) - jnp.log(1.0 + jnp.exp(-jnp.abs(x)))) * (1.0 / GLA_TAU)
    row = lax.broadcasted_iota(I32, (tr, hk), 0)
    rin = row & (GLA_CHUNK - 1)
    b = la
    sfx = la
    s = 1
    while s < GLA_CHUNK:
        b = b + jnp.where(rin >= s, pltpu.roll(b, s, axis=0), 0.0)
        sfx = sfx + jnp.where(rin < GLA_CHUNK - s, pltpu.roll(sfx, tr - s, axis=0), 0.0)
        s *= 2
    blast = b + sfx - la
    qk = qk_ref[...].astype(F32)
    q = qk[:, :hk]
    k = qk[:, hk:]
    qe = q * jnp.exp(b)
    ke = (k * jnp.exp(-b)).astype(BF16)
    kl = (k * jnp.exp(blast - b)).astype(BF16)
    cdec = jnp.exp(blast)
    masks = _half_masks(F32)
    ri = lax.broadcasted_iota(I32, (tr, tr), 0)
    ci = lax.broadcasted_iota(I32, (tr, tr), 1)
    sh = GLA_CHUNK.bit_length() - 1
    cchunk = ci >> sh
    intra = ((ri >> sh) == cchunk) & (ci <= ri)
    for hd in range(GLA_HEADS):
        pair, half = divmod(hd, 2)
        lo = pair * LANES
        qm = (qe[:, lo:lo + LANES] * masks[half]).astype(BF16)
        amat = jnp.where(intra, _dot_nt(qm, ke[:, lo:lo + LANES]), 0.0)
        vh = v_ref[:, hd * GLA_DV:(hd + 1) * GLA_DV]
        o = _dot(amat.astype(BF16), vh)
        vt = vh.astype(F32).T
        st = st_ref[hd]
        cross = []
        for n in range(ncs):
            r0 = n * GLA_CHUNK
            cross.append(_dot_nt(qm[r0:r0 + GLA_CHUNK, :], st.astype(BF16)))
            vt_n = jnp.where(cchunk == n, vt, 0.0).astype(BF16)
            st = st * cdec[r0:r0 + 1, lo:lo + LANES] + _dot(vt_n, kl[:, lo:lo + LANES])
        st_ref[hd] = st
        o = o + jnp.concatenate(cross, axis=0)
        on = o * lax.rsqrt(jnp.mean(o * o, -1, keepdims=True) + 1e-6)
        gate = g_ref[:, hd * GLA_DV:(hd + 1) * GLA_DV]
        o_ref[:, hd * GLA_DV:(hd + 1) * GLA_DV] = (gate * jax.nn.sigmoid(gate) * on).astype(o_ref.dtype)


def _gla(p16, p32, wts, B, nc):
    n_rows = p16.shape[0]
    row = lambda b, c: b * nc + c
    return pl.pallas_call(
        _gla_kernel,
        grid=(B, nc),
        in_specs=[pl.BlockSpec((CHUNK, 512), lambda b, c: (row(b, c), 4)),
                  pl.BlockSpec((CHUNK, 512), lambda b, c: (row(b, c), 5)),
                  pl.BlockSpec((CHUNK, LANES), lambda b, c: (row(b, c), 2048 // LANES)),
                  pl.BlockSpec((CHUNK, 512), lambda b, c: (row(b, c), 3)),
                  _const_spec((LANES, GLA_HEADS * GLA_DK)), _const_spec((1, GLA_HEADS * GLA_DK))],
        out_specs=pl.BlockSpec((CHUNK, 512), lambda b, c: (row(b, c), 0)),
        out_shape=jax.ShapeDtypeStruct((n_rows, 512), BF16),
        scratch_shapes=[pltpu.VMEM((GLA_HEADS, GLA_DV, LANES), F32)],
        name="gla",
        compiler_params=_cparams(("parallel", "arbitrary")),
    )(p16, p16, p32, p32, wts["w_gla_a2"], wts["b_gla_a2"])


def _merge_kernel(h_ref, y0_ref, y1_ref, y2_ref, y3_ref, wm_ref, bm_ref, wb_ref, wo_ref, g_ref, b_ref, wr_ref, br_ref,
                  h1_ref, h1p_ref, ri_ref, rw_ref, cnt_out_ref, cnt_ref, *, alpha):
    @pl.when(pl.program_id(0) == 0)
    def _():
        cnt_ref[...] = jnp.zeros_like(cnt_ref)

    h = h_ref[...]
    tm = h.shape[0]
    h16 = h.astype(BF16)
    ys = (y0_ref[...], y1_ref[...].astype(BF16), y2_ref[...], y3_ref[...])
    mixed = None
    for n in range(N_BRANCH):
        sl = slice(n * D_MODEL, (n + 1) * D_MODEL)
        gate = jax.nn.sigmoid(_dot(h16, wm_ref[:, sl]) + bm_ref[:, sl])
        term = gate * _dot(ys[n], wb_ref[n])
        mixed = term if mixed is None else mixed + term
    mix = _dot(mixed.astype(BF16), wo_ref[...])
    h1 = _ln_rows(alpha * h + mix, g_ref[...], b_ref[...])
    h1_ref[...] = h1
    h1p_ref[...] = _pack_bf16_pairs(h1)
    logits = _dot(h1.astype(BF16), wr_ref[...]) + br_ref[...]
    lane = lax.broadcasted_iota(I32, logits.shape, 1)
    vals, sels, idxs = [], [], []
    cur = logits
    for _ in range(TOP_K):
        mx = jnp.max(cur, -1, keepdims=True)
        idx = jnp.min(jnp.where(cur == mx, lane, LANES), -1, keepdims=True)
        sel = lane == idx
        vals.append(mx)
        sels.append(sel)
        idxs.append(idx)
        cur = jnp.where(sel, -jnp.inf, cur)
    es = [jnp.exp(v - vals[0]) for v in vals]
    den = es[0] + es[1] + es[2] + es[3]
    onehot = jnp.zeros_like(logits)
    for sel in sels:
        onehot = onehot + jnp.where(sel, 1.0, 0.0)
    rr = lax.broadcasted_iota(I32, (tm, tm), 0)
    cc = lax.broadcasted_iota(I32, (tm, tm), 1)
    lower = jnp.where(cc < rr, 1.0, 0.0).astype(BF16)
    base = cnt_ref[...] + _dot(lower, onehot.astype(BF16))
    route_i = jnp.zeros(logits.shape, I32)
    route_w = jnp.zeros_like(logits)
    for k in range(TOP_K):
        rank = jnp.sum(jnp.where(sels[k], base, 0.0), -1, keepdims=True).astype(I32)
        route_i = jnp.where(lane == k, idxs[k], route_i)
        route_i = jnp.where(lane == TOP_K + k, rank, route_i)
        route_w = jnp.where(lane == k, es[k] / den, route_w)
    ri_ref[...] = route_i
    rw_ref[...] = route_w
    cnt_ref[...] = cnt_ref[...] + jnp.sum(onehot, 0, keepdims=True)
    cnt_out_ref[...] = jnp.broadcast_to(cnt_ref[...], cnt_out_ref.shape)


def _merge(h, ys, wts, alpha, tm):
    n_rows = h.shape[0]
    rspec = lambda w: pl.BlockSpec((tm, w), lambda i: (i, 0))
    return pl.pallas_call(
        functools.partial(_merge_kernel, alpha=alpha),
        grid=(n_rows // tm,),
        in_specs=[rspec(D_MODEL), rspec(512), rspec(512), rspec(512), rspec(512),
                  _const_spec((D_MODEL, N_BRANCH * D_MODEL)), _const_spec((1, N_BRANCH * D_MODEL)),
                  _const_spec((N_BRANCH, BRANCH_W, D_MODEL)), _const_spec((D_MODEL, D_MODEL)),
                  _const_spec((1, D_MODEL)), _const_spec((1, D_MODEL)),
                  _const_spec((D_MODEL, LANES)), _const_spec((1, LANES))],
        out_specs=[rspec(D_MODEL), rspec(D_MODEL // 2), rspec(LANES), rspec(LANES),
                   pl.BlockSpec((8, LANES), lambda i: (0, 0))],
        out_shape=[jax.ShapeDtypeStruct((n_rows, D_MODEL), F32), jax.ShapeDtypeStruct((n_rows, D_MODEL // 2), U32),
                   jax.ShapeDtypeStruct((n_rows, LANES), I32), jax.ShapeDtypeStruct((n_rows, LANES), F32),
                   jax.ShapeDtypeStruct((8, LANES), F32)],
        scratch_shapes=[pltpu.VMEM((1, LANES), F32)],
        name="merge_ln1_router",
        compiler_params=_cparams(("arbitrary",)),
    )(h, *ys, wts["w_merge"], wts["b_merge"], wts["w_branch"], wts["w_out"], wts["ln1_g"], wts["ln1_b"],
      wts["w_router"], wts["b_router"])


def _route_plan(route_i, cnt, n_tiles):
    ids = route_i[:, :TOP_K]
    rank = route_i[:, TOP_K:2 * TOP_K]
    counts = cnt[0, :N_EXPERTS].astype(I32)
    padded = (counts + EXPERT_TILE - 1) // EXPERT_TILE * EXPERT_TILE
    gend = jnp.cumsum(padded)
    slot = jnp.take(gend - padded, ids) + rank
    n_used = gend[-1] // EXPERT_TILE
    tstart = jnp.arange(n_tiles, dtype=I32) * EXPERT_TILE
    te = jnp.minimum(jnp.searchsorted(gend, tstart, side="right"), N_EXPERTS - 1).astype(I32)
    te = jnp.where(tstart < gend[-1], te, te[jnp.maximum(n_used - 1, 0)])
    return slot.astype(I32), te, n_used.reshape(1).astype(I32)


def _dispatch_kernel(slot_ref, x_ref, xs_in_ref, xs_ref, sem):
    del xs_in_ref
    tm = x_ref.shape[0]

    def issue(r, _):
        for k in range(TOP_K):
            pltpu.make_async_copy(x_ref.at[pl.ds(r, 1)], xs_ref.at[pl.ds(slot_ref[0, 0, TOP_K * r + k], 1)],
                                  sem).start()
        return 0

    lax.fori_loop(0, tm, issue, 0, unroll=8)
    for _ in range(TOP_K):
        pltpu.make_async_copy(x_ref, xs_ref.at[pl.ds(0, tm)], sem).wait()


def _dispatch(slot3, h1p, n_slots, tm):
    n_rows, w = h1p.shape
    return pl.pallas_call(
        _dispatch_kernel,
        grid=(n_rows // tm,),
        in_specs=[pl.BlockSpec((1, 1, TOP_K * tm), lambda i: (i, 0, 0), memory_space=pltpu.SMEM),
                  pl.BlockSpec((tm, w), lambda i: (i, 0)),
                  pl.BlockSpec(memory_space=pl.ANY)],
        out_specs=pl.BlockSpec(memory_space=pl.ANY),
        out_shape=jax.ShapeDtypeStruct((n_slots, w), U32),
        scratch_shapes=[pltpu.SemaphoreType.DMA(())],
        input_output_aliases={2: 0},
        name="moe_dispatch",
        compiler_params=_cparams(("arbitrary",)),
    )(slot3, h1p, jnp.zeros((n_slots, w), U32))


def _expert_kernel(te_ref, nu_ref, xs_ref, wg_ref, bg_ref, wu_ref, bu_ref, wd_ref, bd_ref, ys_ref, w16_ref):
    t = pl.program_id(0)

    @pl.when(t < nu_ref[0])
    def _():
        @pl.when((t == 0) | (te_ref[t] != te_ref[jnp.maximum(t - 1, 0)]))
        def _():
            w16_ref[0] = wg_ref[0].astype(BF16)
            w16_ref[1] = wu_ref[0].astype(BF16)
            w16_ref[2] = wd_ref[0].astype(BF16)

        x16 = _unpack_bf16_pairs(xs_ref[...]).astype(BF16)
        gate = jnp.minimum(_dot(x16, w16_ref[0]) + bg_ref[0], SWIGLU_LIMIT)
        up = jnp.clip(_dot(x16, w16_ref[1]) + bu_ref[0], -SWIGLU_LIMIT, SWIGLU_LIMIT)
        act = (up + 1.0) * gate * jax.nn.sigmoid(SWIGLU_ALPHA * gate)
        ys_ref[...] = _pack_bf16_pairs(_dot(act.astype(BF16), w16_ref[2]) + bd_ref[0])

    @pl.when(t >= nu_ref[0])
    def _():
        ys_ref[...] = jnp.zeros_like(ys_ref)


def _experts(te, n_used, xs, wts):
    n_slots, w = xs.shape
    n_tiles = n_slots // EXPERT_TILE
    tile = lambda t, te_ref, nu_ref: (jnp.minimum(t, nu_ref[0] - 1), 0)
    wsel = lambda t, te_ref, nu_ref: (te_ref[t], 0, 0)
    grid_spec = pltpu.PrefetchScalarGridSpec(
        num_scalar_prefetch=2,
        grid=(n_tiles,),
        in_specs=[pl.BlockSpec((EXPERT_TILE, w), tile),
                  pl.BlockSpec((1, D_MODEL, D_EXPERT), wsel), pl.BlockSpec((1, 1, D_EXPERT), wsel),
                  pl.BlockSpec((1, D_MODEL, D_EXPERT), wsel), pl.BlockSpec((1, 1, D_EXPERT), wsel),
                  pl.BlockSpec((1, D_EXPERT, D_MODEL), wsel), pl.BlockSpec((1, 1, D_MODEL), wsel)],
        out_specs=pl.BlockSpec((EXPERT_TILE, w), lambda t, te_ref, nu_ref: (t, 0)),
        scratch_shapes=[pltpu.VMEM((3, D_MODEL, D_EXPERT), BF16)])
    return pl.pallas_call(
        _expert_kernel,
        grid_spec=grid_spec,
        out_shape=jax.ShapeDtypeStruct((n_slots, w), U32),
        name="moe_experts",
        compiler_params=_cparams(("arbitrary",)),
    )(te, n_used, xs, *wts)


def _combine_kernel(slot_ref, x_ref, rw_ref, ys_ref, g_ref, b_ref, o_ref, buf_ref, sem, *, alpha, tiles_per_seq):
    tm = x_ref.shape[0]

    def issue(r, _):
        for k in range(TOP_K):
            pltpu.make_async_copy(ys_ref.at[pl.ds(slot_ref[0, 0, TOP_K * r + k], 1)], buf_ref.at[k, pl.ds(r, 1)],
                                  sem).start()
        return 0

    lax.fori_loop(0, tm, issue, 0, unroll=8)
    for k in range(TOP_K):
        pltpu.make_async_copy(ys_ref.at[pl.ds(0, tm)], buf_ref.at[k], sem).wait()
    rw = rw_ref[...]
    acc = None
    for k in range(TOP_K):
        term = rw[:, k:k + 1] * _unpack_bf16_pairs(buf_ref[k])
        acc = term if acc is None else acc + term
    out = _ln_rows(alpha * x_ref[...] + acc, g_ref[...], b_ref[...])
    row = lax.broadcasted_iota(I32, out.shape, 0)
    first_tile = lax.rem(pl.program_id(0), tiles_per_seq) == 0
    o_ref[...] = jnp.where(first_tile & (row < PAD_FRONT), 0.0, out)


def _combine(slot3, h1, route_w, ys, wts, alpha, tm, tp):
    n_rows = h1.shape[0]
    w = ys.shape[1]
    assert tp % tm == 0 and tm >= PAD_FRONT
    return pl.pallas_call(
        functools.partial(_combine_kernel, alpha=alpha, tiles_per_seq=tp // tm),
        grid=(n_rows // tm,),
        in_specs=[pl.BlockSpec((1, 1, TOP_K * tm), lambda i: (i, 0, 0), memory_space=pltpu.SMEM),
                  pl.BlockSpec((tm, D_MODEL), lambda i: (i, 0)),
                  pl.BlockSpec((tm, LANES), lambda i: (i, 0)),
                  pl.BlockSpec(memory_space=pl.ANY),
                  _const_spec((1, D_MODEL)), _const_spec((1, D_MODEL))],
        out_specs=pl.BlockSpec((tm, D_MODEL), lambda i: (i, 0)),
        out_shape=jax.ShapeDtypeStruct((n_rows, D_MODEL), F32),
        scratch_shapes=[pltpu.VMEM((TOP_K, tm, w), U32), pltpu.SemaphoreType.DMA(())],
        name="moe_combine_ln2",
        compiler_params=_cparams(("arbitrary",)),
    )(slot3, h1, route_w, ys, wts["ln2_g"], wts["ln2_b"])


def _moe(h1, h1p, route_i, route_w, cnt, wts, expert_wts, layer, alpha, tm, tp):
    n_rows = h1.shape[0]
    n_tiles = -(-TOP_K * n_rows // EXPERT_TILE) + N_EXPERTS
    slot, te, n_used = _route_plan(route_i, cnt, n_tiles)
    slot3 = slot.reshape(n_rows // tm, 1, TOP_K * tm)
    xs = _dispatch(slot3, h1p, n_tiles * EXPERT_TILE, tm)
    ys = _experts(te + layer * N_EXPERTS, n_used, xs, expert_wts)
    return _combine(slot3, h1, route_w, ys, wts, alpha, tm, tp)


def _rot_half_cols(w, d):
    n = w.shape[1] // d
    w3 = w.reshape(w.shape[0], n, d)
    return jnp.concatenate([-w3[..., d // 2:], w3[..., :d // 2]], -1).reshape(w.shape)


def _tables(tp):
    pos = jnp.maximum(jnp.arange(tp, dtype=I32) - PAD_FRONT, 0).astype(F32)

    def cs(d):
        inv = ROPE_BASE ** (-jnp.arange(0, d, 2, dtype=F32) / d)
        ang = pos[:, None] * inv[None, :]
        return jnp.cos(ang), jnp.sin(ang)

    rc, rs = cs(RET_DK)
    ret_cos = jnp.tile(jnp.concatenate([rc, rc], -1), (1, 2 * RET_HEADS))
    ret_sin = jnp.tile(jnp.concatenate([rs, rs], -1), (1, 2 * RET_HEADS))
    mc, ms = cs(MLA_ROPE)
    ones = jnp.ones((tp, MLA_NOPE), F32)
    tail1 = jnp.ones((tp, LANES - MLA_NOPE - MLA_ROPE), F32)
    mla_cos = jnp.concatenate([ones, mc, mc, tail1], -1)
    mla_sin = jnp.concatenate([0 * ones, ms, ms, 0 * tail1], -1)
    log_gamma = jnp.log1p(-jnp.exp2(-5.0 - jnp.arange(RET_HEADS, dtype=F32)))
    idx = jnp.arange(CHUNK, dtype=F32)
    rel = idx[:, None] - idx[None, :]
    dmat = jnp.where(rel >= 0, jnp.exp(log_gamma[:, None, None] * jnp.maximum(rel, 0.0)), 0.0)
    k_dec = jnp.exp(log_gamma[:, None] * (CHUNK - 1.0 - idx))
    q_dec = jnp.exp(log_gamma[:, None] * (idx + 1.0))
    rep = lambda t: jnp.repeat(t.T, RET_DK, axis=1)
    ret_dec = jnp.concatenate([rep(q_dec), rep(k_dec)], -1)
    ret_cdec = jnp.exp(log_gamma * CHUNK)
    return dict(ret_cos=ret_cos, ret_sin=ret_sin, mla_cos=mla_cos, mla_sin=mla_sin, ret_dmat=dmat, ret_dec=ret_dec,
                ret_cdec=ret_cdec)


def _layer_weights(w_in, w_merge, b_merge, g_cq, g_ckv, w_uq, w_ukv, conv_w, conv_b, w_lru_a, b_lru_a, w_lru_x,
                   b_lru_x, lru_lambda, w_gla_a2, b_gla_a2, w_branch, w_out, ln1_g, ln1_b, w_router, b_router,
                   ln2_g, ln2_b):
    D = D_MODEL
    z = lambda n: jnp.zeros((D, n), F32)
    rqk = jnp.concatenate([w_in[:, _O_RQ:_O_RK], w_in[:, _O_RK:_O_RV] * (RET_DK ** -0.5)], -1)
    kr = w_in[:, _O_KR:_O_LX]
    kr_slot = jnp.concatenate([z(MLA_NOPE), kr, z(LANES - MLA_NOPE - MLA_ROPE)], -1)
    kr_rot_slot = jnp.concatenate([z(MLA_NOPE), _rot_half_cols(kr, MLA_ROPE), z(LANES - MLA_NOPE - MLA_ROPE)], -1)
    gqk = jnp.concatenate([w_in[:, _O_GQ:_O_GK], w_in[:, _O_GK:_O_GV] * (GLA_DK ** -0.5)], -1)
    w16 = jnp.concatenate([rqk, _rot_half_cols(rqk, RET_DK), w_in[:, _O_RV:_O_RG],
                           w_in[:, _O_CQ:_O_KR], kr_slot, gqk, w_in[:, _O_GV:_O_GA], kr_rot_slot], -1).astype(BF16)
    w32 = jnp.concatenate([w_in[:, _O_RG:_O_CQ], w_in[:, _O_LX:_O_LG], w_in[:, _O_LG:_O_GQ], w_in[:, _O_GG:],
                           w_in[:, _O_GA:_O_GG], z(LANES - GLA_RANK)], -1).astype(BF16)
    hq = MLA_NOPE + MLA_ROPE
    uq3 = w_uq.reshape(MLA_Q_RANK, MLA_HEADS, hq)
    zq = jnp.zeros((MLA_Q_RANK, MLA_HEADS, LANES - hq), F32)
    uq = jnp.concatenate([uq3, zq], -1).reshape(MLA_Q_RANK, MLA_HEADS * LANES)
    uq_rope_rot = _rot_half_cols(uq3[..., MLA_NOPE:].reshape(MLA_Q_RANK, -1), MLA_ROPE)
    uq_rope_rot = uq_rope_rot.reshape(MLA_Q_RANK, MLA_HEADS, MLA_ROPE)
    uqr = jnp.concatenate([jnp.zeros((MLA_Q_RANK, MLA_HEADS, MLA_NOPE), F32), uq_rope_rot, zq], -1)
    uqr = uqr.reshape(MLA_Q_RANK, MLA_HEADS * LANES)
    eye = jnp.eye(LRU_BLOCKS, dtype=F32)
    bd = lambda w: jnp.einsum("ncd,nm->ncmd", w, eye).reshape(LRU_WIDTH, LRU_WIDTH)
    w_ab = jnp.concatenate([bd(w_lru_a), bd(w_lru_x)], -1).astype(BF16)
    wa2 = jnp.concatenate([w_gla_a2, jnp.zeros((LANES - GLA_RANK, GLA_HEADS * GLA_DK), F32)], 0).astype(BF16)
    wr = jnp.concatenate([w_router, jnp.zeros((D, LANES - N_EXPERTS), F32)], -1).astype(BF16)
    br = jnp.concatenate([b_router, jnp.full((LANES - N_EXPERTS,), -jnp.inf, F32)]).reshape(1, LANES)
    row = lambda v: v.reshape(1, -1).astype(F32)
    return dict(
        w16=w16, w32=w32,
        g_cq=row(g_cq), g_ckv=row(g_ckv), w_uq=uq.astype(BF16), w_uq_rot=uqr.astype(BF16), w_ukv=w_ukv.astype(BF16),
        conv_w=conv_w, conv_b=row(conv_b), w_lru_ab=w_ab, b_lru_ab=row(jnp.concatenate([b_lru_a, b_lru_x])),
        lru_ncsp=row(-LRU_C * jax.nn.softplus(-lru_lambda)),
        w_gla_a2=wa2, b_gla_a2=row(b_gla_a2),
        w_merge=w_merge.astype(BF16), b_merge=row(b_merge), w_branch=w_branch.astype(BF16), w_out=w_out.astype(BF16),
        ln1_g=row(ln1_g), ln1_b=row(ln1_b), w_router=wr, b_router=br,
        ln2_g=row(ln2_g), ln2_b=row(ln2_b))


def _row_tile(n_rows, want):
    best = CHUNK
    for t in range(CHUNK, want + 1, CHUNK):
        if n_rows % t == 0:
            best = t
    return best


def kernel(x, meta_tokens, ln0_g, ln0_b, w_in, w_merge, b_merge, g_cq, g_ckv, w_uq, w_ukv, conv_w, conv_b, w_lru_a, b_lru_a, w_lru_x, b_lru_x, lru_lambda, w_gla_a2, b_gla_a2, w_branch, w_out, ln1_g, ln1_b, w_router, b_router, w_exp_gate, b_exp_gate, w_exp_up, b_exp_up, w_exp_down, b_exp_down, ln2_g, ln2_b):
    B, S, D = x.shape
    depth = w_in.shape[0]
    alpha = (2.0 * depth) ** 0.25
    tp = S + CHUNK
    nc = tp // CHUNK
    n_rows = B * tp
    tabs = _tables(tp)
    per_layer = (w_in, w_merge, b_merge, g_cq, g_ckv, w_uq, w_ukv, conv_w, conv_b, w_lru_a, b_lru_a, w_lru_x, b_lru_x,
                 lru_lambda, w_gla_a2, b_gla_a2, w_branch, w_out, ln1_g, ln1_b, w_router, b_router, ln2_g, ln2_b)
    n_le = depth * N_EXPERTS
    expert_wts = (w_exp_gate.reshape(n_le, D, D_EXPERT), b_exp_gate.reshape(n_le, 1, D_EXPERT),
                  w_exp_up.reshape(n_le, D, D_EXPERT), b_exp_up.reshape(n_le, 1, D_EXPERT),
                  w_exp_down.reshape(n_le, D_EXPERT, D), b_exp_down.reshape(n_le, 1, D))
    tm = _row_tile(tp, 640)
    h = _ln0(x, meta_tokens.astype(x.dtype), ln0_g, ln0_b).reshape(n_rows, D)
    for l in range(depth):
        wts = _layer_weights(*(p[l] for p in per_layer))
        p16 = _proj(h, wts["w16"], BF16, tm)
        p32 = _proj(h, wts["w32"], F32, tm)
        y_ret = _retention(p16, p32, tabs, B, nc)
        qt, kh, vt = _mla_prep(p16, tabs, wts, B, tp, tm)
        y_mla = _flash(qt, kh, vt, tm)
        y_lru = _lru(p32, wts, B, nc)
        y_gla = _gla(p16, p32, wts, B, nc)
        h1, h1p, route_i, route_w, cnt = _merge(h, (y_ret, y_mla, y_lru, y_gla), wts, alpha, tm)
        h = _moe(h1, h1p, route_i, route_w, cnt, wts, expert_wts, l, alpha, tm, tp)
    return h.reshape(B, tp, D)[:, CHUNK:]
```

```python
import functools

import jax
import jax.numpy as jnp
from jax import lax
from jax.experimental import pallas as pl
from jax.experimental.pallas import tpu as pltpu

F32 = jnp.float32
BF16 = jnp.bfloat16
U32 = jnp.uint32
I32 = jnp.int32

D_MODEL = 1024
N_META = 16
CHUNK = 128
PAD_FRONT = CHUNK - N_META
ROPE_BASE = 10000.0
NEG_BIG = -1e30

RET_HEADS, RET_DK, RET_DV = 4, 64, 128
MLA_HEADS, MLA_Q_RANK, MLA_KV_RANK, MLA_NOPE, MLA_ROPE, MLA_DV = 8, 256, 128, 64, 32, 64
LRU_WIDTH, LRU_BLOCKS, CONV_W, LRU_C = 512, 8, 4, 8.0
LRU_BLOCK = LRU_WIDTH // LRU_BLOCKS
GLA_HEADS, GLA_DK, GLA_DV, GLA_RANK, GLA_TAU, GLA_CHUNK = 4, 64, 128, 16, 16.0, 16
N_BRANCH, BRANCH_W = 4, 512
N_EXPERTS, TOP_K, D_EXPERT = 32, 4, 1024
SWIGLU_LIMIT, SWIGLU_ALPHA = 7.0, 1.702

LANES = 128
VMEM_LIMIT = 56 * 1024 * 1024
EXPERT_TILE = 512

_O_RQ, _O_RK, _O_RV, _O_RG = 0, 256, 512, 1024
_O_CQ, _O_CKV, _O_KR = 1536, 1792, 1920
_O_LX, _O_LG = 1952, 2464
_O_GQ, _O_GK, _O_GV, _O_GA, _O_GG = 2976, 3232, 3488, 4000, 4016


def _cparams(sem):
    return pltpu.CompilerParams(dimension_semantics=sem, vmem_limit_bytes=VMEM_LIMIT)


def _const_spec(shape):
    nd = len(shape)
    return pl.BlockSpec(shape, lambda *_: (0,) * nd, pipeline_mode=pl.Buffered(1))


def _ln_rows(x, g, b, eps=1e-5):
    mu = jnp.mean(x, -1, keepdims=True)
    xc = x - mu
    var = jnp.mean(xc * xc, -1, keepdims=True)
    return xc * lax.rsqrt(var + eps) * g + b


def _dot(a, b):
    return jnp.dot(a, b, preferred_element_type=F32)


def _dot_nt(a, b):
    return lax.dot_general(a, b, (((1,), (1,)), ((), ())), preferred_element_type=F32)


def _pack_bf16_pairs(x):
    n = x.shape[1] // 2
    bits = lax.bitcast_convert_type(x.astype(BF16).astype(F32), U32)
    return (bits[:, :n] & U32(0xFFFF0000)) | (bits[:, n:] >> 16)


def _unpack_bf16_pairs(w):
    hi = lax.bitcast_convert_type(w & U32(0xFFFF0000), F32)
    lo = lax.bitcast_convert_type(w << 16, F32)
    return jnp.concatenate([hi, lo], axis=1)


def _ln0_kernel(x_ref, meta_ref, g_ref, b_ref, o_ref):
    i = pl.program_id(1)
    g = g_ref[...]
    b = b_ref[...]

    @pl.when(i == 0)
    def _():
        o_ref[0, :PAD_FRONT, :] = jnp.zeros((PAD_FRONT, D_MODEL), F32)
        o_ref[0, PAD_FRONT:, :] = _ln_rows(meta_ref[...], g, b)

    @pl.when(i > 0)
    def _():
        o_ref[0] = _ln_rows(x_ref[0], g, b)


def _ln0(x, meta, g, b):
    B, S, D = x.shape
    nc = S // CHUNK + 1
    return pl.pallas_call(
        _ln0_kernel,
        grid=(B, nc),
        in_specs=[pl.BlockSpec((1, CHUNK, D), lambda bb, i: (bb, jnp.maximum(i - 1, 0), 0)),
                  pl.BlockSpec((N_META, D), lambda bb, i: (0, 0)),
                  pl.BlockSpec((1, D), lambda bb, i: (0, 0)),
                  pl.BlockSpec((1, D), lambda bb, i: (0, 0))],
        out_specs=pl.BlockSpec((1, CHUNK, D), lambda bb, i: (bb, i, 0)),
        out_shape=jax.ShapeDtypeStruct((B, nc * CHUNK, D), F32),
        name="ln0",
        compiler_params=_cparams(("parallel", "arbitrary")),
    )(x, meta, g.reshape(1, D), b.reshape(1, D))


def _proj_kernel(h_ref, w_ref, o_ref, *, col_step):
    h = h_ref[...].astype(BF16)
    n = w_ref.shape[1]
    for c0 in range(0, n, col_step):
        c1 = min(c0 + col_step, n)
        o_ref[:, c0:c1] = _dot(h, w_ref[:, c0:c1]).astype(o_ref.dtype)


def _proj(h, w, out_dtype, tm):
    n_rows, d = h.shape
    n = w.shape[1]
    return pl.pallas_call(
        functools.partial(_proj_kernel, col_step=512),
        grid=(n_rows // tm,),
        in_specs=[pl.BlockSpec((tm, d), lambda i: (i, 0)), _const_spec((d, n))],
        out_specs=pl.BlockSpec((tm, n), lambda i: (i, 0)),
        out_shape=jax.ShapeDtypeStruct((n_rows, n), out_dtype),
        name="in_proj",
        compiler_params=_cparams(("parallel",)),
    )(h, w)


def _half_masks(dtype):
    lane = lax.broadcasted_iota(I32, (1, LANES), 1)
    lo = (lane < LANES // 2).astype(dtype)
    return lo, (1 - lo).astype(dtype)


def _ret_kernel(cdec_ref, qk_ref, rot_ref, v_ref, g_ref, cos_ref, sin_ref, dec_ref, dmat_ref, o_ref, s_ref):
    c = pl.program_id(1)

    @pl.when(c == 0)
    def _():
        s_ref[...] = jnp.zeros_like(s_ref)

    r_all = qk_ref[...].astype(F32) * cos_ref[...] + rot_ref[...].astype(F32) * sin_ref[...]
    masks = _half_masks(F32)
    states = [s_ref[hd] for hd in range(RET_HEADS)]
    for ch in range(qk_ref.shape[0] // CHUNK):
        rows = slice(ch * CHUNK, (ch + 1) * CHUNK)
        r = r_all[rows]
        rd = r * dec_ref[...]
        for hd in range(RET_HEADS):
            pair, half = divmod(hd, 2)
            m = masks[half]
            q_lo, k_lo = pair * LANES, 2 * LANES + pair * LANES
            qp = (r[:, q_lo:q_lo + LANES] * m).astype(BF16)
            kp = r[:, k_lo:k_lo + LANES].astype(BF16)
            vh = v_ref[rows, hd * RET_DV:(hd + 1) * RET_DV]
            scores = _dot_nt(qp, kp) * dmat_ref[hd]
            o = _dot(scores.astype(BF16), vh)
            qin = (rd[:, q_lo:q_lo + LANES] * m).astype(BF16)
            o = o + _dot(qin, states[hd].astype(BF16))
            kdec_t = (rd[:, k_lo:k_lo + LANES] * m).T.astype(BF16)
            states[hd] = states[hd] * cdec_ref[hd] + _dot(kdec_t, vh)
            mu = jnp.mean(o, -1, keepdims=True)
            oc = o - mu
            var = jnp.mean(oc * oc, -1, keepdims=True)
            on = oc * lax.rsqrt(var + 1e-5)
            gate = g_ref[rows, hd * RET_DV:(hd + 1) * RET_DV]
            o_ref[rows, hd * RET_DV:(hd + 1) * RET_DV] = (gate * jax.nn.sigmoid(gate) * on).astype(o_ref.dtype)
    for hd in range(RET_HEADS):
        s_ref[hd] = states[hd]


def _retention(p16, p32, tabs, B, tp, tr):
    n_rows = p16.shape[0]
    nc = tp // tr
    row = lambda b, c: b * nc + c
    return pl.pallas_call(
        _ret_kernel,
        grid=(B, nc),
        in_specs=[pl.BlockSpec(memory_space=pltpu.SMEM),
                  pl.BlockSpec((tr, 512), lambda b, c: (row(b, c), 0)),
                  pl.BlockSpec((tr, 512), lambda b, c: (row(b, c), 1)),
                  pl.BlockSpec((tr, 512), lambda b, c: (row(b, c), 2)),
                  pl.BlockSpec((tr, 512), lambda b, c: (row(b, c), 0)),
                  pl.BlockSpec((tr, 512), lambda b, c: (c, 0)),
                  pl.BlockSpec((tr, 512), lambda b, c: (c, 0)),
                  _const_spec((CHUNK, 512)),
                  _const_spec((RET_HEADS, CHUNK, CHUNK))],
        out_specs=pl.BlockSpec((tr, 512), lambda b, c: (row(b, c), 0)),
        out_shape=jax.ShapeDtypeStruct((n_rows, 512), BF16),
        scratch_shapes=[pltpu.VMEM((RET_HEADS, LANES, RET_DV), F32)],
        name="retention",
        compiler_params=_cparams(("parallel", "arbitrary")),
    )(tabs["ret_cdec"], p16, p16, p16, p32, tabs["ret_cos"], tabs["ret_sin"], tabs["ret_dec"], tabs["ret_dmat"])


MLA_BIAS_LANE = MLA_NOPE + MLA_ROPE
M_INIT = -3.0e38
LOG2_E = 1.4426950408889634


def _mla_prep_kernel(lat_ref, krot_ref, cos_ref, sin_ref, gq_ref, gkv_ref, wq_ref, wqr_ref, wkv_ref,
                     qt_ref, k_ref, vt_ref):
    i = pl.program_id(1)
    tr = lat_ref.shape[0]
    lat = lat_ref[...].astype(F32)
    cq = lat[:, :MLA_Q_RANK]
    ckv = lat[:, MLA_Q_RANK:MLA_Q_RANK + MLA_KV_RANK]
    kr = lat[:, MLA_Q_RANK + MLA_KV_RANK:]
    cos = cos_ref[...]
    sin = sin_ref[...]
    cqn = (cq * lax.rsqrt(jnp.mean(cq * cq, -1, keepdims=True) + 1e-6) * gq_ref[...]).astype(BF16)
    ckvn = (ckv * lax.rsqrt(jnp.mean(ckv * ckv, -1, keepdims=True) + 1e-6) * gkv_ref[...]).astype(BF16)
    krr = kr * cos + krot_ref[...].astype(F32) * sin
    scale = (MLA_NOPE + MLA_ROPE) ** -0.5 * LOG2_E
    lane = lax.broadcasted_iota(I32, (tr, LANES), 1)
    row = lax.broadcasted_iota(I32, (tr, LANES), 0)
    is_nope = lane < MLA_NOPE
    is_bias = lane == MLA_BIAS_LANE
    kbias = jnp.where(is_bias & (i == 0) & (row < PAD_FRONT), NEG_BIG, 0.0)
    for hd in range(MLA_HEADS):
        sl = slice(hd * LANES, (hd + 1) * LANES)
        q = (_dot(cqn, wq_ref[:, sl]) * cos + _dot(cqn, wqr_ref[:, sl]) * sin) * scale
        qt_ref[0, hd] = jnp.where(is_bias, 1.0, q).T.astype(qt_ref.dtype)
        kv = _dot(ckvn, wkv_ref[:, sl])
        k_ref[0, hd] = (jnp.where(is_nope, kv, krr) + kbias).astype(k_ref.dtype)
        vt_ref[0, hd] = kv.T[MLA_NOPE:, :].astype(vt_ref.dtype)


def _mla_prep(p16, tabs, wts, B, tp, tr):
    nt = tp // tr
    assert tr >= PAD_FRONT
    row = lambda b, i: b * nt + i
    hm = jax.ShapeDtypeStruct((B, MLA_HEADS, tp, LANES), BF16)
    hm_t = jax.ShapeDtypeStruct((B, MLA_HEADS, LANES, tp), BF16)
    hm_v = jax.ShapeDtypeStruct((B, MLA_HEADS, MLA_DV, tp), BF16)
    hspec = pl.BlockSpec((1, MLA_HEADS, tr, LANES), lambda b, i: (b, 0, i, 0))
    hspec_t = pl.BlockSpec((1, MLA_HEADS, LANES, tr), lambda b, i: (b, 0, 0, i))
    hspec_v = pl.BlockSpec((1, MLA_HEADS, MLA_DV, tr), lambda b, i: (b, 0, 0, i))
    return pl.pallas_call(
        _mla_prep_kernel,
        grid=(B, nt),
        in_specs=[pl.BlockSpec((tr, 512), lambda b, i: (row(b, i), 3)),
                  pl.BlockSpec((tr, LANES), lambda b, i: (row(b, i), 3072 // LANES)),
                  pl.BlockSpec((tr, LANES), lambda b, i: (i, 0)),
                  pl.BlockSpec((tr, LANES), lambda b, i: (i, 0)),
                  _const_spec((1, MLA_Q_RANK)), _const_spec((1, MLA_KV_RANK)),
                  _const_spec((MLA_Q_RANK, MLA_HEADS * LANES)), _const_spec((MLA_Q_RANK, MLA_HEADS * LANES)),
                  _const_spec((MLA_KV_RANK, MLA_HEADS * LANES))],
        out_specs=[hspec_t, hspec, hspec_v],
        out_shape=[hm_t, hm, hm_v],
        name="mla_prep",
        compiler_params=_cparams(("parallel", "arbitrary")),
    )(p16, p16, tabs["mla_cos"], tabs["mla_sin"], wts["g_cq"], wts["g_ckv"], wts["w_uq"], wts["w_uq_rot"],
      wts["w_ukv"])


def _flash_kernel(qt_ref, k_ref, vt_ref, o_ref, *, tb):
    tp = k_ref.shape[2]
    nb = tp // tb
    ng = tb // LANES
    key_i = lax.broadcasted_iota(I32, (tb, LANES), 0)
    qry_i = lax.broadcasted_iota(I32, (tb, LANES), 1)

    for hh in range(2):
        def step(qts, j, stats, diag):
            k0 = pl.multiple_of(j * tb, tb)
            k = k_ref[0, hh, pl.ds(k0, tb), :]
            vt = vt_ref[0, hh, :, pl.ds(k0, tb)]
            vt = jnp.concatenate([vt, jnp.zeros_like(vt)], axis=0)
            new = []
            for g in range(ng):
                m, l, acc = stats[g]
                s = _dot(k, qts[g])
                if diag:
                    s = jnp.where(key_i <= qry_i + g * LANES, s, NEG_BIG)
                m_new = jnp.maximum(m, jnp.max(s, 0, keepdims=True))
                alpha = jnp.exp2(m - m_new)
                p = jnp.exp2(s - m_new)
                l_new = alpha * l + jnp.sum(p, 0, keepdims=True)
                acc_new = alpha * acc + _dot(vt, p.astype(BF16))
                new.append((m_new, l_new, acc_new))
            return tuple(new)

        def q_block(i, _):
            q0 = i * tb
            qts = [qt_ref[0, hh, :, pl.ds(pl.multiple_of(q0 + g * LANES, LANES), LANES)] for g in range(ng)]
            init = tuple((jnp.full((1, LANES), M_INIT, F32), jnp.zeros((1, LANES), F32),
                          jnp.zeros((LANES, LANES), F32)) for _ in range(ng))
            carry = lax.fori_loop(0, i, lambda j, c: step(qts, j, c, False), init)
            carry = step(qts, i, carry, True)
            for g in range(ng):
                m, l, acc = carry[g]
                rows = pl.ds(pl.multiple_of(q0 + g * LANES, LANES), LANES)
                o_ref[rows, hh * MLA_DV:(hh + 1) * MLA_DV] = (acc / l).T[:, :MLA_DV]
            return 0

        lax.fori_loop(0, nb, q_block, 0)


def _flash(qt, kh, vt, tb):
    B, H, tp, _ = kh.shape
    npair = H // 2
    spec = pl.BlockSpec((1, 2, tp, LANES), lambda b, j: (b, j, 0, 0))
    spec_t = pl.BlockSpec((1, 2, LANES, tp), lambda b, j: (b, j, 0, 0))
    spec_v = pl.BlockSpec((1, 2, MLA_DV, tp), lambda b, j: (b, j, 0, 0))
    return pl.pallas_call(
        functools.partial(_flash_kernel, tb=tb),
        grid=(B, npair),
        in_specs=[spec_t, spec, spec_v],
        out_specs=pl.BlockSpec((tp, LANES), lambda b, j: (b, j)),
        out_shape=jax.ShapeDtypeStruct((B * tp, npair * LANES), F32),
        name="mla_flash",
        compiler_params=_cparams(("parallel", "arbitrary")),
    )(qt, kh, vt)


def _lru_kernel(x_ref, g_ref, cw_ref, cb_ref, wab_ref, bab_ref, ncsp_ref, o_ref, hist_ref, h_ref):
    c = pl.program_id(1)
    tr = x_ref.shape[0]

    @pl.when(c == 0)
    def _():
        hist_ref[...] = jnp.zeros_like(hist_ref)
        h_ref[...] = jnp.zeros_like(h_ref)

    x = x_ref[...]
    xcat = jnp.concatenate([hist_ref[...], x], axis=0)
    hist_ref[...] = x[tr - 8:, :]
    xc = cb_ref[...] + cw_ref[CONV_W - 1:CONV_W, :] * x
    for j in range(CONV_W - 1):
        off = 8 - (CONV_W - 1) + j
        xc = xc + cw_ref[j:j + 1, :] * xcat[off:off + tr, :]
    z = _dot(xc.astype(BF16), wab_ref[...]) + bab_ref[...]
    r = jax.nn.sigmoid(z[:, :LRU_WIDTH])
    i = jax.nn.sigmoid(z[:, LRU_WIDTH:])
    log_a = ncsp_ref[...] * r
    a_all = jnp.exp(log_a)
    u_all = jnp.sqrt(1.0 - jnp.exp(2.0 * log_a)) * (i * xc)
    row = lax.broadcasted_iota(I32, (CHUNK, LRU_WIDTH), 0)
    h = h_ref[...]
    for sub in range(tr // CHUNK):
        rows = slice(sub * CHUNK, (sub + 1) * CHUNK)
        a = a_all[rows]
        u = u_all[rows]
        if sub == 0:
            u = jnp.where((c == 0) & (row < PAD_FRONT), 0.0, u)
        s = 1
        while s < CHUNK:
            keep = row >= s
            a_sh = jnp.where(keep, pltpu.roll(a, s, axis=0), 1.0)
            u_sh = jnp.where(keep, pltpu.roll(u, s, axis=0), 0.0)
            u = a * u_sh + u
            a = a * a_sh
            s *= 2
        hs = u + a * h
        h = hs[CHUNK - 1:CHUNK, :]
        o_ref[rows, :] = (hs * jax.nn.gelu(g_ref[rows, :])).astype(o_ref.dtype)
    h_ref[...] = h


def _lru(p32, wts, B, tp, tr):
    n_rows = p32.shape[0]
    nc = tp // tr
    row = lambda b, c: b * nc + c
    return pl.pallas_call(
        _lru_kernel,
        grid=(B, nc),
        in_specs=[pl.BlockSpec((tr, 512), lambda b, c: (row(b, c), 1)),
                  pl.BlockSpec((tr, 512), lambda b, c: (row(b, c), 2)),
                  _const_spec((CONV_W, LRU_WIDTH)), _const_spec((1, LRU_WIDTH)),
                  _const_spec((LRU_WIDTH, 2 * LRU_WIDTH)), _const_spec((1, 2 * LRU_WIDTH)),
                  _const_spec((1, LRU_WIDTH))],
        out_specs=pl.BlockSpec((tr, 512), lambda b, c: (row(b, c), 0)),
        out_shape=jax.ShapeDtypeStruct((n_rows, 512), BF16),
        scratch_shapes=[pltpu.VMEM((8, LRU_WIDTH), F32), pltpu.VMEM((1, LRU_WIDTH), F32)],
        name="rglru",
        compiler_params=_cparams(("parallel", "arbitrary")),
    )(p32, p32, wts["conv_w"], wts["conv_b"], wts["w_lru_ab"], wts["b_lru_ab"], wts["lru_ncsp"])


GLA_LEVELS = (64, 32, 16, 8, 4, 2, 1)


def _gla_kernel(qk_ref, v_ref, a_ref, g_ref, wa2_ref, ba2_ref, o_ref, st_ref):
    c = pl.program_id(1)
    tr = qk_ref.shape[0]
    hk = GLA_HEADS * GLA_DK

    @pl.when(c == 0)
    def _():
        st_ref[...] = jnp.zeros_like(st_ref)

    x = _dot(a_ref[...].astype(BF16), wa2_ref[...]) + ba2_ref[...]
    la = (jnp.minimum(x, 0.0) - jnp.log(1.0 + jnp.exp(-jnp.abs(x)))) * (1.0 / GLA_TAU)
    ri = lax.broadcasted_iota(I32, (tr, tr), 0)
    ci = lax.broadcasted_iota(I32, (tr, tr), 1)
    tri = jnp.where(ci <= ri, 1.0, 0.0).astype(BF16)
    la_hi = la.astype(BF16)
    rem = la - la_hi.astype(F32)
    la_mid = rem.astype(BF16)
    la_lo = (rem - la_mid.astype(F32)).astype(BF16)
    gcum = _dot(tri, la_hi) + _dot(tri, la_mid) + _dot(tri, la_lo)
    row = lax.broadcasted_iota(I32, (tr, hk), 0)
    own_end = {1: gcum}
    s = 1
    while s < GLA_LEVELS[0]:
        in_right = ((row >> (s.bit_length() - 1)) & 1) == 1
        own_end[2 * s] = jnp.where(in_right, own_end[s], pltpu.roll(own_end[s], tr - s, axis=0))
        s *= 2
    qk = qk_ref[...].astype(F32)
    q = qk[:, :hk]
    k = qk[:, hk:]
    terms = [(q.astype(BF16), k.astype(BF16), ri == ci)]
    for s in GLA_LEVELS:
        r_own = own_end[s]
        r_prev = pltpu.roll(r_own, s, axis=0)
        qs = (q * jnp.exp(jnp.minimum(gcum - r_prev, 0.0))).astype(BF16)
        ks = (k * jnp.exp(jnp.minimum(r_own - gcum, 0.0))).astype(BF16)
        sh = s.bit_length() - 1
        bi = ri >> sh
        terms.append((qs, ks, ((bi & 1) == 1) & ((ci >> sh) == bi - 1)))
    gend = gcum[tr - 1:tr, :]
    qg = (q * jnp.exp(gcum)).astype(BF16)
    kg = (k * jnp.exp(gend - gcum)).astype(BF16)
    dec = jnp.exp(gend)
    masks = _half_masks(BF16)
    for hd in range(GLA_HEADS):
        pair, half = divmod(hd, 2)
        lanes = slice(pair * LANES, (pair + 1) * LANES)
        m = masks[half]
        amat = None
        for qs, ks, keep in terms:
            term = jnp.where(keep, _dot_nt(qs[:, lanes] * m, ks[:, lanes]), 0.0)
            amat = term if amat is None else amat + term
        vh = v_ref[:, hd * GLA_DV:(hd + 1) * GLA_DV]
        st = st_ref[hd]
        o = _dot(amat.astype(BF16), vh) + _dot_nt(qg[:, lanes] * m, st.astype(BF16))
        st_ref[hd] = st * dec[:, lanes] + _dot(vh.astype(F32).T.astype(BF16), kg[:, lanes])
        on = o * lax.rsqrt(jnp.mean(o * o, -1, keepdims=True) + 1e-6)
        gate = g_ref[:, hd * GLA_DV:(hd + 1) * GLA_DV]
        o_ref[:, hd * GLA_DV:(hd + 1) * GLA_DV] = (gate * jax.nn.sigmoid(gate) * on).astype(o_ref.dtype)


def _gla(p16, p32, wts, B, nc):
    n_rows = p16.shape[0]
    row = lambda b, c: b * nc + c
    return pl.pallas_call(
        _gla_kernel,
        grid=(B, nc),
        in_specs=[pl.BlockSpec((CHUNK, 512), lambda b, c: (row(b, c), 4)),
                  pl.BlockSpec((CHUNK, 512), lambda b, c: (row(b, c), 5)),
                  pl.BlockSpec((CHUNK, LANES), lambda b, c: (row(b, c), 2048 // LANES)),
                  pl.BlockSpec((CHUNK, 512), lambda b, c: (row(b, c), 3)),
                  _const_spec((LANES, GLA_HEADS * GLA_DK)), _const_spec((1, GLA_HEADS * GLA_DK))],
        out_specs=pl.BlockSpec((CHUNK, 512), lambda b, c: (row(b, c), 0)),
        out_shape=jax.ShapeDtypeStruct((n_rows, 512), BF16),
        scratch_shapes=[pltpu.VMEM((GLA_HEADS, GLA_DV, LANES), F32)],
        name="gla",
        compiler_params=_cparams(("parallel", "arbitrary")),
    )(p16, p16, p32, p32, wts["w_gla_a2"], wts["b_gla_a2"])


def _merge_kernel(h_ref, y0_ref, y1_ref, y2_ref, y3_ref, wm_ref, bm_ref, wb_ref, wo_ref, g_ref, b_ref, wr_ref, br_ref,
                  h1_ref, h1p_ref, ri_ref, rw_ref, cnt_out_ref, cnt_ref, *, alpha):
    @pl.when(pl.program_id(0) == 0)
    def _():
        cnt_ref[...] = jnp.zeros_like(cnt_ref)

    h = h_ref[...]
    tm = h.shape[0]
    h16 = h.astype(BF16)
    ys = (y0_ref[...], y1_ref[...].astype(BF16), y2_ref[...], y3_ref[...])
    mixed = None
    for n in range(N_BRANCH):
        sl = slice(n * D_MODEL, (n + 1) * D_MODEL)
        gate = jax.nn.sigmoid(_dot(h16, wm_ref[:, sl]) + bm_ref[:, sl])
        term = gate * _dot(ys[n], wb_ref[n])
        mixed = term if mixed is None else mixed + term
    mix = _dot(mixed.astype(BF16), wo_ref[...])
    h1 = _ln_rows(alpha * h + mix, g_ref[...], b_ref[...])
    h1_ref[...] = h1
    h1p_ref[...] = _pack_bf16_pairs(h1)
    logits = _dot(h1.astype(BF16), wr_ref[...]) + br_ref[...]
    lane = lax.broadcasted_iota(I32, logits.shape, 1)
    vals, sels, idxs = [], [], []
    cur = logits
    for _ in range(TOP_K):
        mx = jnp.max(cur, -1, keepdims=True)
        idx = jnp.min(jnp.where(cur == mx, lane, LANES), -1, keepdims=True)
        sel = lane == idx
        vals.append(mx)
        sels.append(sel)
        idxs.append(idx)
        cur = jnp.where(sel, -jnp.inf, cur)
    es = [jnp.exp(v - vals[0]) for v in vals]
    den = es[0] + es[1] + es[2] + es[3]
    onehot = jnp.zeros_like(logits)
    for sel in sels:
        onehot = onehot + jnp.where(sel, 1.0, 0.0)
    rr = lax.broadcasted_iota(I32, (tm, tm), 0)
    cc = lax.broadcasted_iota(I32, (tm, tm), 1)
    lower = jnp.where(cc < rr, 1.0, 0.0).astype(BF16)
    base = cnt_ref[...] + _dot(lower, onehot.astype(BF16))
    route_i = jnp.zeros(logits.shape, I32)
    route_w = jnp.zeros_like(logits)
    for k in range(TOP_K):
        rank = jnp.sum(jnp.where(sels[k], base, 0.0), -1, keepdims=True).astype(I32)
        route_i = jnp.where(lane == k, idxs[k], route_i)
        route_i = jnp.where(lane == TOP_K + k, rank, route_i)
        route_w = jnp.where(lane == k, es[k] / den, route_w)
    ri_ref[...] = route_i
    rw_ref[...] = route_w
    cnt_ref[...] = cnt_ref[...] + jnp.sum(onehot, 0, keepdims=True)
    cnt_out_ref[...] = jnp.broadcast_to(cnt_ref[...], cnt_out_ref.shape)


def _merge(h, ys, wts, alpha, tm):
    n_rows = h.shape[0]
    rspec = lambda w: pl.BlockSpec((tm, w), lambda i: (i, 0))
    return pl.pallas_call(
        functools.partial(_merge_kernel, alpha=alpha),
        grid=(n_rows // tm,),
        in_specs=[rspec(D_MODEL), rspec(512), rspec(512), rspec(512), rspec(512),
                  _const_spec((D_MODEL, N_BRANCH * D_MODEL)), _const_spec((1, N_BRANCH * D_MODEL)),
                  _const_spec((N_BRANCH, BRANCH_W, D_MODEL)), _const_spec((D_MODEL, D_MODEL)),
                  _const_spec((1, D_MODEL)), _const_spec((1, D_MODEL)),
                  _const_spec((D_MODEL, LANES)), _const_spec((1, LANES))],
        out_specs=[rspec(D_MODEL), rspec(D_MODEL // 2), rspec(LANES), rspec(LANES),
                   pl.BlockSpec((8, LANES), lambda i: (0, 0))],
        out_shape=[jax.ShapeDtypeStruct((n_rows, D_MODEL), F32), jax.ShapeDtypeStruct((n_rows, D_MODEL // 2), U32),
                   jax.ShapeDtypeStruct((n_rows, LANES), I32), jax.ShapeDtypeStruct((n_rows, LANES), F32),
                   jax.ShapeDtypeStruct((8, LANES), F32)],
        scratch_shapes=[pltpu.VMEM((1, LANES), F32)],
        name="merge_ln1_router",
        compiler_params=_cparams(("arbitrary",)),
    )(h, *ys, wts["w_merge"], wts["b_merge"], wts["w_branch"], wts["w_out"], wts["ln1_g"], wts["ln1_b"],
      wts["w_router"], wts["b_router"])


def _route_plan(route_i, cnt, n_tiles):
    ids = route_i[:, :TOP_K]
    rank = route_i[:, TOP_K:2 * TOP_K]
    counts = cnt[0, :N_EXPERTS].astype(I32)
    padded = (counts + EXPERT_TILE - 1) // EXPERT_TILE * EXPERT_TILE
    gend = jnp.cumsum(padded)
    slot = jnp.take(gend - padded, ids) + rank
    n_used = gend[-1] // EXPERT_TILE
    tstart = jnp.arange(n_tiles, dtype=I32) * EXPERT_TILE
    te = jnp.minimum(jnp.sum(gend[None, :] <= tstart[:, None], axis=1), N_EXPERTS - 1).astype(I32)
    te = jnp.where(tstart < gend[-1], te, te[jnp.maximum(n_used - 1, 0)])
    return slot.astype(I32), te, n_used.reshape(1).astype(I32)


def _dispatch_kernel(slot_ref, x_ref, xs_in_ref, xs_ref, sem):
    del xs_in_ref
    tm = x_ref.shape[0]

    def issue(r, _):
        for k in range(TOP_K):
            pltpu.make_async_copy(x_ref.at[pl.ds(r, 1)], xs_ref.at[pl.ds(slot_ref[0, 0, TOP_K * r + k], 1)],
                                  sem).start()
        return 0

    lax.fori_loop(0, tm, issue, 0, unroll=8)
    for _ in range(TOP_K):
        pltpu.make_async_copy(x_ref, xs_ref.at[pl.ds(0, tm)], sem).wait()


def _dispatch(slot3, h1p, n_slots, tm):
    n_rows, w = h1p.shape
    return pl.pallas_call(
        _dispatch_kernel,
        grid=(n_rows // tm,),
        in_specs=[pl.BlockSpec((1, 1, TOP_K * tm), lambda i: (i, 0, 0), memory_space=pltpu.SMEM),
                  pl.BlockSpec((tm, w), lambda i: (i, 0)),
                  pl.BlockSpec(memory_space=pl.ANY)],
        out_specs=pl.BlockSpec(memory_space=pl.ANY),
        out_shape=jax.ShapeDtypeStruct((n_slots, w), U32),
        scratch_shapes=[pltpu.SemaphoreType.DMA(())],
        input_output_aliases={2: 0},
        name="moe_dispatch",
        compiler_params=_cparams(("arbitrary",)),
    )(slot3, h1p, jnp.zeros((n_slots, w), U32))


def _expert_kernel(te_ref, nu_ref, xs_ref, wg_ref, bg_ref, wu_ref, bu_ref, wd_ref, bd_ref, ys_ref, w16_ref):
    t = pl.program_id(0)

    @pl.when(t < nu_ref[0])
    def _():
        @pl.when((t == 0) | (te_ref[t] != te_ref[jnp.maximum(t - 1, 0)]))
        def _():
            w16_ref[0] = wg_ref[0].astype(BF16)
            w16_ref[1] = wu_ref[0].astype(BF16)
            w16_ref[2] = wd_ref[0].astype(BF16)

        x16 = _unpack_bf16_pairs(xs_ref[...]).astype(BF16)
        gate = jnp.minimum(_dot(x16, w16_ref[0]) + bg_ref[0], SWIGLU_LIMIT)
        up = jnp.clip(_dot(x16, w16_ref[1]) + bu_ref[0], -SWIGLU_LIMIT, SWIGLU_LIMIT)
        act = (up + 1.0) * gate * jax.nn.sigmoid(SWIGLU_ALPHA * gate)
        ys_ref[...] = _pack_bf16_pairs(_dot(act.astype(BF16), w16_ref[2]) + bd_ref[0])

    @pl.when(t >= nu_ref[0])
    def _():
        ys_ref[...] = jnp.zeros_like(ys_ref)


def _experts(te, n_used, xs, wts):
    n_slots, w = xs.shape
    n_tiles = n_slots // EXPERT_TILE
    tile = lambda t, te_ref, nu_ref: (jnp.minimum(t, nu_ref[0] - 1), 0)
    wsel = lambda t, te_ref, nu_ref: (te_ref[t], 0, 0)
    grid_spec = pltpu.PrefetchScalarGridSpec(
        num_scalar_prefetch=2,
        grid=(n_tiles,),
        in_specs=[pl.BlockSpec((EXPERT_TILE, w), tile),
                  pl.BlockSpec((1, D_MODEL, D_EXPERT), wsel), pl.BlockSpec((1, 1, D_EXPERT), wsel),
                  pl.BlockSpec((1, D_MODEL, D_EXPERT), wsel), pl.BlockSpec((1, 1, D_EXPERT), wsel),
                  pl.BlockSpec((1, D_EXPERT, D_MODEL), wsel), pl.BlockSpec((1, 1, D_MODEL), wsel)],
        out_specs=pl.BlockSpec((EXPERT_TILE, w), lambda t, te_ref, nu_ref: (t, 0)),
        scratch_shapes=[pltpu.VMEM((3, D_MODEL, D_EXPERT), BF16)])
    return pl.pallas_call(
        _expert_kernel,
        grid_spec=grid_spec,
        out_shape=jax.ShapeDtypeStruct((n_slots, w), U32),
        name="moe_experts",
        compiler_params=_cparams(("arbitrary",)),
    )(te, n_used, xs, *wts)


def _combine_kernel(slot_ref, x_ref, rw_ref, ys_ref, g_ref, b_ref, o_ref, buf_ref, sem, *, alpha, tiles_per_seq):
    tm = x_ref.shape[0]

    def issue(r, _):
        for k in range(TOP_K):
            pltpu.make_async_copy(ys_ref.at[pl.ds(slot_ref[0, 0, TOP_K * r + k], 1)], buf_ref.at[k, pl.ds(r, 1)],
                                  sem).start()
        return 0

    lax.fori_loop(0, tm, issue, 0, unroll=8)
    for k in range(TOP_K):
        pltpu.make_async_copy(ys_ref.at[pl.ds(0, tm)], buf_ref.at[k], sem).wait()
    rw = rw_ref[...]
    acc = None
    for k in range(TOP_K):
        term = rw[:, k:k + 1] * _unpack_bf16_pairs(buf_ref[k])
        acc = term if acc is None else acc + term
    out = _ln_rows(alpha * x_ref[...] + acc, g_ref[...], b_ref[...])
    row = lax.broadcasted_iota(I32, out.shape, 0)
    first_tile = lax.rem(pl.program_id(0), tiles_per_seq) == 0
    o_ref[...] = jnp.where(first_tile & (row < PAD_FRONT), 0.0, out)


def _combine(slot3, h1, route_w, ys, wts, alpha, tm, tp):
    n_rows = h1.shape[0]
    w = ys.shape[1]
    assert tp % tm == 0 and tm >= PAD_FRONT
    return pl.pallas_call(
        functools.partial(_combine_kernel, alpha=alpha, tiles_per_seq=tp // tm),
        grid=(n_rows // tm,),
        in_specs=[pl.BlockSpec((1, 1, TOP_K * tm), lambda i: (i, 0, 0), memory_space=pltpu.SMEM),
                  pl.BlockSpec((tm, D_MODEL), lambda i: (i, 0)),
                  pl.BlockSpec((tm, LANES), lambda i: (i, 0)),
                  pl.BlockSpec(memory_space=pl.ANY),
                  _const_spec((1, D_MODEL)), _const_spec((1, D_MODEL))],
        out_specs=pl.BlockSpec((tm, D_MODEL), lambda i: (i, 0)),
        out_shape=jax.ShapeDtypeStruct((n_rows, D_MODEL), F32),
        scratch_shapes=[pltpu.VMEM((TOP_K, tm, w), U32), pltpu.SemaphoreType.DMA(())],
        name="moe_combine_ln2",
        compiler_params=_cparams(("arbitrary",)),
    )(slot3, h1, route_w, ys, wts["ln2_g"], wts["ln2_b"])


def _moe(h1, h1p, route_i, route_w, cnt, wts, expert_wts, layer, alpha, tm, tp):
    n_rows = h1.shape[0]
    n_tiles = -(-TOP_K * n_rows // EXPERT_TILE) + N_EXPERTS
    slot, te, n_used = _route_plan(route_i, cnt, n_tiles)
    slot3 = slot.reshape(n_rows // tm, 1, TOP_K * tm)
    xs = _dispatch(slot3, h1p, n_tiles * EXPERT_TILE, tm)
    ys = _experts(te + layer * N_EXPERTS, n_used, xs, expert_wts)
    return _combine(slot3, h1, route_w, ys, wts, alpha, tm, tp)


def _rot_half_cols(w, d):
    n = w.shape[1] // d
    w3 = w.reshape(w.shape[0], n, d)
    return jnp.concatenate([-w3[..., d // 2:], w3[..., :d // 2]], -1).reshape(w.shape)


def _tables(tp):
    pos = jnp.maximum(jnp.arange(tp, dtype=I32) - PAD_FRONT, 0).astype(F32)

    def cs(d):
        inv = ROPE_BASE ** (-jnp.arange(0, d, 2, dtype=F32) / d)
        ang = pos[:, None] * inv[None, :]
        return jnp.cos(ang), jnp.sin(ang)

    rc, rs = cs(RET_DK)
    ret_cos = jnp.tile(jnp.concatenate([rc, rc], -1), (1, 2 * RET_HEADS))
    ret_sin = jnp.tile(jnp.concatenate([rs, rs], -1), (1, 2 * RET_HEADS))
    mc, ms = cs(MLA_ROPE)
    ones = jnp.ones((tp, MLA_NOPE), F32)
    tail1 = jnp.ones((tp, LANES - MLA_NOPE - MLA_ROPE), F32)
    mla_cos = jnp.concatenate([ones, mc, mc, tail1], -1)
    mla_sin = jnp.concatenate([0 * ones, ms, ms, 0 * tail1], -1)
    log_gamma = jnp.log1p(-jnp.exp2(-5.0 - jnp.arange(RET_HEADS, dtype=F32)))
    idx = jnp.arange(CHUNK, dtype=F32)
    rel = idx[:, None] - idx[None, :]
    dmat = jnp.where(rel >= 0, jnp.exp(log_gamma[:, None, None] * jnp.maximum(rel, 0.0)), 0.0)
    k_dec = jnp.exp(log_gamma[:, None] * (CHUNK - 1.0 - idx))
    q_dec = jnp.exp(log_gamma[:, None] * (idx + 1.0))
    rep = lambda t: jnp.repeat(t.T, RET_DK, axis=1)
    ret_dec = jnp.concatenate([rep(q_dec), rep(k_dec)], -1)
    ret_cdec = jnp.exp(log_gamma * CHUNK)
    return dict(ret_cos=ret_cos, ret_sin=ret_sin, mla_cos=mla_cos, mla_sin=mla_sin, ret_dmat=dmat, ret_dec=ret_dec,
                ret_cdec=ret_cdec)


def _layer_weights(w_in, w_merge, b_merge, g_cq, g_ckv, w_uq, w_ukv, conv_w, conv_b, w_lru_a, b_lru_a, w_lru_x,
                   b_lru_x, lru_lambda, w_gla_a2, b_gla_a2, w_branch, w_out, ln1_g, ln1_b, w_router, b_router,
                   ln2_g, ln2_b):
    D = D_MODEL
    z = lambda n: jnp.zeros((D, n), F32)
    rqk = jnp.concatenate([w_in[:, _O_RQ:_O_RK], w_in[:, _O_RK:_O_RV] * (RET_DK ** -0.5)], -1)
    kr = w_in[:, _O_KR:_O_LX]
    kr_slot = jnp.concatenate([z(MLA_NOPE), kr, z(LANES - MLA_NOPE - MLA_ROPE)], -1)
    kr_rot_slot = jnp.concatenate([z(MLA_NOPE), _rot_half_cols(kr, MLA_ROPE), z(LANES - MLA_NOPE - MLA_ROPE)], -1)
    gqk = jnp.concatenate([w_in[:, _O_GQ:_O_GK], w_in[:, _O_GK:_O_GV] * (GLA_DK ** -0.5)], -1)
    w16 = jnp.concatenate([rqk, _rot_half_cols(rqk, RET_DK), w_in[:, _O_RV:_O_RG],
                           w_in[:, _O_CQ:_O_KR], kr_slot, gqk, w_in[:, _O_GV:_O_GA], kr_rot_slot], -1).astype(BF16)
    w32 = jnp.concatenate([w_in[:, _O_RG:_O_CQ], w_in[:, _O_LX:_O_LG], w_in[:, _O_LG:_O_GQ], w_in[:, _O_GG:],
                           w_in[:, _O_GA:_O_GG], z(LANES - GLA_RANK)], -1).astype(BF16)
    hq = MLA_NOPE + MLA_ROPE
    uq3 = w_uq.reshape(MLA_Q_RANK, MLA_HEADS, hq)
    zq = jnp.zeros((MLA_Q_RANK, MLA_HEADS, LANES - hq), F32)
    uq = jnp.concatenate([uq3, zq], -1).reshape(MLA_Q_RANK, MLA_HEADS * LANES)
    uq_rope_rot = _rot_half_cols(uq3[..., MLA_NOPE:].reshape(MLA_Q_RANK, -1), MLA_ROPE)
    uq_rope_rot = uq_rope_rot.reshape(MLA_Q_RANK, MLA_HEADS, MLA_ROPE)
    uqr = jnp.concatenate([jnp.zeros((MLA_Q_RANK, MLA_HEADS, MLA_NOPE), F32), uq_rope_rot, zq], -1)
    uqr = uqr.reshape(MLA_Q_RANK, MLA_HEADS * LANES)
    eye = jnp.eye(LRU_BLOCKS, dtype=F32)
    bd = lambda w: jnp.einsum("ncd,nm->ncmd", w, eye).reshape(LRU_WIDTH, LRU_WIDTH)
    w_ab = jnp.concatenate([bd(w_lru_a), bd(w_lru_x)], -1).astype(BF16)
    wa2 = jnp.concatenate([w_gla_a2, jnp.zeros((LANES - GLA_RANK, GLA_HEADS * GLA_DK), F32)], 0).astype(BF16)
    wr = jnp.concatenate([w_router, jnp.zeros((D, LANES - N_EXPERTS), F32)], -1).astype(BF16)
    br = jnp.concatenate([b_router, jnp.full((LANES - N_EXPERTS,), -jnp.inf, F32)]).reshape(1, LANES)
    row = lambda v: v.reshape(1, -1).astype(F32)
    return dict(
        w16=w16, w32=w32,
        g_cq=row(g_cq), g_ckv=row(g_ckv), w_uq=uq.astype(BF16), w_uq_rot=uqr.astype(BF16), w_ukv=w_ukv.astype(BF16),
        conv_w=conv_w, conv_b=row(conv_b), w_lru_ab=w_ab, b_lru_ab=row(jnp.concatenate([b_lru_a, b_lru_x])),
        lru_ncsp=row(-LRU_C * jax.nn.softplus(-lru_lambda)),
        w_gla_a2=wa2, b_gla_a2=row(b_gla_a2),
        w_merge=w_merge.astype(BF16), b_merge=row(b_merge), w_branch=w_branch.astype(BF16), w_out=w_out.astype(BF16),
        ln1_g=row(ln1_g), ln1_b=row(ln1_b), w_router=wr, b_router=br,
        ln2_g=row(ln2_g), ln2_b=row(ln2_b))


def _row_tile(n_rows, want):
    best = CHUNK
    for t in range(CHUNK, want + 1, CHUNK):
        if n_rows % t == 0:
            best = t
    return best


def kernel(x, meta_tokens, ln0_g, ln0_b, w_in, w_merge, b_merge, g_cq, g_ckv, w_uq, w_ukv, conv_w, conv_b, w_lru_a, b_lru_a, w_lru_x, b_lru_x, lru_lambda, w_gla_a2, b_gla_a2, w_branch, w_out, ln1_g, ln1_b, w_router, b_router, w_exp_gate, b_exp_gate, w_exp_up, b_exp_up, w_exp_down, b_exp_down, ln2_g, ln2_b):
    B, S, D = x.shape
    depth = w_in.shape[0]
    alpha = (2.0 * depth) ** 0.25
    tp = S + CHUNK
    nc = tp // CHUNK
    n_rows = B * tp
    tabs = _tables(tp)
    per_layer = (w_in, w_merge, b_merge, g_cq, g_ckv, w_uq, w_ukv, conv_w, conv_b, w_lru_a, b_lru_a, w_lru_x, b_lru_x,
                 lru_lambda, w_gla_a2, b_gla_a2, w_branch, w_out, ln1_g, ln1_b, w_router, b_router, ln2_g, ln2_b)
    n_le = depth * N_EXPERTS
    expert_wts = (w_exp_gate.reshape(n_le, D, D_EXPERT), b_exp_gate.reshape(n_le, 1, D_EXPERT),
                  w_exp_up.reshape(n_le, D, D_EXPERT), b_exp_up.reshape(n_le, 1, D_EXPERT),
                  w_exp_down.reshape(n_le, D_EXPERT, D), b_exp_down.reshape(n_le, 1, D))
    tm = _row_tile(tp, 640)
    h = _ln0(x, meta_tokens.astype(x.dtype), ln0_g, ln0_b).reshape(n_rows, D)
    for l in range(depth):
        wts = _layer_weights(*(p[l] for p in per_layer))
        p16 = _proj(h, wts["w16"], BF16, tm)
        p32 = _proj(h, wts["w32"], F32, tm)
        y_ret = _retention(p16, p32, tabs, B, tp, tm)
        qt, kh, vt = _mla_prep(p16, tabs, wts, B, tp, tm)
        y_mla = _flash(qt, kh, vt, tm)
        y_lru = _lru(p32, wts, B, tp, tm)
        y_gla = _gla(p16, p32, wts, B, nc)
        h1, h1p, route_i, route_w, cnt = _merge(h, (y_ret, y_mla, y_lru, y_gla), wts, alpha, tm)
        h = _moe(h1, h1p, route_i, route_w, cnt, wts, expert_wts, l, alpha, tm, tp)
    return h.reshape(B, tp, D)[:, CHUNK:]
```

```python
import functools

import jax
import jax.numpy as jnp
from jax import lax
from jax.experimental import pallas as pl
from jax.experimental.pallas import tpu as pltpu

F32 = jnp.float32
BF16 = jnp.bfloat16
U32 = jnp.uint32
I32 = jnp.int32

D_MODEL = 1024
N_META = 16
CHUNK = 128
PAD_FRONT = CHUNK - N_META
ROPE_BASE = 10000.0
NEG_BIG = -1e30

RET_HEADS, RET_DK, RET_DV = 4, 64, 128
MLA_HEADS, MLA_Q_RANK, MLA_KV_RANK, MLA_NOPE, MLA_ROPE, MLA_DV = 8, 256, 128, 64, 32, 64
LRU_WIDTH, LRU_BLOCKS, CONV_W, LRU_C = 512, 8, 4, 8.0
LRU_BLOCK = LRU_WIDTH // LRU_BLOCKS
GLA_HEADS, GLA_DK, GLA_DV, GLA_RANK, GLA_TAU, GLA_CHUNK = 4, 64, 128, 16, 16.0, 16
N_BRANCH, BRANCH_W = 4, 512
N_EXPERTS, TOP_K, D_EXPERT = 32, 4, 1024
SWIGLU_LIMIT, SWIGLU_ALPHA = 7.0, 1.702

LANES = 128
VMEM_LIMIT = 56 * 1024 * 1024
EXPERT_TILE = 512

_O_RQ, _O_RK, _O_RV, _O_RG = 0, 256, 512, 1024
_O_CQ, _O_CKV, _O_KR = 1536, 1792, 1920
_O_LX, _O_LG = 1952, 2464
_O_GQ, _O_GK, _O_GV, _O_GA, _O_GG = 2976, 3232, 3488, 4000, 4016


def _cparams(sem):
    return pltpu.CompilerParams(dimension_semantics=sem, vmem_limit_bytes=VMEM_LIMIT)


def _const_spec(shape):
    nd = len(shape)
    return pl.BlockSpec(shape, lambda *_: (0,) * nd, pipeline_mode=pl.Buffered(1))


def _ln_rows(x, g, b, eps=1e-5):
    mu = jnp.mean(x, -1, keepdims=True)
    xc = x - mu
    var = jnp.mean(xc * xc, -1, keepdims=True)
    return xc * lax.rsqrt(var + eps) * g + b


def _dot(a, b):
    return jnp.dot(a, b, preferred_element_type=F32)


def _dot_nt(a, b):
    return lax.dot_general(a, b, (((1,), (1,)), ((), ())), preferred_element_type=F32)


def _pack_bf16_pairs(x):
    n = x.shape[1] // 2
    bits = lax.bitcast_convert_type(x.astype(BF16).astype(F32), U32)
    return (bits[:, :n] & U32(0xFFFF0000)) | (bits[:, n:] >> 16)


def _unpack_bf16_pairs(w):
    hi = lax.bitcast_convert_type(w & U32(0xFFFF0000), F32)
    lo = lax.bitcast_convert_type(w << 16, F32)
    return jnp.concatenate([hi, lo], axis=1)


def _ln0_kernel(x_ref, meta_ref, g_ref, b_ref, o_ref):
    i = pl.program_id(1)
    g = g_ref[...]
    b = b_ref[...]

    @pl.when(i == 0)
    def _():
        o_ref[0, :PAD_FRONT, :] = jnp.zeros((PAD_FRONT, D_MODEL), F32)
        o_ref[0, PAD_FRONT:, :] = _ln_rows(meta_ref[...], g, b)

    @pl.when(i > 0)
    def _():
        o_ref[0] = _ln_rows(x_ref[0], g, b)


def _ln0(x, meta, g, b):
    B, S, D = x.shape
    nc = S // CHUNK + 1
    return pl.pallas_call(
        _ln0_kernel,
        grid=(B, nc),
        in_specs=[pl.BlockSpec((1, CHUNK, D), lambda bb, i: (bb, jnp.maximum(i - 1, 0), 0)),
                  pl.BlockSpec((N_META, D), lambda bb, i: (0, 0)),
                  pl.BlockSpec((1, D), lambda bb, i: (0, 0)),
                  pl.BlockSpec((1, D), lambda bb, i: (0, 0))],
        out_specs=pl.BlockSpec((1, CHUNK, D), lambda bb, i: (bb, i, 0)),
        out_shape=jax.ShapeDtypeStruct((B, nc * CHUNK, D), F32),
        name="ln0",
        compiler_params=_cparams(("parallel", "arbitrary")),
    )(x, meta, g.reshape(1, D), b.reshape(1, D))


def _proj_kernel(h_ref, w_ref, o_ref, *, col_step):
    h = h_ref[...].astype(BF16)
    n = w_ref.shape[1]
    for c0 in range(0, n, col_step):
        c1 = min(c0 + col_step, n)
        o_ref[:, c0:c1] = _dot(h, w_ref[:, c0:c1]).astype(o_ref.dtype)


def _proj(h, w, out_dtype, tm):
    n_rows, d = h.shape
    n = w.shape[1]
    return pl.pallas_call(
        functools.partial(_proj_kernel, col_step=512),
        grid=(n_rows // tm,),
        in_specs=[pl.BlockSpec((tm, d), lambda i: (i, 0)), _const_spec((d, n))],
        out_specs=pl.BlockSpec((tm, n), lambda i: (i, 0)),
        out_shape=jax.ShapeDtypeStruct((n_rows, n), out_dtype),
        name="in_proj",
        compiler_params=_cparams(("parallel",)),
    )(h, w)


def _half_masks(dtype):
    lane = lax.broadcasted_iota(I32, (1, LANES), 1)
    lo = (lane < LANES // 2).astype(dtype)
    return lo, (1 - lo).astype(dtype)


def _ret_kernel(cdec_ref, qk_ref, rot_ref, v_ref, g_ref, cos_ref, sin_ref, dec_ref, dmat_ref, o_ref, s_ref):
    c = pl.program_id(1)

    @pl.when(c == 0)
    def _():
        s_ref[...] = jnp.zeros_like(s_ref)

    r_all = qk_ref[...].astype(F32) * cos_ref[...] + rot_ref[...].astype(F32) * sin_ref[...]
    masks = _half_masks(F32)
    states = [s_ref[hd] for hd in range(RET_HEADS)]
    for ch in range(qk_ref.shape[0] // CHUNK):
        rows = slice(ch * CHUNK, (ch + 1) * CHUNK)
        r = r_all[rows]
        rd = r * dec_ref[...]
        for hd in range(RET_HEADS):
            pair, half = divmod(hd, 2)
            m = masks[half]
            q_lo, k_lo = pair * LANES, 2 * LANES + pair * LANES
            qp = (r[:, q_lo:q_lo + LANES] * m).astype(BF16)
            kp = r[:, k_lo:k_lo + LANES].astype(BF16)
            vh = v_ref[rows, hd * RET_DV:(hd + 1) * RET_DV]
            scores = _dot_nt(qp, kp) * dmat_ref[hd]
            o = _dot(scores.astype(BF16), vh)
            qin = (rd[:, q_lo:q_lo + LANES] * m).astype(BF16)
            o = o + _dot(qin, states[hd].astype(BF16))
            kdec_t = (rd[:, k_lo:k_lo + LANES] * m).T.astype(BF16)
            states[hd] = states[hd] * cdec_ref[hd] + _dot(kdec_t, vh)
            mu = jnp.mean(o, -1, keepdims=True)
            oc = o - mu
            var = jnp.mean(oc * oc, -1, keepdims=True)
            on = oc * lax.rsqrt(var + 1e-5)
            gate = g_ref[rows, hd * RET_DV:(hd + 1) * RET_DV]
            o_ref[rows, hd * RET_DV:(hd + 1) * RET_DV] = (gate * jax.nn.sigmoid(gate) * on).astype(o_ref.dtype)
    for hd in range(RET_HEADS):
        s_ref[hd] = states[hd]


def _retention(p16, p32, tabs, B, tp, tr):
    n_rows = p16.shape[0]
    nc = tp // tr
    row = lambda b, c: b * nc + c
    return pl.pallas_call(
        _ret_kernel,
        grid=(B, nc),
        in_specs=[pl.BlockSpec(memory_space=pltpu.SMEM),
                  pl.BlockSpec((tr, 512), lambda b, c: (row(b, c), 0)),
                  pl.BlockSpec((tr, 512), lambda b, c: (row(b, c), 1)),
                  pl.BlockSpec((tr, 512), lambda b, c: (row(b, c), 2)),
                  pl.BlockSpec((tr, 512), lambda b, c: (row(b, c), 0)),
                  pl.BlockSpec((tr, 512), lambda b, c: (c, 0)),
                  pl.BlockSpec((tr, 512), lambda b, c: (c, 0)),
                  _const_spec((CHUNK, 512)),
                  _const_spec((RET_HEADS, CHUNK, CHUNK))],
        out_specs=pl.BlockSpec((tr, 512), lambda b, c: (row(b, c), 0)),
        out_shape=jax.ShapeDtypeStruct((n_rows, 512), BF16),
        scratch_shapes=[pltpu.VMEM((RET_HEADS, LANES, RET_DV), F32)],
        name="retention",
        compiler_params=_cparams(("parallel", "arbitrary")),
    )(tabs["ret_cdec"], p16, p16, p16, p32, tabs["ret_cos"], tabs["ret_sin"], tabs["ret_dec"], tabs["ret_dmat"])


MLA_BIAS_LANE = MLA_NOPE + MLA_ROPE
M_INIT = -3.0e38
LOG2_E = 1.4426950408889634


def _mla_prep_kernel(lat_ref, krot_ref, cos_ref, sin_ref, gq_ref, gkv_ref, wq_ref, wqr_ref, wkv_ref,
                     qt_ref, k_ref, vt_ref):
    i = pl.program_id(1)
    tr = lat_ref.shape[0]
    lat = lat_ref[...].astype(F32)
    cq = lat[:, :MLA_Q_RANK]
    ckv = lat[:, MLA_Q_RANK:MLA_Q_RANK + MLA_KV_RANK]
    kr = lat[:, MLA_Q_RANK + MLA_KV_RANK:]
    cos = cos_ref[...]
    sin = sin_ref[...]
    cqn = (cq * lax.rsqrt(jnp.mean(cq * cq, -1, keepdims=True) + 1e-6) * gq_ref[...]).astype(BF16)
    ckvn = (ckv * lax.rsqrt(jnp.mean(ckv * ckv, -1, keepdims=True) + 1e-6) * gkv_ref[...]).astype(BF16)
    krr = kr * cos + krot_ref[...].astype(F32) * sin
    scale = (MLA_NOPE + MLA_ROPE) ** -0.5 * LOG2_E
    lane = lax.broadcasted_iota(I32, (tr, LANES), 1)
    row = lax.broadcasted_iota(I32, (tr, LANES), 0)
    is_nope = lane < MLA_NOPE
    is_bias = lane == MLA_BIAS_LANE
    kbias = jnp.where(is_bias & (i == 0) & (row < PAD_FRONT), NEG_BIG, 0.0)
    for hd in range(MLA_HEADS):
        sl = slice(hd * LANES, (hd + 1) * LANES)
        q = (_dot(cqn, wq_ref[:, sl]) * cos + _dot(cqn, wqr_ref[:, sl]) * sin) * scale
        qt_ref[0, hd] = jnp.where(is_bias, 1.0, q).T.astype(qt_ref.dtype)
        kv = _dot(ckvn, wkv_ref[:, sl])
        k_ref[0, hd] = (jnp.where(is_nope, kv, krr) + kbias).astype(k_ref.dtype)
        vt_ref[0, hd] = kv.T[MLA_NOPE:, :].astype(vt_ref.dtype)


def _mla_prep(p16, tabs, wts, B, tp, tr):
    nt = tp // tr
    assert tr >= PAD_FRONT
    row = lambda b, i: b * nt + i
    hm = jax.ShapeDtypeStruct((B, MLA_HEADS, tp, LANES), BF16)
    hm_t = jax.ShapeDtypeStruct((B, MLA_HEADS, LANES, tp), BF16)
    hm_v = jax.ShapeDtypeStruct((B, MLA_HEADS, MLA_DV, tp), BF16)
    hspec = pl.BlockSpec((1, MLA_HEADS, tr, LANES), lambda b, i: (b, 0, i, 0))
    hspec_t = pl.BlockSpec((1, MLA_HEADS, LANES, tr), lambda b, i: (b, 0, 0, i))
    hspec_v = pl.BlockSpec((1, MLA_HEADS, MLA_DV, tr), lambda b, i: (b, 0, 0, i))
    return pl.pallas_call(
        _mla_prep_kernel,
        grid=(B, nt),
        in_specs=[pl.BlockSpec((tr, 512), lambda b, i: (row(b, i), 3)),
                  pl.BlockSpec((tr, LANES), lambda b, i: (row(b, i), 3072 // LANES)),
                  pl.BlockSpec((tr, LANES), lambda b, i: (i, 0)),
                  pl.BlockSpec((tr, LANES), lambda b, i: (i, 0)),
                  _const_spec((1, MLA_Q_RANK)), _const_spec((1, MLA_KV_RANK)),
                  _const_spec((MLA_Q_RANK, MLA_HEADS * LANES)), _const_spec((MLA_Q_RANK, MLA_HEADS * LANES)),
                  _const_spec((MLA_KV_RANK, MLA_HEADS * LANES))],
        out_specs=[hspec_t, hspec, hspec_v],
        out_shape=[hm_t, hm, hm_v],
        name="mla_prep",
        compiler_params=_cparams(("parallel", "arbitrary")),
    )(p16, p16, tabs["mla_cos"], tabs["mla_sin"], wts["g_cq"], wts["g_ckv"], wts["w_uq"], wts["w_uq_rot"],
      wts["w_ukv"])


def _flash_kernel(qt_ref, k_ref, vt_ref, o_ref, *, tb):
    tp = k_ref.shape[2]
    nb = tp // tb
    ng = tb // LANES
    for hh in range(2):
        def step(qts, j, stats, diag):
            k0 = pl.multiple_of(j * tb, tb)
            new = []
            for g in range(ng):
                m, l, acc = stats[g]
                nk = (g + 1) * LANES if diag else tb
                k = k_ref[0, hh, pl.ds(k0, nk), :]
                vt = vt_ref[0, hh, :, pl.ds(k0, nk)]
                vt = jnp.concatenate([vt, jnp.zeros_like(vt)], axis=0)
                s = _dot(k, qts[g])
                if diag:
                    key_i = lax.broadcasted_iota(I32, (nk, LANES), 0)
                    qry_i = lax.broadcasted_iota(I32, (nk, LANES), 1)
                    s = jnp.where(key_i <= qry_i + g * LANES, s, NEG_BIG)
                m_new = jnp.maximum(m, jnp.max(s, 0, keepdims=True))
                alpha = jnp.exp2(m - m_new)
                p = jnp.exp2(s - m_new)
                l_new = alpha * l + jnp.sum(p, 0, keepdims=True)
                acc_new = alpha * acc + _dot(vt, p.astype(BF16))
                new.append((m_new, l_new, acc_new))
            return tuple(new)

        def q_block(i, _):
            q0 = i * tb
            qts = [qt_ref[0, hh, :, pl.ds(pl.multiple_of(q0 + g * LANES, LANES), LANES)] for g in range(ng)]
            init = tuple((jnp.full((1, LANES), M_INIT, F32), jnp.zeros((1, LANES), F32),
                          jnp.zeros((LANES, LANES), F32)) for _ in range(ng))
            carry = lax.fori_loop(0, i, lambda j, c: step(qts, j, c, False), init)
            carry = step(qts, i, carry, True)
            for g in range(ng):
                m, l, acc = carry[g]
                rows = pl.ds(pl.multiple_of(q0 + g * LANES, LANES), LANES)
                o_ref[rows, hh * MLA_DV:(hh + 1) * MLA_DV] = (acc / l).T[:, :MLA_DV]
            return 0

        lax.fori_loop(0, nb, q_block, 0)


def _flash(qt, kh, vt, tb):
    B, H, tp, _ = kh.shape
    npair = H // 2
    spec = pl.BlockSpec((1, 2, tp, LANES), lambda b, j: (b, j, 0, 0))
    spec_t = pl.BlockSpec((1, 2, LANES, tp), lambda b, j: (b, j, 0, 0))
    spec_v = pl.BlockSpec((1, 2, MLA_DV, tp), lambda b, j: (b, j, 0, 0))
    return pl.pallas_call(
        functools.partial(_flash_kernel, tb=tb),
        grid=(B, npair),
        in_specs=[spec_t, spec, spec_v],
        out_specs=pl.BlockSpec((tp, LANES), lambda b, j: (b, j)),
        out_shape=jax.ShapeDtypeStruct((B * tp, npair * LANES), F32),
        name="mla_flash",
        compiler_params=_cparams(("parallel", "arbitrary")),
    )(qt, kh, vt)


def _lru_kernel(x_ref, g_ref, cw_ref, cb_ref, wab_ref, bab_ref, ncsp_ref, o_ref, hist_ref, h_ref):
    c = pl.program_id(1)
    tr = x_ref.shape[0]

    @pl.when(c == 0)
    def _():
        hist_ref[...] = jnp.zeros_like(hist_ref)
        h_ref[...] = jnp.zeros_like(h_ref)

    x = x_ref[...]
    xcat = jnp.concatenate([hist_ref[...], x], axis=0)
    hist_ref[...] = x[tr - 8:, :]
    xc = cb_ref[...] + cw_ref[CONV_W - 1:CONV_W, :] * x
    for j in range(CONV_W - 1):
        off = 8 - (CONV_W - 1) + j
        xc = xc + cw_ref[j:j + 1, :] * xcat[off:off + tr, :]
    z = _dot(xc.astype(BF16), wab_ref[...]) + bab_ref[...]
    r = jax.nn.sigmoid(z[:, :LRU_WIDTH])
    i = jax.nn.sigmoid(z[:, LRU_WIDTH:])
    log_a = ncsp_ref[...] * r
    a_all = jnp.exp(log_a)
    u_all = jnp.sqrt(1.0 - jnp.exp(2.0 * log_a)) * (i * xc)
    row = lax.broadcasted_iota(I32, (CHUNK, LRU_WIDTH), 0)
    h = h_ref[...]
    for sub in range(tr // CHUNK):
        rows = slice(sub * CHUNK, (sub + 1) * CHUNK)
        a = a_all[rows]
        u = u_all[rows]
        if sub == 0:
            u = jnp.where((c == 0) & (row < PAD_FRONT), 0.0, u)
        s = 1
        while s < CHUNK:
            keep = row >= s
            a_sh = jnp.where(keep, pltpu.roll(a, s, axis=0), 1.0)
            u_sh = jnp.where(keep, pltpu.roll(u, s, axis=0), 0.0)
            u = a * u_sh + u
            a = a * a_sh
            s *= 2
        hs = u + a * h
        h = hs[CHUNK - 1:CHUNK, :]
        o_ref[rows, :] = (hs * jax.nn.gelu(g_ref[rows, :])).astype(o_ref.dtype)
    h_ref[...] = h


def _lru(p32, wts, B, tp, tr):
    n_rows = p32.shape[0]
    nc = tp // tr
    row = lambda b, c: b * nc + c
    return pl.pallas_call(
        _lru_kernel,
        grid=(B, nc),
        in_specs=[pl.BlockSpec((tr, 512), lambda b, c: (row(b, c), 1)),
                  pl.BlockSpec((tr, 512), lambda b, c: (row(b, c), 2)),
                  _const_spec((CONV_W, LRU_WIDTH)), _const_spec((1, LRU_WIDTH)),
                  _const_spec((LRU_WIDTH, 2 * LRU_WIDTH)), _const_spec((1, 2 * LRU_WIDTH)),
                  _const_spec((1, LRU_WIDTH))],
        out_specs=pl.BlockSpec((tr, 512), lambda b, c: (row(b, c), 0)),
        out_shape=jax.ShapeDtypeStruct((n_rows, 512), BF16),
        scratch_shapes=[pltpu.VMEM((8, LRU_WIDTH), F32), pltpu.VMEM((1, LRU_WIDTH), F32)],
        name="rglru",
        compiler_params=_cparams(("parallel", "arbitrary")),
    )(p32, p32, wts["conv_w"], wts["conv_b"], wts["w_lru_ab"], wts["b_lru_ab"], wts["lru_ncsp"])


GLA_LEVELS = (64, 32, 16, 8, 4, 2, 1)


def _gla_kernel(qk_ref, v_ref, a_ref, g_ref, wa2_ref, ba2_ref, o_ref, st_ref):
    c = pl.program_id(1)
    tr = qk_ref.shape[0]
    hk = GLA_HEADS * GLA_DK

    @pl.when(c == 0)
    def _():
        st_ref[...] = jnp.zeros_like(st_ref)

    x = _dot(a_ref[...].astype(BF16), wa2_ref[...]) + ba2_ref[...]
    la = (jnp.minimum(x, 0.0) - jnp.log(1.0 + jnp.exp(-jnp.abs(x)))) * (1.0 / GLA_TAU)
    ri = lax.broadcasted_iota(I32, (tr, tr), 0)
    ci = lax.broadcasted_iota(I32, (tr, tr), 1)
    tri = jnp.where(ci <= ri, 1.0, 0.0).astype(BF16)
    la_hi = la.astype(BF16)
    rem = la - la_hi.astype(F32)
    la_mid = rem.astype(BF16)
    la_lo = (rem - la_mid.astype(F32)).astype(BF16)
    gcum = _dot(tri, la_hi) + _dot(tri, la_mid) + _dot(tri, la_lo)
    row = lax.broadcasted_iota(I32, (tr, hk), 0)
    own_end = {1: gcum}
    s = 1
    while s < GLA_LEVELS[0]:
        in_right = ((row >> (s.bit_length() - 1)) & 1) == 1
        own_end[2 * s] = jnp.where(in_right, own_end[s], pltpu.roll(own_end[s], tr - s, axis=0))
        s *= 2
    qk = qk_ref[...].astype(F32)
    q = qk[:, :hk]
    k = qk[:, hk:]
    terms = [(q.astype(BF16), k.astype(BF16), ri == ci)]
    for s in GLA_LEVELS:
        r_own = own_end[s]
        r_prev = pltpu.roll(r_own, s, axis=0)
        qs = (q * jnp.exp(jnp.minimum(gcum - r_prev, 0.0))).astype(BF16)
        ks = (k * jnp.exp(jnp.minimum(r_own - gcum, 0.0))).astype(BF16)
        sh = s.bit_length() - 1
        bi = ri >> sh
        terms.append((qs, ks, ((bi & 1) == 1) & ((ci >> sh) == bi - 1)))
    gend = gcum[tr - 1:tr, :]
    qg = (q * jnp.exp(gcum)).astype(BF16)
    kg = (k * jnp.exp(gend - gcum)).astype(BF16)
    dec = jnp.exp(gend)
    masks = _half_masks(BF16)
    for hd in range(GLA_HEADS):
        pair, half = divmod(hd, 2)
        lanes = slice(pair * LANES, (pair + 1) * LANES)
        m = masks[half]
        amat = None
        for qs, ks, keep in terms:
            term = jnp.where(keep, _dot_nt(qs[:, lanes] * m, ks[:, lanes]), 0.0)
            amat = term if amat is None else amat + term
        vh = v_ref[:, hd * GLA_DV:(hd + 1) * GLA_DV]
        st = st_ref[hd]
        o = _dot(amat.astype(BF16), vh) + _dot_nt(qg[:, lanes] * m, st.astype(BF16))
        st_ref[hd] = st * dec[:, lanes] + _dot(vh.astype(F32).T.astype(BF16), kg[:, lanes])
        on = o * lax.rsqrt(jnp.mean(o * o, -1, keepdims=True) + 1e-6)
        gate = g_ref[:, hd * GLA_DV:(hd + 1) * GLA_DV]
        o_ref[:, hd * GLA_DV:(hd + 1) * GLA_DV] = (gate * jax.nn.sigmoid(gate) * on).astype(o_ref.dtype)


def _gla(p16, p32, wts, B, nc):
    n_rows = p16.shape[0]
    row = lambda b, c: b * nc + c
    return pl.pallas_call(
        _gla_kernel,
        grid=(B, nc),
        in_specs=[pl.BlockSpec((CHUNK, 512), lambda b, c: (row(b, c), 4)),
                  pl.BlockSpec((CHUNK, 512), lambda b, c: (row(b, c), 5)),
                  pl.BlockSpec((CHUNK, LANES), lambda b, c: (row(b, c), 2048 // LANES)),
                  pl.BlockSpec((CHUNK, 512), lambda b, c: (row(b, c), 3)),
                  _const_spec((LANES, GLA_HEADS * GLA_DK)), _const_spec((1, GLA_HEADS * GLA_DK))],
        out_specs=pl.BlockSpec((CHUNK, 512), lambda b, c: (row(b, c), 0)),
        out_shape=jax.ShapeDtypeStruct((n_rows, 512), BF16),
        scratch_shapes=[pltpu.VMEM((GLA_HEADS, GLA_DV, LANES), F32)],
        name="gla",
        compiler_params=_cparams(("parallel", "arbitrary")),
    )(p16, p16, p32, p32, wts["w_gla_a2"], wts["b_gla_a2"])


def _merge_kernel(h_ref, y0_ref, y1_ref, y2_ref, y3_ref, wm_ref, bm_ref, wb_ref, wo_ref, g_ref, b_ref, wr_ref, br_ref,
                  h1_ref, h1p_ref, ri_ref, rw_ref, cnt_out_ref, cnt_ref, *, alpha):
    @pl.when(pl.program_id(0) == 0)
    def _():
        cnt_ref[...] = jnp.zeros_like(cnt_ref)

    h = h_ref[...]
    tm = h.shape[0]
    h16 = h.astype(BF16)
    ys = (y0_ref[...], y1_ref[...].astype(BF16), y2_ref[...], y3_ref[...])
    mixed = None
    for n in range(N_BRANCH):
        sl = slice(n * D_MODEL, (n + 1) * D_MODEL)
        gate = jax.nn.sigmoid(_dot(h16, wm_ref[:, sl]) + bm_ref[:, sl])
        term = gate * _dot(ys[n], wb_ref[n])
        mixed = term if mixed is None else mixed + term
    mix = _dot(mixed.astype(BF16), wo_ref[...])
    h1 = _ln_rows(alpha * h + mix, g_ref[...], b_ref[...])
    h1_ref[...] = h1
    h1p_ref[...] = _pack_bf16_pairs(h1)
    logits = _dot(h1.astype(BF16), wr_ref[...]) + br_ref[...]
    lane = lax.broadcasted_iota(I32, logits.shape, 1)
    vals, sels, idxs = [], [], []
    cur = logits
    for _ in range(TOP_K):
        mx = jnp.max(cur, -1, keepdims=True)
        idx = jnp.min(jnp.where(cur == mx, lane, LANES), -1, keepdims=True)
        sel = lane == idx
        vals.append(mx)
        sels.append(sel)
        idxs.append(idx)
        cur = jnp.where(sel, -jnp.inf, cur)
    es = [jnp.exp(v - vals[0]) for v in vals]
    den = es[0] + es[1] + es[2] + es[3]
    onehot = jnp.zeros_like(logits)
    for sel in sels:
        onehot = onehot + jnp.where(sel, 1.0, 0.0)
    rr = lax.broadcasted_iota(I32, (tm, tm), 0)
    cc = lax.broadcasted_iota(I32, (tm, tm), 1)
    lower = jnp.where(cc < rr, 1.0, 0.0).astype(BF16)
    base = cnt_ref[...] + _dot(lower, onehot.astype(BF16))
    route_i = jnp.zeros(logits.shape, I32)
    route_w = jnp.zeros_like(logits)
    for k in range(TOP_K):
        rank = jnp.sum(jnp.where(sels[k], base, 0.0), -1, keepdims=True).astype(I32)
        route_i = jnp.where(lane == k, idxs[k], route_i)
        route_i = jnp.where(lane == TOP_K + k, rank, route_i)
        route_w = jnp.where(lane == k, es[k] / den, route_w)
    ri_ref[...] = route_i
    rw_ref[...] = route_w
    cnt_ref[...] = cnt_ref[...] + jnp.sum(onehot, 0, keepdims=True)
    cnt_out_ref[...] = jnp.broadcast_to(cnt_ref[...], cnt_out_ref.shape)


def _merge(h, ys, wts, alpha, tm):
    n_rows = h.shape[0]
    rspec = lambda w: pl.BlockSpec((tm, w), lambda i: (i, 0))
    return pl.pallas_call(
        functools.partial(_merge_kernel, alpha=alpha),
        grid=(n_rows // tm,),
        in_specs=[rspec(D_MODEL), rspec(512), rspec(512), rspec(512), rspec(512),
                  _const_spec((D_MODEL, N_BRANCH * D_MODEL)), _const_spec((1, N_BRANCH * D_MODEL)),
                  _const_spec((N_BRANCH, BRANCH_W, D_MODEL)), _const_spec((D_MODEL, D_MODEL)),
                  _const_spec((1, D_MODEL)), _const_spec((1, D_MODEL)),
                  _const_spec((D_MODEL, LANES)), _const_spec((1, LANES))],
        out_specs=[rspec(D_MODEL), rspec(D_MODEL // 2), rspec(LANES), rspec(LANES),
                   pl.BlockSpec((8, LANES), lambda i: (0, 0))],
        out_shape=[jax.ShapeDtypeStruct((n_rows, D_MODEL), F32), jax.ShapeDtypeStruct((n_rows, D_MODEL // 2), U32),
                   jax.ShapeDtypeStruct((n_rows, LANES), I32), jax.ShapeDtypeStruct((n_rows, LANES), F32),
                   jax.ShapeDtypeStruct((8, LANES), F32)],
        scratch_shapes=[pltpu.VMEM((1, LANES), F32)],
        name="merge_ln1_router",
        compiler_params=_cparams(("arbitrary",)),
    )(h, *ys, wts["w_merge"], wts["b_merge"], wts["w_branch"], wts["w_out"], wts["ln1_g"], wts["ln1_b"],
      wts["w_router"], wts["b_router"])


def _route_plan(route_i, cnt, n_tiles):
    ids = route_i[:, :TOP_K]
    rank = route_i[:, TOP_K:2 * TOP_K]
    counts = cnt[0, :N_EXPERTS].astype(I32)
    padded = (counts + EXPERT_TILE - 1) // EXPERT_TILE * EXPERT_TILE
    gend = jnp.cumsum(padded)
    slot = jnp.take(gend - padded, ids) + rank
    n_used = gend[-1] // EXPERT_TILE
    tstart = jnp.arange(n_tiles, dtype=I32) * EXPERT_TILE
    te = jnp.minimum(jnp.sum(gend[None, :] <= tstart[:, None], axis=1), N_EXPERTS - 1).astype(I32)
    te = jnp.where(tstart < gend[-1], te, te[jnp.maximum(n_used - 1, 0)])
    return slot.astype(I32), te, n_used.reshape(1).astype(I32)


def _dispatch_kernel(slot_ref, x_ref, xs_in_ref, xs_ref, sem):
    del xs_in_ref
    tm = x_ref.shape[0]

    def issue(r, _):
        for k in range(TOP_K):
            pltpu.make_async_copy(x_ref.at[pl.ds(r, 1)], xs_ref.at[pl.ds(slot_ref[0, 0, TOP_K * r + k], 1)],
                                  sem).start()
        return 0

    lax.fori_loop(0, tm, issue, 0, unroll=8)
    for _ in range(TOP_K):
        pltpu.make_async_copy(x_ref, xs_ref.at[pl.ds(0, tm)], sem).wait()


def _dispatch(slot3, h1p, n_slots, tm):
    n_rows, w = h1p.shape
    return pl.pallas_call(
        _dispatch_kernel,
        grid=(n_rows // tm,),
        in_specs=[pl.BlockSpec((1, 1, TOP_K * tm), lambda i: (i, 0, 0), memory_space=pltpu.SMEM),
                  pl.BlockSpec((tm, w), lambda i: (i, 0)),
                  pl.BlockSpec(memory_space=pl.ANY)],
        out_specs=pl.BlockSpec(memory_space=pl.ANY),
        out_shape=jax.ShapeDtypeStruct((n_slots, w), U32),
        scratch_shapes=[pltpu.SemaphoreType.DMA(())],
        input_output_aliases={2: 0},
        name="moe_dispatch",
        compiler_params=_cparams(("arbitrary",)),
    )(slot3, h1p, jnp.zeros((n_slots, w), U32))


def _expert_kernel(te_ref, nu_ref, xs_ref, wg_ref, bg_ref, wu_ref, bu_ref, wd_ref, bd_ref, ys_ref, w16_ref):
    t = pl.program_id(0)

    @pl.when(t < nu_ref[0])
    def _():
        @pl.when((t == 0) | (te_ref[t] != te_ref[jnp.maximum(t - 1, 0)]))
        def _():
            w16_ref[0] = wg_ref[0].astype(BF16)
            w16_ref[1] = wu_ref[0].astype(BF16)
            w16_ref[2] = wd_ref[0].astype(BF16)

        x16 = _unpack_bf16_pairs(xs_ref[...]).astype(BF16)
        gate = jnp.minimum(_dot(x16, w16_ref[0]) + bg_ref[0], SWIGLU_LIMIT)
        up = jnp.clip(_dot(x16, w16_ref[1]) + bu_ref[0], -SWIGLU_LIMIT, SWIGLU_LIMIT)
        act = (up + 1.0) * gate * jax.nn.sigmoid(SWIGLU_ALPHA * gate)
        ys_ref[...] = _pack_bf16_pairs(_dot(act.astype(BF16), w16_ref[2]) + bd_ref[0])

    @pl.when(t >= nu_ref[0])
    def _():
        ys_ref[...] = jnp.zeros_like(ys_ref)


def _experts(te, n_used, xs, wts):
    n_slots, w = xs.shape
    n_tiles = n_slots // EXPERT_TILE
    tile = lambda t, te_ref, nu_ref: (jnp.minimum(t, nu_ref[0] - 1), 0)
    wsel = lambda t, te_ref, nu_ref: (te_ref[t], 0, 0)
    grid_spec = pltpu.PrefetchScalarGridSpec(
        num_scalar_prefetch=2,
        grid=(n_tiles,),
        in_specs=[pl.BlockSpec((EXPERT_TILE, w), tile),
                  pl.BlockSpec((1, D_MODEL, D_EXPERT), wsel), pl.BlockSpec((1, 1, D_EXPERT), wsel),
                  pl.BlockSpec((1, D_MODEL, D_EXPERT), wsel), pl.BlockSpec((1, 1, D_EXPERT), wsel),
                  pl.BlockSpec((1, D_EXPERT, D_MODEL), wsel), pl.BlockSpec((1, 1, D_MODEL), wsel)],
        out_specs=pl.BlockSpec((EXPERT_TILE, w), lambda t, te_ref, nu_ref: (t, 0)),
        scratch_shapes=[pltpu.VMEM((3, D_MODEL, D_EXPERT), BF16)])
    return pl.pallas_call(
        _expert_kernel,
        grid_spec=grid_spec,
        out_shape=jax.ShapeDtypeStruct((n_slots, w), U32),
        name="moe_experts",
        compiler_params=_cparams(("arbitrary",)),
    )(te, n_used, xs, *wts)


def _combine_kernel(slot_ref, x_ref, rw_ref, ys_ref, g_ref, b_ref, o_ref, buf_ref, sem, *, alpha, tiles_per_seq):
    tm = x_ref.shape[0]

    def issue(r, _):
        for k in range(TOP_K):
            pltpu.make_async_copy(ys_ref.at[pl.ds(slot_ref[0, 0, TOP_K * r + k], 1)], buf_ref.at[k, pl.ds(r, 1)],
                                  sem).start()
        return 0

    lax.fori_loop(0, tm, issue, 0, unroll=8)
    for k in range(TOP_K):
        pltpu.make_async_copy(ys_ref.at[pl.ds(0, tm)], buf_ref.at[k], sem).wait()
    rw = rw_ref[...]
    acc = None
    for k in range(TOP_K):
        term = rw[:, k:k + 1] * _unpack_bf16_pairs(buf_ref[k])
        acc = term if acc is None else acc + term
    out = _ln_rows(alpha * x_ref[...] + acc, g_ref[...], b_ref[...])
    row = lax.broadcasted_iota(I32, out.shape, 0)
    first_tile = lax.rem(pl.program_id(0), tiles_per_seq) == 0
    o_ref[...] = jnp.where(first_tile & (row < PAD_FRONT), 0.0, out)


def _combine(slot3, h1, route_w, ys, wts, alpha, tm, tp):
    n_rows = h1.shape[0]
    w = ys.shape[1]
    assert tp % tm == 0 and tm >= PAD_FRONT
    return pl.pallas_call(
        functools.partial(_combine_kernel, alpha=alpha, tiles_per_seq=tp // tm),
        grid=(n_rows // tm,),
        in_specs=[pl.BlockSpec((1, 1, TOP_K * tm), lambda i: (i, 0, 0), memory_space=pltpu.SMEM),
                  pl.BlockSpec((tm, D_MODEL), lambda i: (i, 0)),
                  pl.BlockSpec((tm, LANES), lambda i: (i, 0)),
                  pl.BlockSpec(memory_space=pl.ANY),
                  _const_spec((1, D_MODEL)), _const_spec((1, D_MODEL))],
        out_specs=pl.BlockSpec((tm, D_MODEL), lambda i: (i, 0)),
        out_shape=jax.ShapeDtypeStruct((n_rows, D_MODEL), F32),
        scratch_shapes=[pltpu.VMEM((TOP_K, tm, w), U32), pltpu.SemaphoreType.DMA(())],
        name="moe_combine_ln2",
        compiler_params=_cparams(("arbitrary",)),
    )(slot3, h1, route_w, ys, wts["ln2_g"], wts["ln2_b"])


def _moe(h1, h1p, route_i, route_w, cnt, wts, expert_wts, layer, alpha, tm, tp):
    n_rows = h1.shape[0]
    n_tiles = -(-TOP_K * n_rows // EXPERT_TILE) + N_EXPERTS
    slot, te, n_used = _route_plan(route_i, cnt, n_tiles)
    slot3 = slot.reshape(n_rows // tm, 1, TOP_K * tm)
    xs = _dispatch(slot3, h1p, n_tiles * EXPERT_TILE, tm)
    ys = _experts(te + layer * N_EXPERTS, n_used, xs, expert_wts)
    return _combine(slot3, h1, route_w, ys, wts, alpha, tm, tp)


def _rot_half_cols(w, d):
    n = w.shape[1] // d
    w3 = w.reshape(w.shape[0], n, d)
    return jnp.concatenate([-w3[..., d // 2:], w3[..., :d // 2]], -1).reshape(w.shape)


def _tables(tp):
    pos = jnp.maximum(jnp.arange(tp, dtype=I32) - PAD_FRONT, 0).astype(F32)

    def cs(d):
        inv = ROPE_BASE ** (-jnp.arange(0, d, 2, dtype=F32) / d)
        ang = pos[:, None] * inv[None, :]
        return jnp.cos(ang), jnp.sin(ang)

    rc, rs = cs(RET_DK)
    ret_cos = jnp.tile(jnp.concatenate([rc, rc], -1), (1, 2 * RET_HEADS))
    ret_sin = jnp.tile(jnp.concatenate([rs, rs], -1), (1, 2 * RET_HEADS))
    mc, ms = cs(MLA_ROPE)
    ones = jnp.ones((tp, MLA_NOPE), F32)
    tail1 = jnp.ones((tp, LANES - MLA_NOPE - MLA_ROPE), F32)
    mla_cos = jnp.concatenate([ones, mc, mc, tail1], -1)
    mla_sin = jnp.concatenate([0 * ones, ms, ms, 0 * tail1], -1)
    log_gamma = jnp.log1p(-jnp.exp2(-5.0 - jnp.arange(RET_HEADS, dtype=F32)))
    idx = jnp.arange(CHUNK, dtype=F32)
    rel = idx[:, None] - idx[None, :]
    dmat = jnp.where(rel >= 0, jnp.exp(log_gamma[:, None, None] * jnp.maximum(rel, 0.0)), 0.0)
    k_dec = jnp.exp(log_gamma[:, None] * (CHUNK - 1.0 - idx))
    q_dec = jnp.exp(log_gamma[:, None] * (idx + 1.0))
    rep = lambda t: jnp.repeat(t.T, RET_DK, axis=1)
    ret_dec = jnp.concatenate([rep(q_dec), rep(k_dec)], -1)
    ret_cdec = jnp.exp(log_gamma * CHUNK)
    return dict(ret_cos=ret_cos, ret_sin=ret_sin, mla_cos=mla_cos, mla_sin=mla_sin, ret_dmat=dmat, ret_dec=ret_dec,
                ret_cdec=ret_cdec)


def _layer_weights(w_in, w_merge, b_merge, g_cq, g_ckv, w_uq, w_ukv, conv_w, conv_b, w_lru_a, b_lru_a, w_lru_x,
                   b_lru_x, lru_lambda, w_gla_a2, b_gla_a2, w_branch, w_out, ln1_g, ln1_b, w_router, b_router,
                   ln2_g, ln2_b):
    D = D_MODEL
    z = lambda n: jnp.zeros((D, n), F32)
    rqk = jnp.concatenate([w_in[:, _O_RQ:_O_RK], w_in[:, _O_RK:_O_RV] * (RET_DK ** -0.5)], -1)
    kr = w_in[:, _O_KR:_O_LX]
    kr_slot = jnp.concatenate([z(MLA_NOPE), kr, z(LANES - MLA_NOPE - MLA_ROPE)], -1)
    kr_rot_slot = jnp.concatenate([z(MLA_NOPE), _rot_half_cols(kr, MLA_ROPE), z(LANES - MLA_NOPE - MLA_ROPE)], -1)
    gqk = jnp.concatenate([w_in[:, _O_GQ:_O_GK], w_in[:, _O_GK:_O_GV] * (GLA_DK ** -0.5)], -1)
    w16 = jnp.concatenate([rqk, _rot_half_cols(rqk, RET_DK), w_in[:, _O_RV:_O_RG],
                           w_in[:, _O_CQ:_O_KR], kr_slot, gqk, w_in[:, _O_GV:_O_GA], kr_rot_slot], -1).astype(BF16)
    w32 = jnp.concatenate([w_in[:, _O_RG:_O_CQ], w_in[:, _O_LX:_O_LG], w_in[:, _O_LG:_O_GQ], w_in[:, _O_GG:],
                           w_in[:, _O_GA:_O_GG], z(LANES - GLA_RANK)], -1).astype(BF16)
    hq = MLA_NOPE + MLA_ROPE
    uq3 = w_uq.reshape(MLA_Q_RANK, MLA_HEADS, hq)
    zq = jnp.zeros((MLA_Q_RANK, MLA_HEADS, LANES - hq), F32)
    uq = jnp.concatenate([uq3, zq], -1).reshape(MLA_Q_RANK, MLA_HEADS * LANES)
    uq_rope_rot = _rot_half_cols(uq3[..., MLA_NOPE:].reshape(MLA_Q_RANK, -1), MLA_ROPE)
    uq_rope_rot = uq_rope_rot.reshape(MLA_Q_RANK, MLA_HEADS, MLA_ROPE)
    uqr = jnp.concatenate([jnp.zeros((MLA_Q_RANK, MLA_HEADS, MLA_NOPE), F32), uq_rope_rot, zq], -1)
    uqr = uqr.reshape(MLA_Q_RANK, MLA_HEADS * LANES)
    eye = jnp.eye(LRU_BLOCKS, dtype=F32)
    bd = lambda w: jnp.einsum("ncd,nm->ncmd", w, eye).reshape(LRU_WIDTH, LRU_WIDTH)
    w_ab = jnp.concatenate([bd(w_lru_a), bd(w_lru_x)], -1).astype(BF16)
    wa2 = jnp.concatenate([w_gla_a2, jnp.zeros((LANES - GLA_RANK, GLA_HEADS * GLA_DK), F32)], 0).astype(BF16)
    wr = jnp.concatenate([w_router, jnp.zeros((D, LANES - N_EXPERTS), F32)], -1).astype(BF16)
    br = jnp.concatenate([b_router, jnp.full((LANES - N_EXPERTS,), -jnp.inf, F32)]).reshape(1, LANES)
    row = lambda v: v.reshape(1, -1).astype(F32)
    return dict(
        w16=w16, w32=w32,
        g_cq=row(g_cq), g_ckv=row(g_ckv), w_uq=uq.astype(BF16), w_uq_rot=uqr.astype(BF16), w_ukv=w_ukv.astype(BF16),
        conv_w=conv_w, conv_b=row(conv_b), w_lru_ab=w_ab, b_lru_ab=row(jnp.concatenate([b_lru_a, b_lru_x])),
        lru_ncsp=row(-LRU_C * jax.nn.softplus(-lru_lambda)),
        w_gla_a2=wa2, b_gla_a2=row(b_gla_a2),
        w_merge=w_merge.astype(BF16), b_merge=row(b_merge), w_branch=w_branch.astype(BF16), w_out=w_out.astype(BF16),
        ln1_g=row(ln1_g), ln1_b=row(ln1_b), w_router=wr, b_router=br,
        ln2_g=row(ln2_g), ln2_b=row(ln2_b))


def _row_tile(n_rows, want):
    best = CHUNK
    for t in range(CHUNK, want + 1, CHUNK):
        if n_rows % t == 0:
            best = t
    return best


def kernel(x, meta_tokens, ln0_g, ln0_b, w_in, w_merge, b_merge, g_cq, g_ckv, w_uq, w_ukv, conv_w, conv_b, w_lru_a, b_lru_a, w_lru_x, b_lru_x, lru_lambda, w_gla_a2, b_gla_a2, w_branch, w_out, ln1_g, ln1_b, w_router, b_router, w_exp_gate, b_exp_gate, w_exp_up, b_exp_up, w_exp_down, b_exp_down, ln2_g, ln2_b):
    B, S, D = x.shape
    depth = w_in.shape[0]
    alpha = (2.0 * depth) ** 0.25
    tp = S + CHUNK
    nc = tp // CHUNK
    n_rows = B * tp
    tabs = _tables(tp)
    per_layer = (w_in, w_merge, b_merge, g_cq, g_ckv, w_uq, w_ukv, conv_w, conv_b, w_lru_a, b_lru_a, w_lru_x, b_lru_x,
                 lru_lambda, w_gla_a2, b_gla_a2, w_branch, w_out, ln1_g, ln1_b, w_router, b_router, ln2_g, ln2_b)
    n_le = depth * N_EXPERTS
    expert_wts = (w_exp_gate.reshape(n_le, D, D_EXPERT), b_exp_gate.reshape(n_le, 1, D_EXPERT),
                  w_exp_up.reshape(n_le, D, D_EXPERT), b_exp_up.reshape(n_le, 1, D_EXPERT),
                  w_exp_down.reshape(n_le, D_EXPERT, D), b_exp_down.reshape(n_le, 1, D))
    tm = _row_tile(tp, 640)
    t_att = _row_tile(tp, 1664)
    h = _ln0(x, meta_tokens.astype(x.dtype), ln0_g, ln0_b).reshape(n_rows, D)
    for l in range(depth):
        wts = _layer_weights(*(p[l] for p in per_layer))
        p16 = _proj(h, wts["w16"], BF16, tm)
        p32 = _proj(h, wts["w32"], F32, tm)
        y_ret = _retention(p16, p32, tabs, B, tp, tm)
        qt, kh, vt = _mla_prep(p16, tabs, wts, B, tp, tm)
        y_mla = _flash(qt, kh, vt, t_att)
        y_lru = _lru(p32, wts, B, tp, tm)
        y_gla = _gla(p16, p32, wts, B, nc)
        h1, h1p, route_i, route_w, cnt = _merge(h, (y_ret, y_mla, y_lru, y_gla), wts, alpha, tm)
        h = _moe(h1, h1p, route_i, route_w, cnt, wts, expert_wts, l, alpha, tm, tp)
    return h.reshape(B, tp, D)[:, CHUNK:]
```

```python
import functools

import jax
import jax.numpy as jnp
from jax import lax
from jax.experimental import pallas as pl
from jax.experimental.pallas import tpu as pltpu

F32 = jnp.float32
BF16 = jnp.bfloat16
U32 = jnp.uint32
I32 = jnp.int32

D_MODEL = 1024
N_META = 16
CHUNK = 128
PAD_FRONT = CHUNK - N_META
ROPE_BASE = 10000.0
NEG_BIG = -1e30

RET_HEADS, RET_DK, RET_DV = 4, 64, 128
MLA_HEADS, MLA_Q_RANK, MLA_KV_RANK, MLA_NOPE, MLA_ROPE, MLA_DV = 8, 256, 128, 64, 32, 64
LRU_WIDTH, LRU_BLOCKS, CONV_W, LRU_C = 512, 8, 4, 8.0
LRU_BLOCK = LRU_WIDTH // LRU_BLOCKS
GLA_HEADS, GLA_DK, GLA_DV, GLA_RANK, GLA_TAU, GLA_CHUNK = 4, 64, 128, 16, 16.0, 16
N_BRANCH, BRANCH_W = 4, 512
N_EXPERTS, TOP_K, D_EXPERT = 32, 4, 1024
SWIGLU_LIMIT, SWIGLU_ALPHA = 7.0, 1.702

LANES = 128
VMEM_LIMIT = 56 * 1024 * 1024
EXPERT_TILE = 512

_O_RQ, _O_RK, _O_RV, _O_RG = 0, 256, 512, 1024
_O_CQ, _O_CKV, _O_KR = 1536, 1792, 1920
_O_LX, _O_LG = 1952, 2464
_O_GQ, _O_GK, _O_GV, _O_GA, _O_GG = 2976, 3232, 3488, 4000, 4016


def _cparams(sem):
    return pltpu.CompilerParams(dimension_semantics=sem, vmem_limit_bytes=VMEM_LIMIT)


def _const_spec(shape):
    nd = len(shape)
    return pl.BlockSpec(shape, lambda *_: (0,) * nd, pipeline_mode=pl.Buffered(1))


def _ln_rows(x, g, b, eps=1e-5):
    mu = jnp.mean(x, -1, keepdims=True)
    xc = x - mu
    var = jnp.mean(xc * xc, -1, keepdims=True)
    return xc * lax.rsqrt(var + eps) * g + b


def _dot(a, b):
    return jnp.dot(a, b, preferred_element_type=F32)


def _dot_nt(a, b):
    return lax.dot_general(a, b, (((1,), (1,)), ((), ())), preferred_element_type=F32)


def _pack_bf16_pairs(x):
    n = x.shape[1] // 2
    bits = lax.bitcast_convert_type(x.astype(BF16).astype(F32), U32)
    return (bits[:, :n] & U32(0xFFFF0000)) | (bits[:, n:] >> 16)


def _unpack_bf16_pairs(w):
    hi = lax.bitcast_convert_type(w & U32(0xFFFF0000), F32)
    lo = lax.bitcast_convert_type(w << 16, F32)
    return jnp.concatenate([hi, lo], axis=1)


ROW_SUB = D_MODEL // 2 // LANES


def _store_rows(ref, words):
    rows = words.shape[0]
    for j in range(ROW_SUB):
        ref[pl.ds(j, rows, stride=ROW_SUB), :] = words[:, j * LANES:(j + 1) * LANES]


def _load_rows(ref):
    rows = ref.shape[0] // ROW_SUB
    return jnp.concatenate([ref[pl.ds(j, rows, stride=ROW_SUB), :] for j in range(ROW_SUB)], axis=1)


def _ln0_kernel(x_ref, meta_ref, g_ref, b_ref, o_ref):
    i = pl.program_id(1)
    g = g_ref[...]
    b = b_ref[...]

    @pl.when(i == 0)
    def _():
        o_ref[0, :PAD_FRONT, :] = jnp.zeros((PAD_FRONT, D_MODEL), F32)
        o_ref[0, PAD_FRONT:, :] = _ln_rows(meta_ref[...], g, b)

    @pl.when(i > 0)
    def _():
        o_ref[0] = _ln_rows(x_ref[0], g, b)


def _ln0(x, meta, g, b):
    B, S, D = x.shape
    nc = S // CHUNK + 1
    return pl.pallas_call(
        _ln0_kernel,
        grid=(B, nc),
        in_specs=[pl.BlockSpec((1, CHUNK, D), lambda bb, i: (bb, jnp.maximum(i - 1, 0), 0)),
                  pl.BlockSpec((N_META, D), lambda bb, i: (0, 0)),
                  pl.BlockSpec((1, D), lambda bb, i: (0, 0)),
                  pl.BlockSpec((1, D), lambda bb, i: (0, 0))],
        out_specs=pl.BlockSpec((1, CHUNK, D), lambda bb, i: (bb, i, 0)),
        out_shape=jax.ShapeDtypeStruct((B, nc * CHUNK, D), F32),
        name="ln0",
        compiler_params=_cparams(("parallel", "arbitrary")),
    )(x, meta, g.reshape(1, D), b.reshape(1, D))


def _proj_kernel(h_ref, w_ref, o_ref, *, col_step):
    h = h_ref[...].astype(BF16)
    n = w_ref.shape[1]
    for c0 in range(0, n, col_step):
        c1 = min(c0 + col_step, n)
        o_ref[:, c0:c1] = _dot(h, w_ref[:, c0:c1]).astype(o_ref.dtype)


def _proj(h, w, out_dtype, tm):
    n_rows, d = h.shape
    n = w.shape[1]
    return pl.pallas_call(
        functools.partial(_proj_kernel, col_step=512),
        grid=(n_rows // tm,),
        in_specs=[pl.BlockSpec((tm, d), lambda i: (i, 0)), _const_spec((d, n))],
        out_specs=pl.BlockSpec((tm, n), lambda i: (i, 0)),
        out_shape=jax.ShapeDtypeStruct((n_rows, n), out_dtype),
        name="in_proj",
        compiler_params=_cparams(("parallel",)),
    )(h, w)


def _half_masks(dtype):
    lane = lax.broadcasted_iota(I32, (1, LANES), 1)
    lo = (lane < LANES // 2).astype(dtype)
    return lo, (1 - lo).astype(dtype)


def _ret_kernel(cdec_ref, qk_ref, rot_ref, v_ref, g_ref, cos_ref, sin_ref, dec_ref, dmat_ref, o_ref, s_ref):
    c = pl.program_id(1)

    @pl.when(c == 0)
    def _():
        s_ref[...] = jnp.zeros_like(s_ref)

    r_all = qk_ref[...].astype(F32) * cos_ref[...] + rot_ref[...].astype(F32) * sin_ref[...]
    masks = _half_masks(F32)
    states = [s_ref[hd] for hd in range(RET_HEADS)]
    for ch in range(qk_ref.shape[0] // CHUNK):
        rows = slice(ch * CHUNK, (ch + 1) * CHUNK)
        r = r_all[rows]
        rd = r * dec_ref[...]
        for hd in range(RET_HEADS):
            pair, half = divmod(hd, 2)
            m = masks[half]
            q_lo, k_lo = pair * LANES, 2 * LANES + pair * LANES
            qp = (r[:, q_lo:q_lo + LANES] * m).astype(BF16)
            kp = r[:, k_lo:k_lo + LANES].astype(BF16)
            vh = v_ref[rows, hd * RET_DV:(hd + 1) * RET_DV]
            scores = _dot_nt(qp, kp) * dmat_ref[hd]
            o = _dot(scores.astype(BF16), vh)
            qin = (rd[:, q_lo:q_lo + LANES] * m).astype(BF16)
            o = o + _dot(qin, states[hd].astype(BF16))
            kdec_t = (rd[:, k_lo:k_lo + LANES] * m).T.astype(BF16)
            states[hd] = states[hd] * cdec_ref[hd] + _dot(kdec_t, vh)
            mu = jnp.mean(o, -1, keepdims=True)
            oc = o - mu
            var = jnp.mean(oc * oc, -1, keepdims=True)
            on = oc * lax.rsqrt(var + 1e-5)
            gate = g_ref[rows, hd * RET_DV:(hd + 1) * RET_DV]
            o_ref[rows, hd * RET_DV:(hd + 1) * RET_DV] = (gate * jax.nn.sigmoid(gate) * on).astype(o_ref.dtype)
    for hd in range(RET_HEADS):
        s_ref[hd] = states[hd]


def _retention(p16, p32, tabs, B, tp, tr):
    n_rows = p16.shape[0]
    nc = tp // tr
    row = lambda b, c: b * nc + c
    return pl.pallas_call(
        _ret_kernel,
        grid=(B, nc),
        in_specs=[pl.BlockSpec(memory_space=pltpu.SMEM),
                  pl.BlockSpec((tr, 512), lambda b, c: (row(b, c), 0)),
                  pl.BlockSpec((tr, 512), lambda b, c: (row(b, c), 1)),
                  pl.BlockSpec((tr, 512), lambda b, c: (row(b, c), 2)),
                  pl.BlockSpec((tr, 512), lambda b, c: (row(b, c), 0)),
                  pl.BlockSpec((tr, 512), lambda b, c: (c, 0)),
                  pl.BlockSpec((tr, 512), lambda b, c: (c, 0)),
                  _const_spec((CHUNK, 512)),
                  _const_spec((RET_HEADS, CHUNK, CHUNK))],
        out_specs=pl.BlockSpec((tr, 512), lambda b, c: (row(b, c), 0)),
        out_shape=jax.ShapeDtypeStruct((n_rows, 512), BF16),
        scratch_shapes=[pltpu.VMEM((RET_HEADS, LANES, RET_DV), F32)],
        name="retention",
        compiler_params=_cparams(("parallel", "arbitrary")),
    )(tabs["ret_cdec"], p16, p16, p16, p32, tabs["ret_cos"], tabs["ret_sin"], tabs["ret_dec"], tabs["ret_dmat"])


MLA_BIAS_LANE = MLA_NOPE + MLA_ROPE
M_INIT = -3.0e38
LOG2_E = 1.4426950408889634


def _mla_prep_kernel(lat_ref, krot_ref, cos_ref, sin_ref, gq_ref, gkv_ref, wq_ref, wqr_ref, wkv_ref,
                     qt_ref, k_ref, vt_ref):
    i = pl.program_id(1)
    tr = lat_ref.shape[0]
    lat = lat_ref[...].astype(F32)
    cq = lat[:, :MLA_Q_RANK]
    ckv = lat[:, MLA_Q_RANK:MLA_Q_RANK + MLA_KV_RANK]
    kr = lat[:, MLA_Q_RANK + MLA_KV_RANK:]
    cos = cos_ref[...]
    sin = sin_ref[...]
    cqn = (cq * lax.rsqrt(jnp.mean(cq * cq, -1, keepdims=True) + 1e-6) * gq_ref[...]).astype(BF16)
    ckvn = (ckv * lax.rsqrt(jnp.mean(ckv * ckv, -1, keepdims=True) + 1e-6) * gkv_ref[...]).astype(BF16)
    krr = kr * cos + krot_ref[...].astype(F32) * sin
    scale = (MLA_NOPE + MLA_ROPE) ** -0.5 * LOG2_E
    lane = lax.broadcasted_iota(I32, (tr, LANES), 1)
    row = lax.broadcasted_iota(I32, (tr, LANES), 0)
    is_nope = lane < MLA_NOPE
    is_bias = lane == MLA_BIAS_LANE
    kbias = jnp.where(is_bias & (i == 0) & (row < PAD_FRONT), NEG_BIG, 0.0)
    for hd in range(MLA_HEADS):
        sl = slice(hd * LANES, (hd + 1) * LANES)
        q = (_dot(cqn, wq_ref[:, sl]) * cos + _dot(cqn, wqr_ref[:, sl]) * sin) * scale
        qt_ref[0, hd] = jnp.where(is_bias, 1.0, q).T.astype(qt_ref.dtype)
        kv = _dot(ckvn, wkv_ref[:, sl])
        k_ref[0, hd] = (jnp.where(is_nope, kv, krr) + kbias).astype(k_ref.dtype)
        vt_ref[0, hd] = kv.T[MLA_NOPE:, :].astype(vt_ref.dtype)


def _mla_prep(p16, tabs, wts, B, tp, tr):
    nt = tp // tr
    assert tr >= PAD_FRONT
    row = lambda b, i: b * nt + i
    hm = jax.ShapeDtypeStruct((B, MLA_HEADS, tp, LANES), BF16)
    hm_t = jax.ShapeDtypeStruct((B, MLA_HEADS, LANES, tp), BF16)
    hm_v = jax.ShapeDtypeStruct((B, MLA_HEADS, MLA_DV, tp), BF16)
    hspec = pl.BlockSpec((1, MLA_HEADS, tr, LANES), lambda b, i: (b, 0, i, 0))
    hspec_t = pl.BlockSpec((1, MLA_HEADS, LANES, tr), lambda b, i: (b, 0, 0, i))
    hspec_v = pl.BlockSpec((1, MLA_HEADS, MLA_DV, tr), lambda b, i: (b, 0, 0, i))
    return pl.pallas_call(
        _mla_prep_kernel,
        grid=(B, nt),
        in_specs=[pl.BlockSpec((tr, 512), lambda b, i: (row(b, i), 3)),
                  pl.BlockSpec((tr, LANES), lambda b, i: (row(b, i), 3072 // LANES)),
                  pl.BlockSpec((tr, LANES), lambda b, i: (i, 0)),
                  pl.BlockSpec((tr, LANES), lambda b, i: (i, 0)),
                  _const_spec((1, MLA_Q_RANK)), _const_spec((1, MLA_KV_RANK)),
                  _const_spec((MLA_Q_RANK, MLA_HEADS * LANES)), _const_spec((MLA_Q_RANK, MLA_HEADS * LANES)),
                  _const_spec((MLA_KV_RANK, MLA_HEADS * LANES))],
        out_specs=[hspec_t, hspec, hspec_v],
        out_shape=[hm_t, hm, hm_v],
        name="mla_prep",
        compiler_params=_cparams(("parallel", "arbitrary")),
    )(p16, p16, tabs["mla_cos"], tabs["mla_sin"], wts["g_cq"], wts["g_ckv"], wts["w_uq"], wts["w_uq_rot"],
      wts["w_ukv"])


def _flash_kernel(qt_ref, k_ref, vt_ref, o_ref, *, tb):
    tp = k_ref.shape[2]
    nb = tp // tb
    ng = tb // LANES
    for hh in range(2):
        def step(qts, j, stats, diag):
            k0 = pl.multiple_of(j * tb, tb)
            new = []
            for g in range(ng):
                m, l, acc = stats[g]
                nk = (g + 1) * LANES if diag else tb
                k = k_ref[0, hh, pl.ds(k0, nk), :]
                vt = vt_ref[0, hh, :, pl.ds(k0, nk)]
                vt = jnp.concatenate([vt, jnp.zeros_like(vt)], axis=0)
                s = _dot(k, qts[g])
                if diag:
                    key_i = lax.broadcasted_iota(I32, (nk, LANES), 0)
                    qry_i = lax.broadcasted_iota(I32, (nk, LANES), 1)
                    s = jnp.where(key_i <= qry_i + g * LANES, s, NEG_BIG)
                m_new = jnp.maximum(m, jnp.max(s, 0, keepdims=True))
                alpha = jnp.exp2(m - m_new)
                p = jnp.exp2(s - m_new)
                l_new = alpha * l + jnp.sum(p, 0, keepdims=True)
                acc_new = alpha * acc + _dot(vt, p.astype(BF16))
                new.append((m_new, l_new, acc_new))
            return tuple(new)

        def q_block(i, _):
            q0 = i * tb
            qts = [qt_ref[0, hh, :, pl.ds(pl.multiple_of(q0 + g * LANES, LANES), LANES)] for g in range(ng)]
            init = tuple((jnp.full((1, LANES), M_INIT, F32), jnp.zeros((1, LANES), F32),
                          jnp.zeros((LANES, LANES), F32)) for _ in range(ng))
            carry = lax.fori_loop(0, i, lambda j, c: step(qts, j, c, False), init)
            carry = step(qts, i, carry, True)
            for g in range(ng):
                m, l, acc = carry[g]
                rows = pl.ds(pl.multiple_of(q0 + g * LANES, LANES), LANES)
                o_ref[rows, hh * MLA_DV:(hh + 1) * MLA_DV] = (acc / l).T[:, :MLA_DV]
            return 0

        lax.fori_loop(0, nb, q_block, 0)


def _flash(qt, kh, vt, tb):
    B, H, tp, _ = kh.shape
    npair = H // 2
    spec = pl.BlockSpec((1, 2, tp, LANES), lambda b, j: (b, j, 0, 0))
    spec_t = pl.BlockSpec((1, 2, LANES, tp), lambda b, j: (b, j, 0, 0))
    spec_v = pl.BlockSpec((1, 2, MLA_DV, tp), lambda b, j: (b, j, 0, 0))
    return pl.pallas_call(
        functools.partial(_flash_kernel, tb=tb),
        grid=(B, npair),
        in_specs=[spec_t, spec, spec_v],
        out_specs=pl.BlockSpec((tp, LANES), lambda b, j: (b, j)),
        out_shape=jax.ShapeDtypeStruct((B * tp, npair * LANES), F32),
        name="mla_flash",
        compiler_params=_cparams(("parallel", "arbitrary")),
    )(qt, kh, vt)


def _lru_kernel(x_ref, g_ref, cw_ref, cb_ref, wab_ref, bab_ref, ncsp_ref, o_ref, hist_ref, h_ref):
    c = pl.program_id(1)
    tr = x_ref.shape[0]

    @pl.when(c == 0)
    def _():
        hist_ref[...] = jnp.zeros_like(hist_ref)
        h_ref[...] = jnp.zeros_like(h_ref)

    x = x_ref[...]
    xcat = jnp.concatenate([hist_ref[...], x], axis=0)
    hist_ref[...] = x[tr - 8:, :]
    xc = cb_ref[...] + cw_ref[CONV_W - 1:CONV_W, :] * x
    for j in range(CONV_W - 1):
        off = 8 - (CONV_W - 1) + j
        xc = xc + cw_ref[j:j + 1, :] * xcat[off:off + tr, :]
    z = _dot(xc.astype(BF16), wab_ref[...]) + bab_ref[...]
    r = jax.nn.sigmoid(z[:, :LRU_WIDTH])
    i = jax.nn.sigmoid(z[:, LRU_WIDTH:])
    log_a = ncsp_ref[...] * r
    a_all = jnp.exp(log_a)
    u_all = jnp.sqrt(1.0 - jnp.exp(2.0 * log_a)) * (i * xc)
    row = lax.broadcasted_iota(I32, (CHUNK, LRU_WIDTH), 0)
    h = h_ref[...]
    for sub in range(tr // CHUNK):
        rows = slice(sub * CHUNK, (sub + 1) * CHUNK)
        a = a_all[rows]
        u = u_all[rows]
        if sub == 0:
            u = jnp.where((c == 0) & (row < PAD_FRONT), 0.0, u)
        s = 1
        while s < CHUNK:
            keep = row >= s
            a_sh = jnp.where(keep, pltpu.roll(a, s, axis=0), 1.0)
            u_sh = jnp.where(keep, pltpu.roll(u, s, axis=0), 0.0)
            u = a * u_sh + u
            a = a * a_sh
            s *= 2
        hs = u + a * h
        h = hs[CHUNK - 1:CHUNK, :]
        o_ref[rows, :] = (hs * jax.nn.gelu(g_ref[rows, :])).astype(o_ref.dtype)
    h_ref[...] = h


def _lru(p32, wts, B, tp, tr):
    n_rows = p32.shape[0]
    nc = tp // tr
    row = lambda b, c: b * nc + c
    return pl.pallas_call(
        _lru_kernel,
        grid=(B, nc),
        in_specs=[pl.BlockSpec((tr, 512), lambda b, c: (row(b, c), 1)),
                  pl.BlockSpec((tr, 512), lambda b, c: (row(b, c), 2)),
                  _const_spec((CONV_W, LRU_WIDTH)), _const_spec((1, LRU_WIDTH)),
                  _const_spec((LRU_WIDTH, 2 * LRU_WIDTH)), _const_spec((1, 2 * LRU_WIDTH)),
                  _const_spec((1, LRU_WIDTH))],
        out_specs=pl.BlockSpec((tr, 512), lambda b, c: (row(b, c), 0)),
        out_shape=jax.ShapeDtypeStruct((n_rows, 512), BF16),
        scratch_shapes=[pltpu.VMEM((8, LRU_WIDTH), F32), pltpu.VMEM((1, LRU_WIDTH), F32)],
        name="rglru",
        compiler_params=_cparams(("parallel", "arbitrary")),
    )(p32, p32, wts["conv_w"], wts["conv_b"], wts["w_lru_ab"], wts["b_lru_ab"], wts["lru_ncsp"])


GLA_LEVELS = (64, 32, 16, 8, 4, 2, 1)


def _gla_kernel(qk_ref, v_ref, a_ref, g_ref, wa2_ref, ba2_ref, o_ref, st_ref):
    c = pl.program_id(1)
    tr = qk_ref.shape[0]
    hk = GLA_HEADS * GLA_DK

    @pl.when(c == 0)
    def _():
        st_ref[...] = jnp.zeros_like(st_ref)

    x = _dot(a_ref[...].astype(BF16), wa2_ref[...]) + ba2_ref[...]
    la = (jnp.minimum(x, 0.0) - jnp.log(1.0 + jnp.exp(-jnp.abs(x)))) * (1.0 / GLA_TAU)
    ri = lax.broadcasted_iota(I32, (tr, tr), 0)
    ci = lax.broadcasted_iota(I32, (tr, tr), 1)
    tri = jnp.where(ci <= ri, 1.0, 0.0).astype(BF16)
    la_hi = la.astype(BF16)
    rem = la - la_hi.astype(F32)
    la_mid = rem.astype(BF16)
    la_lo = (rem - la_mid.astype(F32)).astype(BF16)
    gcum = _dot(tri, la_hi) + _dot(tri, la_mid) + _dot(tri, la_lo)
    row = lax.broadcasted_iota(I32, (tr, hk), 0)
    own_end = {1: gcum}
    s = 1
    while s < GLA_LEVELS[0]:
        in_right = ((row >> (s.bit_length() - 1)) & 1) == 1
        own_end[2 * s] = jnp.where(in_right, own_end[s], pltpu.roll(own_end[s], tr - s, axis=0))
        s *= 2
    qk = qk_ref[...].astype(F32)
    q = qk[:, :hk]
    k = qk[:, hk:]
    terms = [(q.astype(BF16), k.astype(BF16), ri == ci)]
    for s in GLA_LEVELS:
        r_own = own_end[s]
        r_prev = pltpu.roll(r_own, s, axis=0)
        qs = (q * jnp.exp(jnp.minimum(gcum - r_prev, 0.0))).astype(BF16)
        ks = (k * jnp.exp(jnp.minimum(r_own - gcum, 0.0))).astype(BF16)
        sh = s.bit_length() - 1
        bi = ri >> sh
        terms.append((qs, ks, ((bi & 1) == 1) & ((ci >> sh) == bi - 1)))
    gend = gcum[tr - 1:tr, :]
    qg = (q * jnp.exp(gcum)).astype(BF16)
    kg = (k * jnp.exp(gend - gcum)).astype(BF16)
    dec = jnp.exp(gend)
    masks = _half_masks(BF16)
    for hd in range(GLA_HEADS):
        pair, half = divmod(hd, 2)
        lanes = slice(pair * LANES, (pair + 1) * LANES)
        m = masks[half]
        amat = None
        for qs, ks, keep in terms:
            term = jnp.where(keep, _dot_nt(qs[:, lanes] * m, ks[:, lanes]), 0.0)
            amat = term if amat is None else amat + term
        vh = v_ref[:, hd * GLA_DV:(hd + 1) * GLA_DV]
        st = st_ref[hd]
        o = _dot(amat.astype(BF16), vh) + _dot_nt(qg[:, lanes] * m, st.astype(BF16))
        st_ref[hd] = st * dec[:, lanes] + _dot(vh.astype(F32).T.astype(BF16), kg[:, lanes])
        on = o * lax.rsqrt(jnp.mean(o * o, -1, keepdims=True) + 1e-6)
        gate = g_ref[:, hd * GLA_DV:(hd + 1) * GLA_DV]
        o_ref[:, hd * GLA_DV:(hd + 1) * GLA_DV] = (gate * jax.nn.sigmoid(gate) * on).astype(o_ref.dtype)


def _gla(p16, p32, wts, B, nc):
    n_rows = p16.shape[0]
    row = lambda b, c: b * nc + c
    return pl.pallas_call(
        _gla_kernel,
        grid=(B, nc),
        in_specs=[pl.BlockSpec((CHUNK, 512), lambda b, c: (row(b, c), 4)),
                  pl.BlockSpec((CHUNK, 512), lambda b, c: (row(b, c), 5)),
                  pl.BlockSpec((CHUNK, LANES), lambda b, c: (row(b, c), 2048 // LANES)),
                  pl.BlockSpec((CHUNK, 512), lambda b, c: (row(b, c), 3)),
                  _const_spec((LANES, GLA_HEADS * GLA_DK)), _const_spec((1, GLA_HEADS * GLA_DK))],
        out_specs=pl.BlockSpec((CHUNK, 512), lambda b, c: (row(b, c), 0)),
        out_shape=jax.ShapeDtypeStruct((n_rows, 512), BF16),
        scratch_shapes=[pltpu.VMEM((GLA_HEADS, GLA_DV, LANES), F32)],
        name="gla",
        compiler_params=_cparams(("parallel", "arbitrary")),
    )(p16, p16, p32, p32, wts["w_gla_a2"], wts["b_gla_a2"])


def _merge_kernel(h_ref, y0_ref, y1_ref, y2_ref, y3_ref, wm_ref, bm_ref, wb_ref, wo_ref, g_ref, b_ref, wr_ref, br_ref,
                  h1_ref, h1p_ref, ri_ref, rw_ref, cnt_out_ref, cnt_ref, *, alpha):
    @pl.when(pl.program_id(0) == 0)
    def _():
        cnt_ref[...] = jnp.zeros_like(cnt_ref)

    h = h_ref[...]
    tm = h.shape[0]
    h16 = h.astype(BF16)
    ys = (y0_ref[...], y1_ref[...].astype(BF16), y2_ref[...], y3_ref[...])
    mixed = None
    for n in range(N_BRANCH):
        sl = slice(n * D_MODEL, (n + 1) * D_MODEL)
        gate = jax.nn.sigmoid(_dot(h16, wm_ref[:, sl]) + bm_ref[:, sl])
        term = gate * _dot(ys[n], wb_ref[n])
        mixed = term if mixed is None else mixed + term
    mix = _dot(mixed.astype(BF16), wo_ref[...])
    h1 = _ln_rows(alpha * h + mix, g_ref[...], b_ref[...])
    h1_ref[...] = h1
    _store_rows(h1p_ref, _pack_bf16_pairs(h1))
    logits = _dot(h1.astype(BF16), wr_ref[...]) + br_ref[...]
    lane = lax.broadcasted_iota(I32, logits.shape, 1)
    vals, sels, idxs = [], [], []
    cur = logits
    for _ in range(TOP_K):
        mx = jnp.max(cur, -1, keepdims=True)
        idx = jnp.min(jnp.where(cur == mx, lane, LANES), -1, keepdims=True)
        sel = lane == idx
        vals.append(mx)
        sels.append(sel)
        idxs.append(idx)
        cur = jnp.where(sel, -jnp.inf, cur)
    es = [jnp.exp(v - vals[0]) for v in vals]
    den = es[0] + es[1] + es[2] + es[3]
    onehot = jnp.zeros_like(logits)
    for sel in sels:
        onehot = onehot + jnp.where(sel, 1.0, 0.0)
    rr = lax.broadcasted_iota(I32, (tm, tm), 0)
    cc = lax.broadcasted_iota(I32, (tm, tm), 1)
    lower = jnp.where(cc < rr, 1.0, 0.0).astype(BF16)
    base = cnt_ref[...] + _dot(lower, onehot.astype(BF16))
    route_i = jnp.zeros(logits.shape, I32)
    route_w = jnp.zeros_like(logits)
    for k in range(TOP_K):
        rank = jnp.sum(jnp.where(sels[k], base, 0.0), -1, keepdims=True).astype(I32)
        route_i = jnp.where(lane == k, idxs[k], route_i)
        route_i = jnp.where(lane == TOP_K + k, rank, route_i)
        route_w = jnp.where(lane == k, es[k] / den, route_w)
    ri_ref[...] = route_i
    rw_ref[...] = route_w
    cnt_ref[...] = cnt_ref[...] + jnp.sum(onehot, 0, keepdims=True)
    cnt_out_ref[...] = jnp.broadcast_to(cnt_ref[...], cnt_out_ref.shape)


def _merge(h, ys, wts, alpha, tm):
    n_rows = h.shape[0]
    rspec = lambda w: pl.BlockSpec((tm, w), lambda i: (i, 0))
    return pl.pallas_call(
        functools.partial(_merge_kernel, alpha=alpha),
        grid=(n_rows // tm,),
        in_specs=[rspec(D_MODEL), rspec(512), rspec(512), rspec(512), rspec(512),
                  _const_spec((D_MODEL, N_BRANCH * D_MODEL)), _const_spec((1, N_BRANCH * D_MODEL)),
                  _const_spec((N_BRANCH, BRANCH_W, D_MODEL)), _const_spec((D_MODEL, D_MODEL)),
                  _const_spec((1, D_MODEL)), _const_spec((1, D_MODEL)),
                  _const_spec((D_MODEL, LANES)), _const_spec((1, LANES))],
        out_specs=[rspec(D_MODEL), pl.BlockSpec((tm * ROW_SUB, LANES), lambda i: (i, 0)), rspec(LANES), rspec(LANES),
                   pl.BlockSpec((8, LANES), lambda i: (0, 0))],
        out_shape=[jax.ShapeDtypeStruct((n_rows, D_MODEL), F32), jax.ShapeDtypeStruct((n_rows * ROW_SUB, LANES), U32),
                   jax.ShapeDtypeStruct((n_rows, LANES), I32), jax.ShapeDtypeStruct((n_rows, LANES), F32),
                   jax.ShapeDtypeStruct((8, LANES), F32)],
        scratch_shapes=[pltpu.VMEM((1, LANES), F32)],
        name="merge_ln1_router",
        compiler_params=_cparams(("arbitrary",)),
    )(h, *ys, wts["w_merge"], wts["b_merge"], wts["w_branch"], wts["w_out"], wts["ln1_g"], wts["ln1_b"],
      wts["w_router"], wts["b_router"])


def _route_plan(route_i, cnt, n_tiles):
    ids = route_i[:, :TOP_K]
    rank = route_i[:, TOP_K:2 * TOP_K]
    counts = cnt[0, :N_EXPERTS].astype(I32)
    padded = (counts + EXPERT_TILE - 1) // EXPERT_TILE * EXPERT_TILE
    gend = jnp.cumsum(padded)
    slot = jnp.take(gend - padded, ids) + rank
    n_used = gend[-1] // EXPERT_TILE
    tstart = jnp.arange(n_tiles, dtype=I32) * EXPERT_TILE
    te = jnp.minimum(jnp.sum(gend[None, :] <= tstart[:, None], axis=1), N_EXPERTS - 1).astype(I32)
    te = jnp.where(tstart < gend[-1], te, te[jnp.maximum(n_used - 1, 0)])
    return slot.astype(I32), te, n_used.reshape(1).astype(I32)


def _dispatch_kernel(slot_ref, x_ref, xs_in_ref, xs_ref, sem):
    del xs_in_ref
    tm = x_ref.shape[0] // ROW_SUB

    def issue(r, _):
        src = x_ref.at[pl.ds(pl.multiple_of(ROW_SUB * r, ROW_SUB), ROW_SUB)]
        for k in range(TOP_K):
            dst_row = pl.multiple_of(slot_ref[0, 0, TOP_K * r + k], ROW_SUB)
            pltpu.make_async_copy(src, xs_ref.at[pl.ds(dst_row, ROW_SUB)], sem).start()
        return 0

    lax.fori_loop(0, tm, issue, 0, unroll=8)
    for _ in range(TOP_K):
        pltpu.make_async_copy(x_ref, xs_ref.at[pl.ds(0, tm * ROW_SUB)], sem).wait()


def _dispatch(slot3, h1p, n_slots, tm):
    n_rows = h1p.shape[0] // ROW_SUB
    return pl.pallas_call(
        _dispatch_kernel,
        grid=(n_rows // tm,),
        in_specs=[pl.BlockSpec((1, 1, TOP_K * tm), lambda i: (i, 0, 0), memory_space=pltpu.SMEM),
                  pl.BlockSpec((tm * ROW_SUB, LANES), lambda i: (i, 0)),
                  pl.BlockSpec(memory_space=pl.ANY)],
        out_specs=pl.BlockSpec(memory_space=pl.ANY),
        out_shape=jax.ShapeDtypeStruct((n_slots * ROW_SUB, LANES), U32),
        scratch_shapes=[pltpu.SemaphoreType.DMA(())],
        input_output_aliases={2: 0},
        name="moe_dispatch",
        compiler_params=_cparams(("arbitrary",)),
    )(slot3, h1p, jnp.zeros((n_slots * ROW_SUB, LANES), U32))


def _expert_kernel(te_ref, nu_ref, xs_ref, wg_ref, bg_ref, wu_ref, bu_ref, wd_ref, bd_ref, ys_ref, w16_ref):
    t = pl.program_id(0)

    @pl.when(t < nu_ref[0])
    def _():
        @pl.when((t == 0) | (te_ref[t] != te_ref[jnp.maximum(t - 1, 0)]))
        def _():
            w16_ref[0] = wg_ref[0].astype(BF16)
            w16_ref[1] = wu_ref[0].astype(BF16)
            w16_ref[2] = wd_ref[0].astype(BF16)

        x16 = _unpack_bf16_pairs(_load_rows(xs_ref)).astype(BF16)
        gate = jnp.minimum(_dot(x16, w16_ref[0]) + bg_ref[0], SWIGLU_LIMIT)
        up = jnp.clip(_dot(x16, w16_ref[1]) + bu_ref[0], -SWIGLU_LIMIT, SWIGLU_LIMIT)
        act = (up + 1.0) * gate * jax.nn.sigmoid(SWIGLU_ALPHA * gate)
        _store_rows(ys_ref, _pack_bf16_pairs(_dot(act.astype(BF16), w16_ref[2]) + bd_ref[0]))

    @pl.when(t >= nu_ref[0])
    def _():
        ys_ref[...] = jnp.zeros_like(ys_ref)


def _experts(te, n_used, xs, wts):
    n_slots = xs.shape[0] // ROW_SUB
    n_tiles = n_slots // EXPERT_TILE
    blk = (EXPERT_TILE * ROW_SUB, LANES)
    tile = lambda t, te_ref, nu_ref: (jnp.minimum(t, nu_ref[0] - 1), 0)
    wsel = lambda t, te_ref, nu_ref: (te_ref[t], 0, 0)
    grid_spec = pltpu.PrefetchScalarGridSpec(
        num_scalar_prefetch=2,
        grid=(n_tiles,),
        in_specs=[pl.BlockSpec(blk, tile),
                  pl.BlockSpec((1, D_MODEL, D_EXPERT), wsel), pl.BlockSpec((1, 1, D_EXPERT), wsel),
                  pl.BlockSpec((1, D_MODEL, D_EXPERT), wsel), pl.BlockSpec((1, 1, D_EXPERT), wsel),
                  pl.BlockSpec((1, D_EXPERT, D_MODEL), wsel), pl.BlockSpec((1, 1, D_MODEL), wsel)],
        out_specs=pl.BlockSpec(blk, lambda t, te_ref, nu_ref: (t, 0)),
        scratch_shapes=[pltpu.VMEM((3, D_MODEL, D_EXPERT), BF16)])
    return pl.pallas_call(
        _expert_kernel,
        grid_spec=grid_spec,
        out_shape=jax.ShapeDtypeStruct(xs.shape, U32),
        name="moe_experts",
        compiler_params=_cparams(("arbitrary",)),
    )(te, n_used, xs, *wts)


def _combine_kernel(slot_ref, x_ref, rw_ref, ys_ref, g_ref, b_ref, o_ref, buf_ref, sem, *, alpha, tiles_per_seq):
    tm = x_ref.shape[0]

    def issue(r, _):
        dst_row = pl.multiple_of(ROW_SUB * r, ROW_SUB)
        for k in range(TOP_K):
            src_row = pl.multiple_of(slot_ref[0, 0, TOP_K * r + k], ROW_SUB)
            pltpu.make_async_copy(ys_ref.at[pl.ds(src_row, ROW_SUB)], buf_ref.at[k, pl.ds(dst_row, ROW_SUB)],
                                  sem).start()
        return 0

    lax.fori_loop(0, tm, issue, 0, unroll=8)
    for k in range(TOP_K):
        pltpu.make_async_copy(ys_ref.at[pl.ds(0, tm * ROW_SUB)], buf_ref.at[k], sem).wait()
    rw = rw_ref[...]
    acc = None
    for k in range(TOP_K):
        term = rw[:, k:k + 1] * _unpack_bf16_pairs(_load_rows(buf_ref.at[k]))
        acc = term if acc is None else acc + term
    out = _ln_rows(alpha * x_ref[...] + acc, g_ref[...], b_ref[...])
    row = lax.broadcasted_iota(I32, out.shape, 0)
    first_tile = lax.rem(pl.program_id(0), tiles_per_seq) == 0
    o_ref[...] = jnp.where(first_tile & (row < PAD_FRONT), 0.0, out)


def _combine(slot3, h1, route_w, ys, wts, alpha, tm, tp):
    n_rows = h1.shape[0]
    assert tp % tm == 0 and tm >= PAD_FRONT
    return pl.pallas_call(
        functools.partial(_combine_kernel, alpha=alpha, tiles_per_seq=tp // tm),
        grid=(n_rows // tm,),
        in_specs=[pl.BlockSpec((1, 1, TOP_K * tm), lambda i: (i, 0, 0), memory_space=pltpu.SMEM),
                  pl.BlockSpec((tm, D_MODEL), lambda i: (i, 0)),
                  pl.BlockSpec((tm, LANES), lambda i: (i, 0)),
                  pl.BlockSpec(memory_space=pl.ANY),
                  _const_spec((1, D_MODEL)), _const_spec((1, D_MODEL))],
        out_specs=pl.BlockSpec((tm, D_MODEL), lambda i: (i, 0)),
        out_shape=jax.ShapeDtypeStruct((n_rows, D_MODEL), F32),
        scratch_shapes=[pltpu.VMEM((TOP_K, tm * ROW_SUB, LANES), U32), pltpu.SemaphoreType.DMA(())],
        name="moe_combine_ln2",
        compiler_params=_cparams(("arbitrary",)),
    )(slot3, h1, route_w, ys, wts["ln2_g"], wts["ln2_b"])


def _moe(h1, h1p, route_i, route_w, cnt, wts, expert_wts, layer, alpha, tm, tp):
    n_rows = h1.shape[0]
    n_tiles = -(-TOP_K * n_rows // EXPERT_TILE) + N_EXPERTS
    slot, te, n_used = _route_plan(route_i, cnt, n_tiles)
    slot3 = (slot * ROW_SUB).reshape(n_rows // tm, 1, TOP_K * tm)
    xs = _dispatch(slot3, h1p, n_tiles * EXPERT_TILE, tm)
    ys = _experts(te + layer * N_EXPERTS, n_used, xs, expert_wts)
    return _combine(slot3, h1, route_w, ys, wts, alpha, tm, tp)


def _rot_half_cols(w, d):
    n = w.shape[1] // d
    w3 = w.reshape(w.shape[0], n, d)
    return jnp.concatenate([-w3[..., d // 2:], w3[..., :d // 2]], -1).reshape(w.shape)


def _tables(tp):
    pos = jnp.maximum(jnp.arange(tp, dtype=I32) - PAD_FRONT, 0).astype(F32)

    def cs(d):
        inv = ROPE_BASE ** (-jnp.arange(0, d, 2, dtype=F32) / d)
        ang = pos[:, None] * inv[None, :]
        return jnp.cos(ang), jnp.sin(ang)

    rc, rs = cs(RET_DK)
    ret_cos = jnp.tile(jnp.concatenate([rc, rc], -1), (1, 2 * RET_HEADS))
    ret_sin = jnp.tile(jnp.concatenate([rs, rs], -1), (1, 2 * RET_HEADS))
    mc, ms = cs(MLA_ROPE)
    ones = jnp.ones((tp, MLA_NOPE), F32)
    tail1 = jnp.ones((tp, LANES - MLA_NOPE - MLA_ROPE), F32)
    mla_cos = jnp.concatenate([ones, mc, mc, tail1], -1)
    mla_sin = jnp.concatenate([0 * ones, ms, ms, 0 * tail1], -1)
    log_gamma = jnp.log1p(-jnp.exp2(-5.0 - jnp.arange(RET_HEADS, dtype=F32)))
    idx = jnp.arange(CHUNK, dtype=F32)
    rel = idx[:, None] - idx[None, :]
    dmat = jnp.where(rel >= 0, jnp.exp(log_gamma[:, None, None] * jnp.maximum(rel, 0.0)), 0.0)
    k_dec = jnp.exp(log_gamma[:, None] * (CHUNK - 1.0 - idx))
    q_dec = jnp.exp(log_gamma[:, None] * (idx + 1.0))
    rep = lambda t: jnp.repeat(t.T, RET_DK, axis=1)
    ret_dec = jnp.concatenate([rep(q_dec), rep(k_dec)], -1)
    ret_cdec = jnp.exp(log_gamma * CHUNK)
    return dict(ret_cos=ret_cos, ret_sin=ret_sin, mla_cos=mla_cos, mla_sin=mla_sin, ret_dmat=dmat, ret_dec=ret_dec,
                ret_cdec=ret_cdec)


def _layer_weights(w_in, w_merge, b_merge, g_cq, g_ckv, w_uq, w_ukv, conv_w, conv_b, w_lru_a, b_lru_a, w_lru_x,
                   b_lru_x, lru_lambda, w_gla_a2, b_gla_a2, w_branch, w_out, ln1_g, ln1_b, w_router, b_router,
                   ln2_g, ln2_b):
    D = D_MODEL
    z = lambda n: jnp.zeros((D, n), F32)
    rqk = jnp.concatenate([w_in[:, _O_RQ:_O_RK], w_in[:, _O_RK:_O_RV] * (RET_DK ** -0.5)], -1)
    kr = w_in[:, _O_KR:_O_LX]
    kr_slot = jnp.concatenate([z(MLA_NOPE), kr, z(LANES - MLA_NOPE - MLA_ROPE)], -1)
    kr_rot_slot = jnp.concatenate([z(MLA_NOPE), _rot_half_cols(kr, MLA_ROPE), z(LANES - MLA_NOPE - MLA_ROPE)], -1)
    gqk = jnp.concatenate([w_in[:, _O_GQ:_O_GK], w_in[:, _O_GK:_O_GV] * (GLA_DK ** -0.5)], -1)
    w16 = jnp.concatenate([rqk, _rot_half_cols(rqk, RET_DK), w_in[:, _O_RV:_O_RG],
                           w_in[:, _O_CQ:_O_KR], kr_slot, gqk, w_in[:, _O_GV:_O_GA], kr_rot_slot], -1).astype(BF16)
    w32 = jnp.concatenate([w_in[:, _O_RG:_O_CQ], w_in[:, _O_LX:_O_LG], w_in[:, _O_LG:_O_GQ], w_in[:, _O_GG:],
                           w_in[:, _O_GA:_O_GG], z(LANES - GLA_RANK)], -1).astype(BF16)
    hq = MLA_NOPE + MLA_ROPE
    uq3 = w_uq.reshape(MLA_Q_RANK, MLA_HEADS, hq)
    zq = jnp.zeros((MLA_Q_RANK, MLA_HEADS, LANES - hq), F32)
    uq = jnp.concatenate([uq3, zq], -1).reshape(MLA_Q_RANK, MLA_HEADS * LANES)
    uq_rope_rot = _rot_half_cols(uq3[..., MLA_NOPE:].reshape(MLA_Q_RANK, -1), MLA_ROPE)
    uq_rope_rot = uq_rope_rot.reshape(MLA_Q_RANK, MLA_HEADS, MLA_ROPE)
    uqr = jnp.concatenate([jnp.zeros((MLA_Q_RANK, MLA_HEADS, MLA_NOPE), F32), uq_rope_rot, zq], -1)
    uqr = uqr.reshape(MLA_Q_RANK, MLA_HEADS * LANES)
    eye = jnp.eye(LRU_BLOCKS, dtype=F32)
    bd = lambda w: jnp.einsum("ncd,nm->ncmd", w, eye).reshape(LRU_WIDTH, LRU_WIDTH)
    w_ab = jnp.concatenate([bd(w_lru_a), bd(w_lru_x)], -1).astype(BF16)
    wa2 = jnp.concatenate([w_gla_a2, jnp.zeros((LANES - GLA_RANK, GLA_HEADS * GLA_DK), F32)], 0).astype(BF16)
    wr = jnp.concatenate([w_router, jnp.zeros((D, LANES - N_EXPERTS), F32)], -1).astype(BF16)
    br = jnp.concatenate([b_router, jnp.full((LANES - N_EXPERTS,), -jnp.inf, F32)]).reshape(1, LANES)
    row = lambda v: v.reshape(1, -1).astype(F32)
    return dict(
        w16=w16, w32=w32,
        g_cq=row(g_cq), g_ckv=row(g_ckv), w_uq=uq.astype(BF16), w_uq_rot=uqr.astype(BF16), w_ukv=w_ukv.astype(BF16),
        conv_w=conv_w, conv_b=row(conv_b), w_lru_ab=w_ab, b_lru_ab=row(jnp.concatenate([b_lru_a, b_lru_x])),
        lru_ncsp=row(-LRU_C * jax.nn.softplus(-lru_lambda)),
        w_gla_a2=wa2, b_gla_a2=row(b_gla_a2),
        w_merge=w_merge.astype(BF16), b_merge=row(b_merge), w_branch=w_branch.astype(BF16), w_out=w_out.astype(BF16),
        ln1_g=row(ln1_g), ln1_b=row(ln1_b), w_router=wr, b_router=br,
        ln2_g=row(ln2_g), ln2_b=row(ln2_b))


def _row_tile(n_rows, want):
    best = CHUNK
    for t in range(CHUNK, want + 1, CHUNK):
        if n_rows % t == 0:
            best = t
    return best


def kernel(x, meta_tokens, ln0_g, ln0_b, w_in, w_merge, b_merge, g_cq, g_ckv, w_uq, w_ukv, conv_w, conv_b, w_lru_a, b_lru_a, w_lru_x, b_lru_x, lru_lambda, w_gla_a2, b_gla_a2, w_branch, w_out, ln1_g, ln1_b, w_router, b_router, w_exp_gate, b_exp_gate, w_exp_up, b_exp_up, w_exp_down, b_exp_down, ln2_g, ln2_b):
    B, S, D = x.shape
    depth = w_in.shape[0]
    alpha = (2.0 * depth) ** 0.25
    tp = S + CHUNK
    nc = tp // CHUNK
    n_rows = B * tp
    tabs = _tables(tp)
    per_layer = (w_in, w_merge, b_merge, g_cq, g_ckv, w_uq, w_ukv, conv_w, conv_b, w_lru_a, b_lru_a, w_lru_x, b_lru_x,
                 lru_lambda, w_gla_a2, b_gla_a2, w_branch, w_out, ln1_g, ln1_b, w_router, b_router, ln2_g, ln2_b)
    n_le = depth * N_EXPERTS
    expert_wts = (w_exp_gate.reshape(n_le, D, D_EXPERT), b_exp_gate.reshape(n_le, 1, D_EXPERT),
                  w_exp_up.reshape(n_le, D, D_EXPERT), b_exp_up.reshape(n_le, 1, D_EXPERT),
                  w_exp_down.reshape(n_le, D_EXPERT, D), b_exp_down.reshape(n_le, 1, D))
    tm = _row_tile(tp, 640)
    t_att = _row_tile(tp, 1664)
    h = _ln0(x, meta_tokens.astype(x.dtype), ln0_g, ln0_b).reshape(n_rows, D)
    for l in range(depth):
        wts = _layer_weights(*(p[l] for p in per_layer))
        p16 = _proj(h, wts["w16"], BF16, tm)
        p32 = _proj(h, wts["w32"], F32, tm)
        y_ret = _retention(p16, p32, tabs, B, tp, tm)
        qt, kh, vt = _mla_prep(p16, tabs, wts, B, tp, tm)
        y_mla = _flash(qt, kh, vt, t_att)
        y_lru = _lru(p32, wts, B, tp, tm)
        y_gla = _gla(p16, p32, wts, B, nc)
        h1, h1p, route_i, route_w, cnt = _merge(h, (y_ret, y_mla, y_lru, y_gla), wts, alpha, tm)
        h = _moe(h1, h1p, route_i, route_w, cnt, wts, expert_wts, l, alpha, tm, tp)
    return h.reshape(B, tp, D)[:, CHUNK:]
```

```python
import functools

import jax
import jax.numpy as jnp
from jax import lax
from jax.experimental import pallas as pl
from jax.experimental.pallas import tpu as pltpu

F32 = jnp.float32
BF16 = jnp.bfloat16
U32 = jnp.uint32
I32 = jnp.int32

D_MODEL = 1024
N_META = 16
CHUNK = 128
PAD_FRONT = CHUNK - N_META
ROPE_BASE = 10000.0
NEG_BIG = -1e30

RET_HEADS, RET_DK, RET_DV = 4, 64, 128
MLA_HEADS, MLA_Q_RANK, MLA_KV_RANK, MLA_NOPE, MLA_ROPE, MLA_DV = 8, 256, 128, 64, 32, 64
LRU_WIDTH, LRU_BLOCKS, CONV_W, LRU_C = 512, 8, 4, 8.0
LRU_BLOCK = LRU_WIDTH // LRU_BLOCKS
GLA_HEADS, GLA_DK, GLA_DV, GLA_RANK, GLA_TAU, GLA_CHUNK = 4, 64, 128, 16, 16.0, 16
N_BRANCH, BRANCH_W = 4, 512
N_EXPERTS, TOP_K, D_EXPERT = 32, 4, 1024
SWIGLU_LIMIT, SWIGLU_ALPHA = 7.0, 1.702

LANES = 128
VMEM_LIMIT = 56 * 1024 * 1024
EXPERT_TILE = 512

_O_RQ, _O_RK, _O_RV, _O_RG = 0, 256, 512, 1024
_O_CQ, _O_CKV, _O_KR = 1536, 1792, 1920
_O_LX, _O_LG = 1952, 2464
_O_GQ, _O_GK, _O_GV, _O_GA, _O_GG = 2976, 3232, 3488, 4000, 4016


def _cparams(sem):
    return pltpu.CompilerParams(dimension_semantics=sem, vmem_limit_bytes=VMEM_LIMIT)


def _const_spec(shape):
    nd = len(shape)
    return pl.BlockSpec(shape, lambda *_: (0,) * nd, pipeline_mode=pl.Buffered(1))


def _ln_rows(x, g, b, eps=1e-5):
    mu = jnp.mean(x, -1, keepdims=True)
    xc = x - mu
    var = jnp.mean(xc * xc, -1, keepdims=True)
    return xc * lax.rsqrt(var + eps) * g + b


def _dot(a, b):
    return jnp.dot(a, b, preferred_element_type=F32)


def _dot_nt(a, b):
    return lax.dot_general(a, b, (((1,), (1,)), ((), ())), preferred_element_type=F32)


def _pack_bf16_pairs(x):
    n = x.shape[1] // 2
    bits = lax.bitcast_convert_type(x.astype(BF16).astype(F32), U32)
    return (bits[:, :n] & U32(0xFFFF0000)) | (bits[:, n:] >> 16)


def _unpack_bf16_pairs(w):
    hi = lax.bitcast_convert_type(w & U32(0xFFFF0000), F32)
    lo = lax.bitcast_convert_type(w << 16, F32)
    return jnp.concatenate([hi, lo], axis=1)


ROW_SUB = D_MODEL // 2 // LANES


def _store_rows(ref, words):
    rows = words.shape[0]
    for j in range(ROW_SUB):
        ref[pl.ds(j, rows, stride=ROW_SUB), :] = words[:, j * LANES:(j + 1) * LANES]


def _load_rows(ref):
    rows = ref.shape[0] // ROW_SUB
    return jnp.concatenate([ref[pl.ds(j, rows, stride=ROW_SUB), :] for j in range(ROW_SUB)], axis=1)


def _ln0_kernel(x_ref, meta_ref, g_ref, b_ref, o_ref):
    i = pl.program_id(1)
    g = g_ref[...]
    b = b_ref[...]

    @pl.when(i == 0)
    def _():
        o_ref[0, :PAD_FRONT, :] = jnp.zeros((PAD_FRONT, D_MODEL), F32)
        o_ref[0, PAD_FRONT:, :] = _ln_rows(meta_ref[...], g, b)

    @pl.when(i > 0)
    def _():
        o_ref[0] = _ln_rows(x_ref[0], g, b)


def _ln0(x, meta, g, b):
    B, S, D = x.shape
    nc = S // CHUNK + 1
    return pl.pallas_call(
        _ln0_kernel,
        grid=(B, nc),
        in_specs=[pl.BlockSpec((1, CHUNK, D), lambda bb, i: (bb, jnp.maximum(i - 1, 0), 0)),
                  pl.BlockSpec((N_META, D), lambda bb, i: (0, 0)),
                  pl.BlockSpec((1, D), lambda bb, i: (0, 0)),
                  pl.BlockSpec((1, D), lambda bb, i: (0, 0))],
        out_specs=pl.BlockSpec((1, CHUNK, D), lambda bb, i: (bb, i, 0)),
        out_shape=jax.ShapeDtypeStruct((B, nc * CHUNK, D), F32),
        name="ln0",
        compiler_params=_cparams(("parallel", "arbitrary")),
    )(x, meta, g.reshape(1, D), b.reshape(1, D))


def _proj_kernel(h_ref, w_ref, o_ref, *, col_step):
    h = h_ref[...].astype(BF16)
    n = w_ref.shape[1]
    for c0 in range(0, n, col_step):
        c1 = min(c0 + col_step, n)
        o_ref[:, c0:c1] = _dot(h, w_ref[:, c0:c1]).astype(o_ref.dtype)


def _proj(h, w, out_dtype, tm):
    n_rows, d = h.shape
    n = w.shape[1]
    return pl.pallas_call(
        functools.partial(_proj_kernel, col_step=512),
        grid=(n_rows // tm,),
        in_specs=[pl.BlockSpec((tm, d), lambda i: (i, 0)), _const_spec((d, n))],
        out_specs=pl.BlockSpec((tm, n), lambda i: (i, 0)),
        out_shape=jax.ShapeDtypeStruct((n_rows, n), out_dtype),
        name="in_proj",
        compiler_params=_cparams(("parallel",)),
    )(h, w)


def _half_masks(dtype):
    lane = lax.broadcasted_iota(I32, (1, LANES), 1)
    lo = (lane < LANES // 2).astype(dtype)
    return lo, (1 - lo).astype(dtype)


def _ret_kernel(cdec_ref, qk_ref, rot_ref, v_ref, g_ref, cos_ref, sin_ref, dec_ref, dmat_ref, o_ref, s_ref):
    c = pl.program_id(1)

    @pl.when(c == 0)
    def _():
        s_ref[...] = jnp.zeros_like(s_ref)

    r_all = qk_ref[...].astype(F32) * cos_ref[...] + rot_ref[...].astype(F32) * sin_ref[...]
    masks = _half_masks(F32)
    states = [s_ref[hd] for hd in range(RET_HEADS)]
    for ch in range(qk_ref.shape[0] // CHUNK):
        rows = slice(ch * CHUNK, (ch + 1) * CHUNK)
        r = r_all[rows]
        rd = r * dec_ref[...]
        for hd in range(RET_HEADS):
            pair, half = divmod(hd, 2)
            m = masks[half]
            q_lo, k_lo = pair * LANES, 2 * LANES + pair * LANES
            qp = (r[:, q_lo:q_lo + LANES] * m).astype(BF16)
            kp = r[:, k_lo:k_lo + LANES].astype(BF16)
            vh = v_ref[rows, hd * RET_DV:(hd + 1) * RET_DV]
            scores = _dot_nt(qp, kp) * dmat_ref[hd]
            o = _dot(scores.astype(BF16), vh)
            qin = (rd[:, q_lo:q_lo + LANES] * m).astype(BF16)
            o = o + _dot(qin, states[hd].astype(BF16))
            kdec_t = (rd[:, k_lo:k_lo + LANES] * m).T.astype(BF16)
            states[hd] = states[hd] * cdec_ref[hd] + _dot(kdec_t, vh)
            mu = jnp.mean(o, -1, keepdims=True)
            oc = o - mu
            var = jnp.mean(oc * oc, -1, keepdims=True)
            on = oc * lax.rsqrt(var + 1e-5)
            gate = g_ref[rows, hd * RET_DV:(hd + 1) * RET_DV]
            o_ref[rows, hd * RET_DV:(hd + 1) * RET_DV] = (gate * jax.nn.sigmoid(gate) * on).astype(o_ref.dtype)
    for hd in range(RET_HEADS):
        s_ref[hd] = states[hd]


def _retention(p16, p32, tabs, B, tp, tr):
    n_rows = p16.shape[0]
    nc = tp // tr
    row = lambda b, c: b * nc + c
    return pl.pallas_call(
        _ret_kernel,
        grid=(B, nc),
        in_specs=[pl.BlockSpec(memory_space=pltpu.SMEM),
                  pl.BlockSpec((tr, 512), lambda b, c: (row(b, c), 0)),
                  pl.BlockSpec((tr, 512), lambda b, c: (row(b, c), 1)),
                  pl.BlockSpec((tr, 512), lambda b, c: (row(b, c), 2)),
                  pl.BlockSpec((tr, 512), lambda b, c: (row(b, c), 0)),
                  pl.BlockSpec((tr, 512), lambda b, c: (c, 0)),
                  pl.BlockSpec((tr, 512), lambda b, c: (c, 0)),
                  _const_spec((CHUNK, 512)),
                  _const_spec((RET_HEADS, CHUNK, CHUNK))],
        out_specs=pl.BlockSpec((tr, 512), lambda b, c: (row(b, c), 0)),
        out_shape=jax.ShapeDtypeStruct((n_rows, 512), BF16),
        scratch_shapes=[pltpu.VMEM((RET_HEADS, LANES, RET_DV), F32)],
        name="retention",
        compiler_params=_cparams(("parallel", "arbitrary")),
    )(tabs["ret_cdec"], p16, p16, p16, p32, tabs["ret_cos"], tabs["ret_sin"], tabs["ret_dec"], tabs["ret_dmat"])


MLA_BIAS_LANE = MLA_NOPE + MLA_ROPE
M_INIT = -3.0e38
LOG2_E = 1.4426950408889634


def _mla_prep_kernel(lat_ref, krot_ref, cos_ref, sin_ref, gq_ref, gkv_ref, wq_ref, wqr_ref, wkv_ref,
                     qt_ref, k_ref, vt_ref):
    i = pl.program_id(1)
    tr = lat_ref.shape[0]
    lat = lat_ref[...].astype(F32)
    cq = lat[:, :MLA_Q_RANK]
    ckv = lat[:, MLA_Q_RANK:MLA_Q_RANK + MLA_KV_RANK]
    kr = lat[:, MLA_Q_RANK + MLA_KV_RANK:]
    cos = cos_ref[...]
    sin = sin_ref[...]
    cqn = (cq * lax.rsqrt(jnp.mean(cq * cq, -1, keepdims=True) + 1e-6) * gq_ref[...]).astype(BF16)
    ckvn = (ckv * lax.rsqrt(jnp.mean(ckv * ckv, -1, keepdims=True) + 1e-6) * gkv_ref[...]).astype(BF16)
    krr = kr * cos + krot_ref[...].astype(F32) * sin
    scale = (MLA_NOPE + MLA_ROPE) ** -0.5 * LOG2_E
    lane = lax.broadcasted_iota(I32, (tr, LANES), 1)
    row = lax.broadcasted_iota(I32, (tr, LANES), 0)
    is_nope = lane < MLA_NOPE
    is_bias = lane == MLA_BIAS_LANE
    kbias = jnp.where(is_bias & (i == 0) & (row < PAD_FRONT), NEG_BIG, 0.0)
    for hd in range(MLA_HEADS):
        sl = slice(hd * LANES, (hd + 1) * LANES)
        q = (_dot(cqn, wq_ref[:, sl]) * cos + _dot(cqn, wqr_ref[:, sl]) * sin) * scale
        qt_ref[0, hd] = jnp.where(is_bias, 1.0, q).T.astype(qt_ref.dtype)
        kv = _dot(ckvn, wkv_ref[:, sl])
        k_ref[0, hd] = (jnp.where(is_nope, kv, krr) + kbias).astype(k_ref.dtype)
        vt_ref[0, hd] = kv.T[MLA_NOPE:, :].astype(vt_ref.dtype)


def _mla_prep(p16, tabs, wts, B, tp, tr):
    nt = tp // tr
    assert tr >= PAD_FRONT
    row = lambda b, i: b * nt + i
    hm = jax.ShapeDtypeStruct((B, MLA_HEADS, tp, LANES), BF16)
    hm_t = jax.ShapeDtypeStruct((B, MLA_HEADS, LANES, tp), BF16)
    hm_v = jax.ShapeDtypeStruct((B, MLA_HEADS, MLA_DV, tp), BF16)
    hspec = pl.BlockSpec((1, MLA_HEADS, tr, LANES), lambda b, i: (b, 0, i, 0))
    hspec_t = pl.BlockSpec((1, MLA_HEADS, LANES, tr), lambda b, i: (b, 0, 0, i))
    hspec_v = pl.BlockSpec((1, MLA_HEADS, MLA_DV, tr), lambda b, i: (b, 0, 0, i))
    return pl.pallas_call(
        _mla_prep_kernel,
        grid=(B, nt),
        in_specs=[pl.BlockSpec((tr, 512), lambda b, i: (row(b, i), 3)),
                  pl.BlockSpec((tr, LANES), lambda b, i: (row(b, i), 3072 // LANES)),
                  pl.BlockSpec((tr, LANES), lambda b, i: (i, 0)),
                  pl.BlockSpec((tr, LANES), lambda b, i: (i, 0)),
                  _const_spec((1, MLA_Q_RANK)), _const_spec((1, MLA_KV_RANK)),
                  _const_spec((MLA_Q_RANK, MLA_HEADS * LANES)), _const_spec((MLA_Q_RANK, MLA_HEADS * LANES)),
                  _const_spec((MLA_KV_RANK, MLA_HEADS * LANES))],
        out_specs=[hspec_t, hspec, hspec_v],
        out_shape=[hm_t, hm, hm_v],
        name="mla_prep",
        compiler_params=_cparams(("parallel", "arbitrary")),
    )(p16, p16, tabs["mla_cos"], tabs["mla_sin"], wts["g_cq"], wts["g_ckv"], wts["w_uq"], wts["w_uq_rot"],
      wts["w_ukv"])


def _flash_kernel(qt_ref, k_ref, vt_ref, o_ref, *, tb):
    tp = k_ref.shape[2]
    nb = tp // tb
    ng = tb // LANES
    for hh in range(2):
        def step(qts, j, stats, diag):
            k0 = pl.multiple_of(j * tb, tb)
            new = []
            for g in range(ng):
                m, l, acc = stats[g]
                nk = (g + 1) * LANES if diag else tb
                k = k_ref[0, hh, pl.ds(k0, nk), :]
                vt = vt_ref[0, hh, :, pl.ds(k0, nk)]
                vt = jnp.concatenate([vt, jnp.zeros_like(vt)], axis=0)
                s = _dot(k, qts[g])
                if diag:
                    key_i = lax.broadcasted_iota(I32, (nk, LANES), 0)
                    qry_i = lax.broadcasted_iota(I32, (nk, LANES), 1)
                    s = jnp.where(key_i <= qry_i + g * LANES, s, NEG_BIG)
                m_new = jnp.maximum(m, jnp.max(s, 0, keepdims=True))
                alpha = jnp.exp2(m - m_new)
                p = jnp.exp2(s - m_new)
                l_new = alpha * l + jnp.sum(p, 0, keepdims=True)
                acc_new = alpha * acc + _dot(vt, p.astype(BF16))
                new.append((m_new, l_new, acc_new))
            return tuple(new)

        def q_block(i, _):
            q0 = i * tb
            qts = [qt_ref[0, hh, :, pl.ds(pl.multiple_of(q0 + g * LANES, LANES), LANES)] for g in range(ng)]
            init = tuple((jnp.full((1, LANES), M_INIT, F32), jnp.zeros((1, LANES), F32),
                          jnp.zeros((LANES, LANES), F32)) for _ in range(ng))
            carry = lax.fori_loop(0, i, lambda j, c: step(qts, j, c, False), init)
            carry = step(qts, i, carry, True)
            for g in range(ng):
                m, l, acc = carry[g]
                rows = pl.ds(pl.multiple_of(q0 + g * LANES, LANES), LANES)
                o_ref[rows, hh * MLA_DV:(hh + 1) * MLA_DV] = (acc / l).T[:, :MLA_DV]
            return 0

        lax.fori_loop(0, nb, q_block, 0)


def _flash(qt, kh, vt, tb):
    B, H, tp, _ = kh.shape
    npair = H // 2
    spec = pl.BlockSpec((1, 2, tp, LANES), lambda b, j: (b, j, 0, 0))
    spec_t = pl.BlockSpec((1, 2, LANES, tp), lambda b, j: (b, j, 0, 0))
    spec_v = pl.BlockSpec((1, 2, MLA_DV, tp), lambda b, j: (b, j, 0, 0))
    return pl.pallas_call(
        functools.partial(_flash_kernel, tb=tb),
        grid=(B, npair),
        in_specs=[spec_t, spec, spec_v],
        out_specs=pl.BlockSpec((tp, LANES), lambda b, j: (b, j)),
        out_shape=jax.ShapeDtypeStruct((B * tp, npair * LANES), F32),
        name="mla_flash",
        compiler_params=_cparams(("parallel", "arbitrary")),
    )(qt, kh, vt)


def _lru_kernel(x_ref, g_ref, cw_ref, cb_ref, wab_ref, bab_ref, ncsp_ref, o_ref, hist_ref, h_ref):
    c = pl.program_id(1)
    tr = x_ref.shape[0]

    @pl.when(c == 0)
    def _():
        hist_ref[...] = jnp.zeros_like(hist_ref)
        h_ref[...] = jnp.zeros_like(h_ref)

    x = x_ref[...]
    xcat = jnp.concatenate([hist_ref[...], x], axis=0)
    hist_ref[...] = x[tr - 8:, :]
    xc = cb_ref[...] + cw_ref[CONV_W - 1:CONV_W, :] * x
    for j in range(CONV_W - 1):
        off = 8 - (CONV_W - 1) + j
        xc = xc + cw_ref[j:j + 1, :] * xcat[off:off + tr, :]
    z = _dot(xc.astype(BF16), wab_ref[...]) + bab_ref[...]
    r = jax.nn.sigmoid(z[:, :LRU_WIDTH])
    i = jax.nn.sigmoid(z[:, LRU_WIDTH:])
    log_a = ncsp_ref[...] * r
    a_all = jnp.exp(log_a)
    u_all = jnp.sqrt(1.0 - jnp.exp(2.0 * log_a)) * (i * xc)
    row = lax.broadcasted_iota(I32, (CHUNK, LRU_WIDTH), 0)
    h = h_ref[...]
    for sub in range(tr // CHUNK):
        rows = slice(sub * CHUNK, (sub + 1) * CHUNK)
        a = a_all[rows]
        u = u_all[rows]
        if sub == 0:
            u = jnp.where((c == 0) & (row < PAD_FRONT), 0.0, u)
        s = 1
        while s < CHUNK:
            keep = row >= s
            a_sh = jnp.where(keep, pltpu.roll(a, s, axis=0), 1.0)
            u_sh = jnp.where(keep, pltpu.roll(u, s, axis=0), 0.0)
            u = a * u_sh + u
            a = a * a_sh
            s *= 2
        hs = u + a * h
        h = hs[CHUNK - 1:CHUNK, :]
        o_ref[rows, :] = (hs * jax.nn.gelu(g_ref[rows, :])).astype(o_ref.dtype)
    h_ref[...] = h


def _lru(p32, wts, B, tp, tr):
    n_rows = p32.shape[0]
    nc = tp // tr
    row = lambda b, c: b * nc + c
    return pl.pallas_call(
        _lru_kernel,
        grid=(B, nc),
        in_specs=[pl.BlockSpec((tr, 512), lambda b, c: (row(b, c), 1)),
                  pl.BlockSpec((tr, 512), lambda b, c: (row(b, c), 2)),
                  _const_spec((CONV_W, LRU_WIDTH)), _const_spec((1, LRU_WIDTH)),
                  _const_spec((LRU_WIDTH, 2 * LRU_WIDTH)), _const_spec((1, 2 * LRU_WIDTH)),
                  _const_spec((1, LRU_WIDTH))],
        out_specs=pl.BlockSpec((tr, 512), lambda b, c: (row(b, c), 0)),
        out_shape=jax.ShapeDtypeStruct((n_rows, 512), BF16),
        scratch_shapes=[pltpu.VMEM((8, LRU_WIDTH), F32), pltpu.VMEM((1, LRU_WIDTH), F32)],
        name="rglru",
        compiler_params=_cparams(("parallel", "arbitrary")),
    )(p32, p32, wts["conv_w"], wts["conv_b"], wts["w_lru_ab"], wts["b_lru_ab"], wts["lru_ncsp"])


GLA_LEVELS = (64, 32, 16, 8, 4, 2, 1)


def _gla_kernel(qk_ref, v_ref, a_ref, g_ref, wa2_ref, ba2_ref, o_ref, st_ref):
    c = pl.program_id(1)
    tr = qk_ref.shape[0]
    hk = GLA_HEADS * GLA_DK

    @pl.when(c == 0)
    def _():
        st_ref[...] = jnp.zeros_like(st_ref)

    x = _dot(a_ref[...].astype(BF16), wa2_ref[...]) + ba2_ref[...]
    la = (jnp.minimum(x, 0.0) - jnp.log(1.0 + jnp.exp(-jnp.abs(x)))) * (1.0 / GLA_TAU)
    ri = lax.broadcasted_iota(I32, (tr, tr), 0)
    ci = lax.broadcasted_iota(I32, (tr, tr), 1)
    tri = jnp.where(ci <= ri, 1.0, 0.0).astype(BF16)
    la_hi = la.astype(BF16)
    rem = la - la_hi.astype(F32)
    la_mid = rem.astype(BF16)
    la_lo = (rem - la_mid.astype(F32)).astype(BF16)
    gcum = _dot(tri, la_hi) + _dot(tri, la_mid) + _dot(tri, la_lo)
    row = lax.broadcasted_iota(I32, (tr, hk), 0)
    own_end = {1: gcum}
    s = 1
    while s < GLA_LEVELS[0]:
        in_right = ((row >> (s.bit_length() - 1)) & 1) == 1
        own_end[2 * s] = jnp.where(in_right, own_end[s], pltpu.roll(own_end[s], tr - s, axis=0))
        s *= 2
    qk = qk_ref[...].astype(F32)
    q = qk[:, :hk]
    k = qk[:, hk:]
    terms = [(q.astype(BF16), k.astype(BF16), ri == ci)]
    for s in GLA_LEVELS:
        r_own = own_end[s]
        r_prev = pltpu.roll(r_own, s, axis=0)
        qs = (q * jnp.exp(jnp.minimum(gcum - r_prev, 0.0))).astype(BF16)
        ks = (k * jnp.exp(jnp.minimum(r_own - gcum, 0.0))).astype(BF16)
        sh = s.bit_length() - 1
        bi = ri >> sh
        terms.append((qs, ks, ((bi & 1) == 1) & ((ci >> sh) == bi - 1)))
    gend = gcum[tr - 1:tr, :]
    qg = (q * jnp.exp(gcum)).astype(BF16)
    kg = (k * jnp.exp(gend - gcum)).astype(BF16)
    dec = jnp.exp(gend)
    masks = _half_masks(BF16)
    for hd in range(GLA_HEADS):
        pair, half = divmod(hd, 2)
        lanes = slice(pair * LANES, (pair + 1) * LANES)
        m = masks[half]
        amat = None
        for qs, ks, keep in terms:
            term = jnp.where(keep, _dot_nt(qs[:, lanes] * m, ks[:, lanes]), 0.0)
            amat = term if amat is None else amat + term
        vh = v_ref[:, hd * GLA_DV:(hd + 1) * GLA_DV]
        st = st_ref[hd]
        o = _dot(amat.astype(BF16), vh) + _dot_nt(qg[:, lanes] * m, st.astype(BF16))
        st_ref[hd] = st * dec[:, lanes] + _dot(vh.astype(F32).T.astype(BF16), kg[:, lanes])
        on = o * lax.rsqrt(jnp.mean(o * o, -1, keepdims=True) + 1e-6)
        gate = g_ref[:, hd * GLA_DV:(hd + 1) * GLA_DV]
        o_ref[:, hd * GLA_DV:(hd + 1) * GLA_DV] = (gate * jax.nn.sigmoid(gate) * on).astype(o_ref.dtype)


def _gla(p16, p32, wts, B, nc):
    n_rows = p16.shape[0]
    row = lambda b, c: b * nc + c
    return pl.pallas_call(
        _gla_kernel,
        grid=(B, nc),
        in_specs=[pl.BlockSpec((CHUNK, 512), lambda b, c: (row(b, c), 4)),
                  pl.BlockSpec((CHUNK, 512), lambda b, c: (row(b, c), 5)),
                  pl.BlockSpec((CHUNK, LANES), lambda b, c: (row(b, c), 2048 // LANES)),
                  pl.BlockSpec((CHUNK, 512), lambda b, c: (row(b, c), 3)),
                  _const_spec((LANES, GLA_HEADS * GLA_DK)), _const_spec((1, GLA_HEADS * GLA_DK))],
        out_specs=pl.BlockSpec((CHUNK, 512), lambda b, c: (row(b, c), 0)),
        out_shape=jax.ShapeDtypeStruct((n_rows, 512), BF16),
        scratch_shapes=[pltpu.VMEM((GLA_HEADS, GLA_DV, LANES), F32)],
        name="gla",
        compiler_params=_cparams(("parallel", "arbitrary")),
    )(p16, p16, p32, p32, wts["w_gla_a2"], wts["b_gla_a2"])


def _merge_kernel(h_ref, y0_ref, y1_ref, y2_ref, y3_ref, wm_ref, bm_ref, wb_ref, wo_ref, g_ref, b_ref, wr_ref, br_ref,
                  h1_ref, h1p_ref, ri_ref, rw_ref, cnt_out_ref, cnt_ref, *, alpha):
    @pl.when(pl.program_id(0) == 0)
    def _():
        cnt_ref[...] = jnp.zeros_like(cnt_ref)

    h = h_ref[...]
    tm = h.shape[0]
    h16 = h.astype(BF16)
    ys = (y0_ref[...], y1_ref[...].astype(BF16), y2_ref[...], y3_ref[...])
    mixed = None
    for n in range(N_BRANCH):
        sl = slice(n * D_MODEL, (n + 1) * D_MODEL)
        gate = jax.nn.sigmoid(_dot(h16, wm_ref[:, sl]) + bm_ref[:, sl])
        term = gate * _dot(ys[n], wb_ref[n])
        mixed = term if mixed is None else mixed + term
    mix = _dot(mixed.astype(BF16), wo_ref[...])
    h1 = _ln_rows(alpha * h + mix, g_ref[...], b_ref[...])
    h1_ref[...] = h1
    _store_rows(h1p_ref, _pack_bf16_pairs(h1))
    logits = _dot(h1.astype(BF16), wr_ref[...]) + br_ref[...]
    lane = lax.broadcasted_iota(I32, logits.shape, 1)
    vals, sels, idxs = [], [], []
    cur = logits
    for _ in range(TOP_K):
        mx = jnp.max(cur, -1, keepdims=True)
        idx = jnp.min(jnp.where(cur == mx, lane, LANES), -1, keepdims=True)
        sel = lane == idx
        vals.append(mx)
        sels.append(sel)
        idxs.append(idx)
        cur = jnp.where(sel, -jnp.inf, cur)
    es = [jnp.exp(v - vals[0]) for v in vals]
    den = es[0] + es[1] + es[2] + es[3]
    onehot = jnp.zeros_like(logits)
    for sel in sels:
        onehot = onehot + jnp.where(sel, 1.0, 0.0)
    rr = lax.broadcasted_iota(I32, (tm, tm), 0)
    cc = lax.broadcasted_iota(I32, (tm, tm), 1)
    lower = jnp.where(cc < rr, 1.0, 0.0).astype(BF16)
    base = cnt_ref[...] + _dot(lower, onehot.astype(BF16))
    route_i = jnp.zeros(logits.shape, I32)
    route_w = jnp.zeros_like(logits)
    for k in range(TOP_K):
        rank = jnp.sum(jnp.where(sels[k], base, 0.0), -1, keepdims=True).astype(I32)
        route_i = jnp.where(lane == k, idxs[k], route_i)
        route_i = jnp.where(lane == TOP_K + k, rank, route_i)
        route_w = jnp.where(lane == k, es[k] / den, route_w)
    ri_ref[...] = route_i
    rw_ref[...] = route_w
    cnt_ref[...] = cnt_ref[...] + jnp.sum(onehot, 0, keepdims=True)
    cnt_out_ref[...] = jnp.broadcast_to(cnt_ref[...], cnt_out_ref.shape)


def _merge(h, ys, wts, alpha, tm):
    n_rows = h.shape[0]
    rspec = lambda w: pl.BlockSpec((tm, w), lambda i: (i, 0))
    return pl.pallas_call(
        functools.partial(_merge_kernel, alpha=alpha),
        grid=(n_rows // tm,),
        in_specs=[rspec(D_MODEL), rspec(512), rspec(512), rspec(512), rspec(512),
                  _const_spec((D_MODEL, N_BRANCH * D_MODEL)), _const_spec((1, N_BRANCH * D_MODEL)),
                  _const_spec((N_BRANCH, BRANCH_W, D_MODEL)), _const_spec((D_MODEL, D_MODEL)),
                  _const_spec((1, D_MODEL)), _const_spec((1, D_MODEL)),
                  _const_spec((D_MODEL, LANES)), _const_spec((1, LANES))],
        out_specs=[rspec(D_MODEL), pl.BlockSpec((tm * ROW_SUB, LANES), lambda i: (i, 0)), rspec(LANES), rspec(LANES),
                   pl.BlockSpec((8, LANES), lambda i: (0, 0))],
        out_shape=[jax.ShapeDtypeStruct((n_rows, D_MODEL), F32), jax.ShapeDtypeStruct((n_rows * ROW_SUB, LANES), U32),
                   jax.ShapeDtypeStruct((n_rows, LANES), I32), jax.ShapeDtypeStruct((n_rows, LANES), F32),
                   jax.ShapeDtypeStruct((8, LANES), F32)],
        scratch_shapes=[pltpu.VMEM((1, LANES), F32)],
        name="merge_ln1_router",
        compiler_params=_cparams(("arbitrary",)),
    )(h, *ys, wts["w_merge"], wts["b_merge"], wts["w_branch"], wts["w_out"], wts["ln1_g"], wts["ln1_b"],
      wts["w_router"], wts["b_router"])


def _route_plan(route_i, cnt, n_tiles):
    ids = route_i[:, :TOP_K]
    rank = route_i[:, TOP_K:2 * TOP_K]
    counts = cnt[0, :N_EXPERTS].astype(I32)
    padded = (counts + EXPERT_TILE - 1) // EXPERT_TILE * EXPERT_TILE
    gend = jnp.cumsum(padded)
    slot = jnp.take(gend - padded, ids) + rank
    n_used = gend[-1] // EXPERT_TILE
    tstart = jnp.arange(n_tiles, dtype=I32) * EXPERT_TILE
    te = jnp.minimum(jnp.sum(gend[None, :] <= tstart[:, None], axis=1), N_EXPERTS - 1).astype(I32)
    te = jnp.where(tstart < gend[-1], te, te[jnp.maximum(n_used - 1, 0)])
    return slot.astype(I32), te, n_used.reshape(1).astype(I32)


def _dispatch_kernel(slot_ref, x_ref, xs_in_ref, xs_ref, sem):
    del xs_in_ref
    tm = x_ref.shape[0] // ROW_SUB

    def issue(r, _):
        src = x_ref.at[pl.ds(pl.multiple_of(ROW_SUB * r, ROW_SUB), ROW_SUB)]
        for k in range(TOP_K):
            dst_row = pl.multiple_of(slot_ref[0, 0, TOP_K * r + k], ROW_SUB)
            pltpu.make_async_copy(src, xs_ref.at[pl.ds(dst_row, ROW_SUB)], sem).start(priority=k % 2)
        return 0

    lax.fori_loop(0, tm, issue, 0, unroll=8)
    for _ in range(TOP_K):
        pltpu.make_async_copy(x_ref, xs_ref.at[pl.ds(0, tm * ROW_SUB)], sem).wait()


def _dispatch(slot3, h1p, n_slots, tm):
    n_rows = h1p.shape[0] // ROW_SUB
    return pl.pallas_call(
        _dispatch_kernel,
        grid=(n_rows // tm,),
        in_specs=[pl.BlockSpec((1, 1, TOP_K * tm), lambda i: (i, 0, 0), memory_space=pltpu.SMEM),
                  pl.BlockSpec((tm * ROW_SUB, LANES), lambda i: (i, 0)),
                  pl.BlockSpec(memory_space=pl.ANY)],
        out_specs=pl.BlockSpec(memory_space=pl.ANY),
        out_shape=jax.ShapeDtypeStruct((n_slots * ROW_SUB, LANES), U32),
        scratch_shapes=[pltpu.SemaphoreType.DMA(())],
        input_output_aliases={2: 0},
        name="moe_dispatch",
        compiler_params=_cparams(("arbitrary",)),
    )(slot3, h1p, jnp.zeros((n_slots * ROW_SUB, LANES), U32))


def _expert_kernel(te_ref, nu_ref, xs_ref, wg_ref, bg_ref, wu_ref, bu_ref, wd_ref, bd_ref, ys_ref, w16_ref):
    t = pl.program_id(0)

    @pl.when(t < nu_ref[0])
    def _():
        @pl.when((t == 0) | (te_ref[t] != te_ref[jnp.maximum(t - 1, 0)]))
        def _():
            w16_ref[0] = wg_ref[0].astype(BF16)
            w16_ref[1] = wu_ref[0].astype(BF16)
            w16_ref[2] = wd_ref[0].astype(BF16)

        x16 = _unpack_bf16_pairs(_load_rows(xs_ref)).astype(BF16)
        gate = jnp.minimum(_dot(x16, w16_ref[0]) + bg_ref[0], SWIGLU_LIMIT)
        up = jnp.clip(_dot(x16, w16_ref[1]) + bu_ref[0], -SWIGLU_LIMIT, SWIGLU_LIMIT)
        act = (up + 1.0) * gate * jax.nn.sigmoid(SWIGLU_ALPHA * gate)
        _store_rows(ys_ref, _pack_bf16_pairs(_dot(act.astype(BF16), w16_ref[2]) + bd_ref[0]))

    @pl.when(t >= nu_ref[0])
    def _():
        ys_ref[...] = jnp.zeros_like(ys_ref)


def _experts(te, n_used, xs, wts):
    n_slots = xs.shape[0] // ROW_SUB
    n_tiles = n_slots // EXPERT_TILE
    blk = (EXPERT_TILE * ROW_SUB, LANES)
    tile = lambda t, te_ref, nu_ref: (jnp.minimum(t, nu_ref[0] - 1), 0)
    wsel = lambda t, te_ref, nu_ref: (te_ref[t], 0, 0)
    grid_spec = pltpu.PrefetchScalarGridSpec(
        num_scalar_prefetch=2,
        grid=(n_tiles,),
        in_specs=[pl.BlockSpec(blk, tile),
                  pl.BlockSpec((1, D_MODEL, D_EXPERT), wsel), pl.BlockSpec((1, 1, D_EXPERT), wsel),
                  pl.BlockSpec((1, D_MODEL, D_EXPERT), wsel), pl.BlockSpec((1, 1, D_EXPERT), wsel),
                  pl.BlockSpec((1, D_EXPERT, D_MODEL), wsel), pl.BlockSpec((1, 1, D_MODEL), wsel)],
        out_specs=pl.BlockSpec(blk, lambda t, te_ref, nu_ref: (t, 0)),
        scratch_shapes=[pltpu.VMEM((3, D_MODEL, D_EXPERT), BF16)])
    return pl.pallas_call(
        _expert_kernel,
        grid_spec=grid_spec,
        out_shape=jax.ShapeDtypeStruct(xs.shape, U32),
        name="moe_experts",
        compiler_params=_cparams(("arbitrary",)),
    )(te, n_used, xs, *wts)


def _combine_kernel(slot_ref, x_ref, rw_ref, ys_ref, g_ref, b_ref, o_ref, buf_ref, sem, *, alpha, tiles_per_seq):
    tm = x_ref.shape[0]

    def issue(r, _):
        dst_row = pl.multiple_of(ROW_SUB * r, ROW_SUB)
        for k in range(TOP_K):
            src_row = pl.multiple_of(slot_ref[0, 0, TOP_K * r + k], ROW_SUB)
            pltpu.make_async_copy(ys_ref.at[pl.ds(src_row, ROW_SUB)], buf_ref.at[k, pl.ds(dst_row, ROW_SUB)],
                                  sem).start(priority=k % 2)
        return 0

    lax.fori_loop(0, tm, issue, 0, unroll=8)
    for k in range(TOP_K):
        pltpu.make_async_copy(ys_ref.at[pl.ds(0, tm * ROW_SUB)], buf_ref.at[k], sem).wait()
    rw = rw_ref[...]
    acc = None
    for k in range(TOP_K):
        term = rw[:, k:k + 1] * _unpack_bf16_pairs(_load_rows(buf_ref.at[k]))
        acc = term if acc is None else acc + term
    out = _ln_rows(alpha * x_ref[...] + acc, g_ref[...], b_ref[...])
    row = lax.broadcasted_iota(I32, out.shape, 0)
    first_tile = lax.rem(pl.program_id(0), tiles_per_seq) == 0
    o_ref[...] = jnp.where(first_tile & (row < PAD_FRONT), 0.0, out)


def _combine(slot3, h1, route_w, ys, wts, alpha, tm, tp):
    n_rows = h1.shape[0]
    assert tp % tm == 0 and tm >= PAD_FRONT
    return pl.pallas_call(
        functools.partial(_combine_kernel, alpha=alpha, tiles_per_seq=tp // tm),
        grid=(n_rows // tm,),
        in_specs=[pl.BlockSpec((1, 1, TOP_K * tm), lambda i: (i, 0, 0), memory_space=pltpu.SMEM),
                  pl.BlockSpec((tm, D_MODEL), lambda i: (i, 0)),
                  pl.BlockSpec((tm, LANES), lambda i: (i, 0)),
                  pl.BlockSpec(memory_space=pl.ANY),
                  _const_spec((1, D_MODEL)), _const_spec((1, D_MODEL))],
        out_specs=pl.BlockSpec((tm, D_MODEL), lambda i: (i, 0)),
        out_shape=jax.ShapeDtypeStruct((n_rows, D_MODEL), F32),
        scratch_shapes=[pltpu.VMEM((TOP_K, tm * ROW_SUB, LANES), U32), pltpu.SemaphoreType.DMA(())],
        name="moe_combine_ln2",
        compiler_params=_cparams(("arbitrary",)),
    )(slot3, h1, route_w, ys, wts["ln2_g"], wts["ln2_b"])


def _moe(h1, h1p, route_i, route_w, cnt, wts, expert_wts, layer, alpha, tm, tp):
    n_rows = h1.shape[0]
    n_tiles = -(-TOP_K * n_rows // EXPERT_TILE) + N_EXPERTS
    slot, te, n_used = _route_plan(route_i, cnt, n_tiles)
    slot3 = (slot * ROW_SUB).reshape(n_rows // tm, 1, TOP_K * tm)
    xs = _dispatch(slot3, h1p, n_tiles * EXPERT_TILE, tm)
    ys = _experts(te + layer * N_EXPERTS, n_used, xs, expert_wts)
    return _combine(slot3, h1, route_w, ys, wts, alpha, tm, tp)


def _rot_half_cols(w, d):
    n = w.shape[1] // d
    w3 = w.reshape(w.shape[0], n, d)
    return jnp.concatenate([-w3[..., d // 2:], w3[..., :d // 2]], -1).reshape(w.shape)


def _tables(tp):
    pos = jnp.maximum(jnp.arange(tp, dtype=I32) - PAD_FRONT, 0).astype(F32)

    def cs(d):
        inv = ROPE_BASE ** (-jnp.arange(0, d, 2, dtype=F32) / d)
        ang = pos[:, None] * inv[None, :]
        return jnp.cos(ang), jnp.sin(ang)

    rc, rs = cs(RET_DK)
    ret_cos = jnp.tile(jnp.concatenate([rc, rc], -1), (1, 2 * RET_HEADS))
    ret_sin = jnp.tile(jnp.concatenate([rs, rs], -1), (1, 2 * RET_HEADS))
    mc, ms = cs(MLA_ROPE)
    ones = jnp.ones((tp, MLA_NOPE), F32)
    tail1 = jnp.ones((tp, LANES - MLA_NOPE - MLA_ROPE), F32)
    mla_cos = jnp.concatenate([ones, mc, mc, tail1], -1)
    mla_sin = jnp.concatenate([0 * ones, ms, ms, 0 * tail1], -1)
    log_gamma = jnp.log1p(-jnp.exp2(-5.0 - jnp.arange(RET_HEADS, dtype=F32)))
    idx = jnp.arange(CHUNK, dtype=F32)
    rel = idx[:, None] - idx[None, :]
    dmat = jnp.where(rel >= 0, jnp.exp(log_gamma[:, None, None] * jnp.maximum(rel, 0.0)), 0.0)
    k_dec = jnp.exp(log_gamma[:, None] * (CHUNK - 1.0 - idx))
    q_dec = jnp.exp(log_gamma[:, None] * (idx + 1.0))
    rep = lambda t: jnp.repeat(t.T, RET_DK, axis=1)
    ret_dec = jnp.concatenate([rep(q_dec), rep(k_dec)], -1)
    ret_cdec = jnp.exp(log_gamma * CHUNK)
    return dict(ret_cos=ret_cos, ret_sin=ret_sin, mla_cos=mla_cos, mla_sin=mla_sin, ret_dmat=dmat, ret_dec=ret_dec,
                ret_cdec=ret_cdec)


def _layer_weights(w_in, w_merge, b_merge, g_cq, g_ckv, w_uq, w_ukv, conv_w, conv_b, w_lru_a, b_lru_a, w_lru_x,
                   b_lru_x, lru_lambda, w_gla_a2, b_gla_a2, w_branch, w_out, ln1_g, ln1_b, w_router, b_router,
                   ln2_g, ln2_b):
    D = D_MODEL
    z = lambda n: jnp.zeros((D, n), F32)
    rqk = jnp.concatenate([w_in[:, _O_RQ:_O_RK], w_in[:, _O_RK:_O_RV] * (RET_DK ** -0.5)], -1)
    kr = w_in[:, _O_KR:_O_LX]
    kr_slot = jnp.concatenate([z(MLA_NOPE), kr, z(LANES - MLA_NOPE - MLA_ROPE)], -1)
    kr_rot_slot = jnp.concatenate([z(MLA_NOPE), _rot_half_cols(kr, MLA_ROPE), z(LANES - MLA_NOPE - MLA_ROPE)], -1)
    gqk = jnp.concatenate([w_in[:, _O_GQ:_O_GK], w_in[:, _O_GK:_O_GV] * (GLA_DK ** -0.5)], -1)
    w16 = jnp.concatenate([rqk, _rot_half_cols(rqk, RET_DK), w_in[:, _O_RV:_O_RG],
                           w_in[:, _O_CQ:_O_KR], kr_slot, gqk, w_in[:, _O_GV:_O_GA], kr_rot_slot], -1).astype(BF16)
    w32 = jnp.concatenate([w_in[:, _O_RG:_O_CQ], w_in[:, _O_LX:_O_LG], w_in[:, _O_LG:_O_GQ], w_in[:, _O_GG:],
                           w_in[:, _O_GA:_O_GG], z(LANES - GLA_RANK)], -1).astype(BF16)
    hq = MLA_NOPE + MLA_ROPE
    uq3 = w_uq.reshape(MLA_Q_RANK, MLA_HEADS, hq)
    zq = jnp.zeros((MLA_Q_RANK, MLA_HEADS, LANES - hq), F32)
    uq = jnp.concatenate([uq3, zq], -1).reshape(MLA_Q_RANK, MLA_HEADS * LANES)
    uq_rope_rot = _rot_half_cols(uq3[..., MLA_NOPE:].reshape(MLA_Q_RANK, -1), MLA_ROPE)
    uq_rope_rot = uq_rope_rot.reshape(MLA_Q_RANK, MLA_HEADS, MLA_ROPE)
    uqr = jnp.concatenate([jnp.zeros((MLA_Q_RANK, MLA_HEADS, MLA_NOPE), F32), uq_rope_rot, zq], -1)
    uqr = uqr.reshape(MLA_Q_RANK, MLA_HEADS * LANES)
    eye = jnp.eye(LRU_BLOCKS, dtype=F32)
    bd = lambda w: jnp.einsum("ncd,nm->ncmd", w, eye).reshape(LRU_WIDTH, LRU_WIDTH)
    w_ab = jnp.concatenate([bd(w_lru_a), bd(w_lru_x)], -1).astype(BF16)
    wa2 = jnp.concatenate([w_gla_a2, jnp.zeros((LANES - GLA_RANK, GLA_HEADS * GLA_DK), F32)], 0).astype(BF16)
    wr = jnp.concatenate([w_router, jnp.zeros((D, LANES - N_EXPERTS), F32)], -1).astype(BF16)
    br = jnp.concatenate([b_router, jnp.full((LANES - N_EXPERTS,), -jnp.inf, F32)]).reshape(1, LANES)
    row = lambda v: v.reshape(1, -1).astype(F32)
    return dict(
        w16=w16, w32=w32,
        g_cq=row(g_cq), g_ckv=row(g_ckv), w_uq=uq.astype(BF16), w_uq_rot=uqr.astype(BF16), w_ukv=w_ukv.astype(BF16),
        conv_w=conv_w, conv_b=row(conv_b), w_lru_ab=w_ab, b_lru_ab=row(jnp.concatenate([b_lru_a, b_lru_x])),
        lru_ncsp=row(-LRU_C * jax.nn.softplus(-lru_lambda)),
        w_gla_a2=wa2, b_gla_a2=row(b_gla_a2),
        w_merge=w_merge.astype(BF16), b_merge=row(b_merge), w_branch=w_branch.astype(BF16), w_out=w_out.astype(BF16),
        ln1_g=row(ln1_g), ln1_b=row(ln1_b), w_router=wr, b_router=br,
        ln2_g=row(ln2_g), ln2_b=row(ln2_b))


def _row_tile(n_rows, want):
    best = CHUNK
    for t in range(CHUNK, want + 1, CHUNK):
        if n_rows % t == 0:
            best = t
    return best


def kernel(x, meta_tokens, ln0_g, ln0_b, w_in, w_merge, b_merge, g_cq, g_ckv, w_uq, w_ukv, conv_w, conv_b, w_lru_a, b_lru_a, w_lru_x, b_lru_x, lru_lambda, w_gla_a2, b_gla_a2, w_branch, w_out, ln1_g, ln1_b, w_router, b_router, w_exp_gate, b_exp_gate, w_exp_up, b_exp_up, w_exp_down, b_exp_down, ln2_g, ln2_b):
    B, S, D = x.shape
    depth = w_in.shape[0]
    alpha = (2.0 * depth) ** 0.25
    tp = S + CHUNK
    nc = tp // CHUNK
    n_rows = B * tp
    tabs = _tables(tp)
    per_layer = (w_in, w_merge, b_merge, g_cq, g_ckv, w_uq, w_ukv, conv_w, conv_b, w_lru_a, b_lru_a, w_lru_x, b_lru_x,
                 lru_lambda, w_gla_a2, b_gla_a2, w_branch, w_out, ln1_g, ln1_b, w_router, b_router, ln2_g, ln2_b)
    n_le = depth * N_EXPERTS
    expert_wts = (w_exp_gate.reshape(n_le, D, D_EXPERT), b_exp_gate.reshape(n_le, 1, D_EXPERT),
                  w_exp_up.reshape(n_le, D, D_EXPERT), b_exp_up.reshape(n_le, 1, D_EXPERT),
                  w_exp_down.reshape(n_le, D_EXPERT, D), b_exp_down.reshape(n_le, 1, D))
    tm = _row_tile(tp, 640)
    t_att = _row_tile(tp, 1664)
    h = _ln0(x, meta_tokens.astype(x.dtype), ln0_g, ln0_b).reshape(n_rows, D)
    for l in range(depth):
        wts = _layer_weights(*(p[l] for p in per_layer))
        p16 = _proj(h, wts["w16"], BF16, tm)
        p32 = _proj(h, wts["w32"], F32, tm)
        y_ret = _retention(p16, p32, tabs, B, tp, tm)
        qt, kh, vt = _mla_prep(p16, tabs, wts, B, tp, tm)
        y_mla = _flash(qt, kh, vt, t_att)
        y_lru = _lru(p32, wts, B, tp, tm)
        y_gla = _gla(p16, p32, wts, B, nc)
        h1, h1p, route_i, route_w, cnt = _merge(h, (y_ret, y_mla, y_lru, y_gla), wts, alpha, tm)
        h = _moe(h1, h1p, route_i, route_w, cnt, wts, expert_wts, l, alpha, tm, tp)
    return h.reshape(B, tp, D)[:, CHUNK:]
```

```python
import functools

import jax
import jax.numpy as jnp
from jax import lax
from jax.experimental import pallas as pl
from jax.experimental.pallas import tpu as pltpu

F32 = jnp.float32
BF16 = jnp.bfloat16
U32 = jnp.uint32
I32 = jnp.int32

D_MODEL = 1024
N_META = 16
CHUNK = 128
PAD_FRONT = CHUNK - N_META
ROPE_BASE = 10000.0
NEG_BIG = -1e30

RET_HEADS, RET_DK, RET_DV = 4, 64, 128
MLA_HEADS, MLA_Q_RANK, MLA_KV_RANK, MLA_NOPE, MLA_ROPE, MLA_DV = 8, 256, 128, 64, 32, 64
LRU_WIDTH, LRU_BLOCKS, CONV_W, LRU_C = 512, 8, 4, 8.0
LRU_BLOCK = LRU_WIDTH // LRU_BLOCKS
GLA_HEADS, GLA_DK, GLA_DV, GLA_RANK, GLA_TAU, GLA_CHUNK = 4, 64, 128, 16, 16.0, 16
N_BRANCH, BRANCH_W = 4, 512
N_EXPERTS, TOP_K, D_EXPERT = 32, 4, 1024
SWIGLU_LIMIT, SWIGLU_ALPHA = 7.0, 1.702

LANES = 128
VMEM_LIMIT = 56 * 1024 * 1024
EXPERT_TILE = 512

_O_RQ, _O_RK, _O_RV, _O_RG = 0, 256, 512, 1024
_O_CQ, _O_CKV, _O_KR = 1536, 1792, 1920
_O_LX, _O_LG = 1952, 2464
_O_GQ, _O_GK, _O_GV, _O_GA, _O_GG = 2976, 3232, 3488, 4000, 4016


def _cparams(sem):
    return pltpu.CompilerParams(dimension_semantics=sem, vmem_limit_bytes=VMEM_LIMIT)


def _const_spec(shape):
    nd = len(shape)
    return pl.BlockSpec(shape, lambda *_: (0,) * nd, pipeline_mode=pl.Buffered(1))


def _ln_rows(x, g, b, eps=1e-5):
    mu = jnp.mean(x, -1, keepdims=True)
    xc = x - mu
    var = jnp.mean(xc * xc, -1, keepdims=True)
    return xc * lax.rsqrt(var + eps) * g + b


def _dot(a, b):
    return jnp.dot(a, b, preferred_element_type=F32)


def _dot_nt(a, b):
    return lax.dot_general(a, b, (((1,), (1,)), ((), ())), preferred_element_type=F32)


def _pack_bf16_pairs(x):
    n = x.shape[1] // 2
    bits = lax.bitcast_convert_type(x.astype(BF16).astype(F32), U32)
    return (bits[:, :n] & U32(0xFFFF0000)) | (bits[:, n:] >> 16)


def _unpack_bf16_pairs(w):
    hi = lax.bitcast_convert_type(w & U32(0xFFFF0000), F32)
    lo = lax.bitcast_convert_type(w << 16, F32)
    return jnp.concatenate([hi, lo], axis=1)


ROW_SUB = D_MODEL // 2 // LANES


def _store_rows(ref, words):
    rows = words.shape[0]
    for j in range(ROW_SUB):
        ref[pl.ds(j, rows, stride=ROW_SUB), :] = words[:, j * LANES:(j + 1) * LANES]


def _load_rows(ref):
    rows = ref.shape[0] // ROW_SUB
    return jnp.concatenate([ref[pl.ds(j, rows, stride=ROW_SUB), :] for j in range(ROW_SUB)], axis=1)


def _ln0_kernel(*refs):
    *x_refs, meta_ref, g_ref, b_ref, o_ref = refs
    i = pl.program_id(1)
    g = g_ref[...]
    b = b_ref[...]
    for j, x_ref in enumerate(x_refs):
        rows = slice(j * CHUNK, (j + 1) * CHUNK)
        if j == 0:
            @pl.when(i == 0)
            def _():
                o_ref[0, :PAD_FRONT, :] = jnp.zeros((PAD_FRONT, D_MODEL), F32)
                o_ref[0, PAD_FRONT:CHUNK, :] = _ln_rows(meta_ref[...], g, b)

            @pl.when(i > 0)
            def _():
                o_ref[0, rows, :] = _ln_rows(x_ref[0], g, b)
        else:
            o_ref[0, rows, :] = _ln_rows(x_ref[0], g, b)


def _ln0(x, meta, g, b, tr):
    B, S, D = x.shape
    tp = S + CHUNK
    n = tr // CHUNK
    xspec = lambda j: pl.BlockSpec((1, CHUNK, D), lambda bb, i: (bb, jnp.maximum(n * i + j - 1, 0), 0))
    return pl.pallas_call(
        _ln0_kernel,
        grid=(B, tp // tr),
        in_specs=[xspec(j) for j in range(n)] + [pl.BlockSpec((N_META, D), lambda bb, i: (0, 0)),
                                                   pl.BlockSpec((1, D), lambda bb, i: (0, 0)),
                                                   pl.BlockSpec((1, D), lambda bb, i: (0, 0))],
        out_specs=pl.BlockSpec((1, tr, D), lambda bb, i: (bb, i, 0)),
        out_shape=jax.ShapeDtypeStruct((B, tp, D), F32),
        name="ln0",
        compiler_params=_cparams(("parallel", "arbitrary")),
    )(*([x] * n), meta, g.reshape(1, D), b.reshape(1, D))


def _proj_kernel(h_ref, w_ref, o_ref, *, col_step):
    h = h_ref[...].astype(BF16)
    n = w_ref.shape[1]
    for c0 in range(0, n, col_step):
        c1 = min(c0 + col_step, n)
        o_ref[:, c0:c1] = _dot(h, w_ref[:, c0:c1]).astype(o_ref.dtype)


def _proj(h, w, out_dtype, tm):
    n_rows, d = h.shape
    n = w.shape[1]
    return pl.pallas_call(
        functools.partial(_proj_kernel, col_step=512),
        grid=(n_rows // tm,),
        in_specs=[pl.BlockSpec((tm, d), lambda i: (i, 0)), _const_spec((d, n))],
        out_specs=pl.BlockSpec((tm, n), lambda i: (i, 0)),
        out_shape=jax.ShapeDtypeStruct((n_rows, n), out_dtype),
        name="in_proj",
        compiler_params=_cparams(("parallel",)),
    )(h, w)


def _half_masks(dtype):
    lane = lax.broadcasted_iota(I32, (1, LANES), 1)
    lo = (lane < LANES // 2).astype(dtype)
    return lo, (1 - lo).astype(dtype)


def _ret_kernel(cdec_ref, qk_ref, rot_ref, v_ref, g_ref, cos_ref, sin_ref, dec_ref, dmat_ref, o_ref, s_ref):
    c = pl.program_id(1)

    @pl.when(c == 0)
    def _():
        s_ref[...] = jnp.zeros_like(s_ref)

    r_all = qk_ref[...].astype(F32) * cos_ref[...] + rot_ref[...].astype(F32) * sin_ref[...]
    masks = _half_masks(F32)
    states = [s_ref[hd] for hd in range(RET_HEADS)]
    for ch in range(qk_ref.shape[0] // CHUNK):
        rows = slice(ch * CHUNK, (ch + 1) * CHUNK)
        r = r_all[rows]
        rd = r * dec_ref[...]
        for hd in range(RET_HEADS):
            pair, half = divmod(hd, 2)
            m = masks[half]
            q_lo, k_lo = pair * LANES, 2 * LANES + pair * LANES
            qp = (r[:, q_lo:q_lo + LANES] * m).astype(BF16)
            kp = r[:, k_lo:k_lo + LANES].astype(BF16)
            vh = v_ref[rows, hd * RET_DV:(hd + 1) * RET_DV]
            scores = _dot_nt(qp, kp) * dmat_ref[hd]
            o = _dot(scores.astype(BF16), vh)
            qin = (rd[:, q_lo:q_lo + LANES] * m).astype(BF16)
            o = o + _dot(qin, states[hd].astype(BF16))
            kdec_t = (rd[:, k_lo:k_lo + LANES] * m).T.astype(BF16)
            states[hd] = states[hd] * cdec_ref[hd] + _dot(kdec_t, vh)
            mu = jnp.mean(o, -1, keepdims=True)
            oc = o - mu
            var = jnp.mean(oc * oc, -1, keepdims=True)
            on = oc * lax.rsqrt(var + 1e-5)
            gate = g_ref[rows, hd * RET_DV:(hd + 1) * RET_DV]
            o_ref[rows, hd * RET_DV:(hd + 1) * RET_DV] = (gate * jax.nn.sigmoid(gate) * on).astype(o_ref.dtype)
    for hd in range(RET_HEADS):
        s_ref[hd] = states[hd]


def _retention(p16, p32, tabs, B, tp, tr):
    n_rows = p16.shape[0]
    nc = tp // tr
    row = lambda b, c: b * nc + c
    return pl.pallas_call(
        _ret_kernel,
        grid=(B, nc),
        in_specs=[pl.BlockSpec(memory_space=pltpu.SMEM),
                  pl.BlockSpec((tr, 512), lambda b, c: (row(b, c), 0)),
                  pl.BlockSpec((tr, 512), lambda b, c: (row(b, c), 1)),
                  pl.BlockSpec((tr, 512), lambda b, c: (row(b, c), 2)),
                  pl.BlockSpec((tr, 512), lambda b, c: (row(b, c), 0)),
                  pl.BlockSpec((tr, 512), lambda b, c: (c, 0)),
                  pl.BlockSpec((tr, 512), lambda b, c: (c, 0)),
                  _const_spec((CHUNK, 512)),
                  _const_spec((RET_HEADS, CHUNK, CHUNK))],
        out_specs=pl.BlockSpec((tr, 512), lambda b, c: (row(b, c), 0)),
        out_shape=jax.ShapeDtypeStruct((n_rows, 512), BF16),
        scratch_shapes=[pltpu.VMEM((RET_HEADS, LANES, RET_DV), F32)],
        name="retention",
        compiler_params=_cparams(("parallel", "arbitrary")),
    )(tabs["ret_cdec"], p16, p16, p16, p32, tabs["ret_cos"], tabs["ret_sin"], tabs["ret_dec"], tabs["ret_dmat"])


MLA_BIAS_LANE = MLA_NOPE + MLA_ROPE
M_INIT = -3.0e38
LOG2_E = 1.4426950408889634


def _mla_prep_kernel(lat_ref, krot_ref, cos_ref, sin_ref, gq_ref, gkv_ref, wq_ref, wqr_ref, wkv_ref,
                     qt_ref, k_ref, vt_ref):
    i = pl.program_id(1)
    tr = lat_ref.shape[0]
    lat = lat_ref[...].astype(F32)
    cq = lat[:, :MLA_Q_RANK]
    ckv = lat[:, MLA_Q_RANK:MLA_Q_RANK + MLA_KV_RANK]
    kr = lat[:, MLA_Q_RANK + MLA_KV_RANK:]
    cos = cos_ref[...]
    sin = sin_ref[...]
    cqn = (cq * lax.rsqrt(jnp.mean(cq * cq, -1, keepdims=True) + 1e-6) * gq_ref[...]).astype(BF16)
    ckvn = (ckv * lax.rsqrt(jnp.mean(ckv * ckv, -1, keepdims=True) + 1e-6) * gkv_ref[...]).astype(BF16)
    krr = kr * cos + krot_ref[...].astype(F32) * sin
    scale = (MLA_NOPE + MLA_ROPE) ** -0.5 * LOG2_E
    lane = lax.broadcasted_iota(I32, (tr, LANES), 1)
    row = lax.broadcasted_iota(I32, (tr, LANES), 0)
    is_nope = lane < MLA_NOPE
    is_bias = lane == MLA_BIAS_LANE
    kbias = jnp.where(is_bias & (i == 0) & (row < PAD_FRONT), NEG_BIG, 0.0)
    for hd in range(MLA_HEADS):
        sl = slice(hd * LANES, (hd + 1) * LANES)
        q = (_dot(cqn, wq_ref[:, sl]) * cos + _dot(cqn, wqr_ref[:, sl]) * sin) * scale
        qt_ref[0, hd] = jnp.where(is_bias, 1.0, q).T.astype(qt_ref.dtype)
        kv = _dot(ckvn, wkv_ref[:, sl])
        k_ref[0, hd] = (jnp.where(is_nope, kv, krr) + kbias).astype(k_ref.dtype)
        vt_ref[0, hd] = kv.T[MLA_NOPE:, :].astype(vt_ref.dtype)


def _mla_prep(p16, tabs, wts, B, tp, tr):
    nt = tp // tr
    assert tr >= PAD_FRONT
    row = lambda b, i: b * nt + i
    hm = jax.ShapeDtypeStruct((B, MLA_HEADS, tp, LANES), BF16)
    hm_t = jax.ShapeDtypeStruct((B, MLA_HEADS, LANES, tp), BF16)
    hm_v = jax.ShapeDtypeStruct((B, MLA_HEADS, MLA_DV, tp), BF16)
    hspec = pl.BlockSpec((1, MLA_HEADS, tr, LANES), lambda b, i: (b, 0, i, 0))
    hspec_t = pl.BlockSpec((1, MLA_HEADS, LANES, tr), lambda b, i: (b, 0, 0, i))
    hspec_v = pl.BlockSpec((1, MLA_HEADS, MLA_DV, tr), lambda b, i: (b, 0, 0, i))
    return pl.pallas_call(
        _mla_prep_kernel,
        grid=(B, nt),
        in_specs=[pl.BlockSpec((tr, 512), lambda b, i: (row(b, i), 3)),
                  pl.BlockSpec((tr, LANES), lambda b, i: (row(b, i), 3072 // LANES)),
                  pl.BlockSpec((tr, LANES), lambda b, i: (i, 0)),
                  pl.BlockSpec((tr, LANES), lambda b, i: (i, 0)),
                  _const_spec((1, MLA_Q_RANK)), _const_spec((1, MLA_KV_RANK)),
                  _const_spec((MLA_Q_RANK, MLA_HEADS * LANES)), _const_spec((MLA_Q_RANK, MLA_HEADS * LANES)),
                  _const_spec((MLA_KV_RANK, MLA_HEADS * LANES))],
        out_specs=[hspec_t, hspec, hspec_v],
        out_shape=[hm_t, hm, hm_v],
        name="mla_prep",
        compiler_params=_cparams(("parallel", "arbitrary")),
    )(p16, p16, tabs["mla_cos"], tabs["mla_sin"], wts["g_cq"], wts["g_ckv"], wts["w_uq"], wts["w_uq_rot"],
      wts["w_ukv"])


def _flash_kernel(qt_ref, k_ref, vt_ref, o_ref, *, tb):
    tp = k_ref.shape[2]
    nb = tp // tb
    ng = tb // LANES
    for hh in range(2):
        def step(qts, j, stats, diag):
            k0 = pl.multiple_of(j * tb, tb)
            new = []
            for g in range(ng):
                m, l, acc = stats[g]
                nk = (g + 1) * LANES if diag else tb
                k = k_ref[0, hh, pl.ds(k0, nk), :]
                vt = vt_ref[0, hh, :, pl.ds(k0, nk)]
                vt = jnp.concatenate([vt, jnp.zeros_like(vt)], axis=0)
                s = _dot(k, qts[g])
                if diag:
                    key_i = lax.broadcasted_iota(I32, (nk, LANES), 0)
                    qry_i = lax.broadcasted_iota(I32, (nk, LANES), 1)
                    s = jnp.where(key_i <= qry_i + g * LANES, s, NEG_BIG)
                m_new = jnp.maximum(m, jnp.max(s, 0, keepdims=True))
                alpha = jnp.exp2(m - m_new)
                p = jnp.exp2(s - m_new)
                l_new = alpha * l + jnp.sum(p, 0, keepdims=True)
                acc_new = alpha * acc + _dot(vt, p.astype(BF16))
                new.append((m_new, l_new, acc_new))
            return tuple(new)

        def q_block(i, _):
            q0 = i * tb
            qts = [qt_ref[0, hh, :, pl.ds(pl.multiple_of(q0 + g * LANES, LANES), LANES)] for g in range(ng)]
            init = tuple((jnp.full((1, LANES), M_INIT, F32), jnp.zeros((1, LANES), F32),
                          jnp.zeros((LANES, LANES), F32)) for _ in range(ng))
            carry = lax.fori_loop(0, i, lambda j, c: step(qts, j, c, False), init)
            carry = step(qts, i, carry, True)
            for g in range(ng):
                m, l, acc = carry[g]
                rows = pl.ds(pl.multiple_of(q0 + g * LANES, LANES), LANES)
                o_ref[rows, hh * MLA_DV:(hh + 1) * MLA_DV] = (acc / l).T[:, :MLA_DV]
            return 0

        lax.fori_loop(0, nb, q_block, 0)


def _flash(qt, kh, vt, tb):
    B, H, tp, _ = kh.shape
    npair = H // 2
    spec = pl.BlockSpec((1, 2, tp, LANES), lambda b, j: (b, j, 0, 0))
    spec_t = pl.BlockSpec((1, 2, LANES, tp), lambda b, j: (b, j, 0, 0))
    spec_v = pl.BlockSpec((1, 2, MLA_DV, tp), lambda b, j: (b, j, 0, 0))
    return pl.pallas_call(
        functools.partial(_flash_kernel, tb=tb),
        grid=(B, npair),
        in_specs=[spec_t, spec, spec_v],
        out_specs=pl.BlockSpec((tp, LANES), lambda b, j: (b, j)),
        out_shape=jax.ShapeDtypeStruct((B * tp, npair * LANES), F32),
        name="mla_flash",
        compiler_params=_cparams(("parallel", "arbitrary")),
    )(qt, kh, vt)


def _lru_kernel(x_ref, g_ref, cw_ref, cb_ref, wab_ref, bab_ref, ncsp_ref, o_ref, hist_ref, h_ref):
    c = pl.program_id(1)
    tr = x_ref.shape[0]

    @pl.when(c == 0)
    def _():
        hist_ref[...] = jnp.zeros_like(hist_ref)
        h_ref[...] = jnp.zeros_like(h_ref)

    x = x_ref[...]
    xcat = jnp.concatenate([hist_ref[...], x], axis=0)
    hist_ref[...] = x[tr - 8:, :]
    xc = cb_ref[...] + cw_ref[CONV_W - 1:CONV_W, :] * x
    for j in range(CONV_W - 1):
        off = 8 - (CONV_W - 1) + j
        xc = xc + cw_ref[j:j + 1, :] * xcat[off:off + tr, :]
    z = _dot(xc.astype(BF16), wab_ref[...]) + bab_ref[...]
    r = jax.nn.sigmoid(z[:, :LRU_WIDTH])
    i = jax.nn.sigmoid(z[:, LRU_WIDTH:])
    log_a = ncsp_ref[...] * r
    a_all = jnp.exp(log_a)
    u_all = jnp.sqrt(1.0 - jnp.exp(2.0 * log_a)) * (i * xc)
    row = lax.broadcasted_iota(I32, (CHUNK, LRU_WIDTH), 0)
    h = h_ref[...]
    for sub in range(tr // CHUNK):
        rows = slice(sub * CHUNK, (sub + 1) * CHUNK)
        a = a_all[rows]
        u = u_all[rows]
        if sub == 0:
            u = jnp.where((c == 0) & (row < PAD_FRONT), 0.0, u)
        s = 1
        while s < CHUNK:
            keep = row >= s
            a_sh = jnp.where(keep, pltpu.roll(a, s, axis=0), 1.0)
            u_sh = jnp.where(keep, pltpu.roll(u, s, axis=0), 0.0)
            u = a * u_sh + u
            a = a * a_sh
            s *= 2
        hs = u + a * h
        h = hs[CHUNK - 1:CHUNK, :]
        o_ref[rows, :] = (hs * jax.nn.gelu(g_ref[rows, :])).astype(o_ref.dtype)
    h_ref[...] = h


def _lru(p32, wts, B, tp, tr):
    n_rows = p32.shape[0]
    nc = tp // tr
    row = lambda b, c: b * nc + c
    return pl.pallas_call(
        _lru_kernel,
        grid=(B, nc),
        in_specs=[pl.BlockSpec((tr, 512), lambda b, c: (row(b, c), 1)),
                  pl.BlockSpec((tr, 512), lambda b, c: (row(b, c), 2)),
                  _const_spec((CONV_W, LRU_WIDTH)), _const_spec((1, LRU_WIDTH)),
                  _const_spec((LRU_WIDTH, 2 * LRU_WIDTH)), _const_spec((1, 2 * LRU_WIDTH)),
                  _const_spec((1, LRU_WIDTH))],
        out_specs=pl.BlockSpec((tr, 512), lambda b, c: (row(b, c), 0)),
        out_shape=jax.ShapeDtypeStruct((n_rows, 512), BF16),
        scratch_shapes=[pltpu.VMEM((8, LRU_WIDTH), F32), pltpu.VMEM((1, LRU_WIDTH), F32)],
        name="rglru",
        compiler_params=_cparams(("parallel", "arbitrary")),
    )(p32, p32, wts["conv_w"], wts["conv_b"], wts["w_lru_ab"], wts["b_lru_ab"], wts["lru_ncsp"])


GLA_LEVELS = (64, 32, 16, 8, 4, 2, 1)


def _gla_kernel(qk_ref, v_ref, a_ref, g_ref, wa2_ref, ba2_ref, o_ref, st_ref):
    c = pl.program_id(1)
    tr = qk_ref.shape[0]
    hk = GLA_HEADS * GLA_DK

    @pl.when(c == 0)
    def _():
        st_ref[...] = jnp.zeros_like(st_ref)

    x = _dot(a_ref[...].astype(BF16), wa2_ref[...]) + ba2_ref[...]
    la = (jnp.minimum(x, 0.0) - jnp.log(1.0 + jnp.exp(-jnp.abs(x)))) * (1.0 / GLA_TAU)
    ri = lax.broadcasted_iota(I32, (tr, tr), 0)
    ci = lax.broadcasted_iota(I32, (tr, tr), 1)
    tri = jnp.where(ci <= ri, 1.0, 0.0).astype(BF16)
    la_hi = la.astype(BF16)
    rem = la - la_hi.astype(F32)
    la_mid = rem.astype(BF16)
    la_lo = (rem - la_mid.astype(F32)).astype(BF16)
    gcum = _dot(tri, la_hi) + _dot(tri, la_mid) + _dot(tri, la_lo)
    row = lax.broadcasted_iota(I32, (tr, hk), 0)
    own_end = {1: gcum}
    s = 1
    while s < GLA_LEVELS[0]:
        in_right = ((row >> (s.bit_length() - 1)) & 1) == 1
        own_end[2 * s] = jnp.where(in_right, own_end[s], pltpu.roll(own_end[s], tr - s, axis=0))
        s *= 2
    qk = qk_ref[...].astype(F32)
    q = qk[:, :hk]
    k = qk[:, hk:]
    terms = [(q.astype(BF16), k.astype(BF16), ri == ci)]
    for s in GLA_LEVELS:
        r_own = own_end[s]
        r_prev = pltpu.roll(r_own, s, axis=0)
        qs = (q * jnp.exp(jnp.minimum(gcum - r_prev, 0.0))).astype(BF16)
        ks = (k * jnp.exp(jnp.minimum(r_own - gcum, 0.0))).astype(BF16)
        sh = s.bit_length() - 1
        bi = ri >> sh
        terms.append((qs, ks, ((bi & 1) == 1) & ((ci >> sh) == bi - 1)))
    gend = gcum[tr - 1:tr, :]
    qg = (q * jnp.exp(gcum)).astype(BF16)
    kg = (k * jnp.exp(gend - gcum)).astype(BF16)
    dec = jnp.exp(gend)
    masks = _half_masks(BF16)
    for hd in range(GLA_HEADS):
        pair, half = divmod(hd, 2)
        lanes = slice(pair * LANES, (pair + 1) * LANES)
        m = masks[half]
        amat = None
        for qs, ks, keep in terms:
            term = jnp.where(keep, _dot_nt(qs[:, lanes] * m, ks[:, lanes]), 0.0)
            amat = term if amat is None else amat + term
        vh = v_ref[:, hd * GLA_DV:(hd + 1) * GLA_DV]
        st = st_ref[hd]
        o = _dot(amat.astype(BF16), vh) + _dot_nt(qg[:, lanes] * m, st.astype(BF16))
        st_ref[hd] = st * dec[:, lanes] + _dot(vh.astype(F32).T.astype(BF16), kg[:, lanes])
        on = o * lax.rsqrt(jnp.mean(o * o, -1, keepdims=True) + 1e-6)
        gate = g_ref[:, hd * GLA_DV:(hd + 1) * GLA_DV]
        o_ref[:, hd * GLA_DV:(hd + 1) * GLA_DV] = (gate * jax.nn.sigmoid(gate) * on).astype(o_ref.dtype)


def _gla(p16, p32, wts, B, nc):
    n_rows = p16.shape[0]
    row = lambda b, c: b * nc + c
    return pl.pallas_call(
        _gla_kernel,
        grid=(B, nc),
        in_specs=[pl.BlockSpec((CHUNK, 512), lambda b, c: (row(b, c), 4)),
                  pl.BlockSpec((CHUNK, 512), lambda b, c: (row(b, c), 5)),
                  pl.BlockSpec((CHUNK, LANES), lambda b, c: (row(b, c), 2048 // LANES)),
                  pl.BlockSpec((CHUNK, 512), lambda b, c: (row(b, c), 3)),
                  _const_spec((LANES, GLA_HEADS * GLA_DK)), _const_spec((1, GLA_HEADS * GLA_DK))],
        out_specs=pl.BlockSpec((CHUNK, 512), lambda b, c: (row(b, c), 0)),
        out_shape=jax.ShapeDtypeStruct((n_rows, 512), BF16),
        scratch_shapes=[pltpu.VMEM((GLA_HEADS, GLA_DV, LANES), F32)],
        name="gla",
        compiler_params=_cparams(("parallel", "arbitrary")),
    )(p16, p16, p32, p32, wts["w_gla_a2"], wts["b_gla_a2"])


def _merge_kernel(h_ref, y0_ref, y1_ref, y2_ref, y3_ref, wm_ref, bm_ref, wb_ref, wo_ref, g_ref, b_ref, wr_ref, br_ref,
                  h1_ref, h1p_ref, ri_ref, rw_ref, cnt_out_ref, cnt_ref, *, alpha):
    @pl.when(pl.program_id(0) == 0)
    def _():
        cnt_ref[...] = jnp.zeros_like(cnt_ref)

    h = h_ref[...]
    tm = h.shape[0]
    h16 = h.astype(BF16)
    ys = (y0_ref[...], y1_ref[...].astype(BF16), y2_ref[...], y3_ref[...])
    mixed = None
    for n in range(N_BRANCH):
        sl = slice(n * D_MODEL, (n + 1) * D_MODEL)
        gate = jax.nn.sigmoid(_dot(h16, wm_ref[:, sl]) + bm_ref[:, sl])
        term = gate * _dot(ys[n], wb_ref[n])
        mixed = term if mixed is None else mixed + term
    mix = _dot(mixed.astype(BF16), wo_ref[...])
    h1 = _ln_rows(alpha * h + mix, g_ref[...], b_ref[...])
    h1_ref[...] = h1
    _store_rows(h1p_ref, _pack_bf16_pairs(h1))
    logits = _dot(h1.astype(BF16), wr_ref[...]) + br_ref[...]
    lane = lax.broadcasted_iota(I32, logits.shape, 1)
    vals, sels, idxs = [], [], []
    cur = logits
    for _ in range(TOP_K):
        mx = jnp.max(cur, -1, keepdims=True)
        idx = jnp.min(jnp.where(cur == mx, lane, LANES), -1, keepdims=True)
        sel = lane == idx
        vals.append(mx)
        sels.append(sel)
        idxs.append(idx)
        cur = jnp.where(sel, -jnp.inf, cur)
    es = [jnp.exp(v - vals[0]) for v in vals]
    den = es[0] + es[1] + es[2] + es[3]
    onehot = jnp.zeros_like(logits)
    for sel in sels:
        onehot = onehot + jnp.where(sel, 1.0, 0.0)
    rr = lax.broadcasted_iota(I32, (tm, tm), 0)
    cc = lax.broadcasted_iota(I32, (tm, tm), 1)
    lower = jnp.where(cc < rr, 1.0, 0.0).astype(BF16)
    base = cnt_ref[...] + _dot(lower, onehot.astype(BF16))
    route_i = jnp.zeros(logits.shape, I32)
    route_w = jnp.zeros_like(logits)
    for k in range(TOP_K):
        rank = jnp.sum(jnp.where(sels[k], base, 0.0), -1, keepdims=True).astype(I32)
        route_i = jnp.where(lane == k, idxs[k], route_i)
        route_i = jnp.where(lane == TOP_K + k, rank, route_i)
        route_w = jnp.where(lane == k, es[k] / den, route_w)
    ri_ref[...] = route_i
    rw_ref[...] = route_w
    cnt_ref[...] = cnt_ref[...] + jnp.sum(onehot, 0, keepdims=True)
    cnt_out_ref[...] = jnp.broadcast_to(cnt_ref[...], cnt_out_ref.shape)


def _merge(h, ys, wts, alpha, tm):
    n_rows = h.shape[0]
    rspec = lambda w: pl.BlockSpec((tm, w), lambda i: (i, 0))
    return pl.pallas_call(
        functools.partial(_merge_kernel, alpha=alpha),
        grid=(n_rows // tm,),
        in_specs=[rspec(D_MODEL), rspec(512), rspec(512), rspec(512), rspec(512),
                  _const_spec((D_MODEL, N_BRANCH * D_MODEL)), _const_spec((1, N_BRANCH * D_MODEL)),
                  _const_spec((N_BRANCH, BRANCH_W, D_MODEL)), _const_spec((D_MODEL, D_MODEL)),
                  _const_spec((1, D_MODEL)), _const_spec((1, D_MODEL)),
                  _const_spec((D_MODEL, LANES)), _const_spec((1, LANES))],
        out_specs=[rspec(D_MODEL), pl.BlockSpec((tm * ROW_SUB, LANES), lambda i: (i, 0)), rspec(LANES), rspec(LANES),
                   pl.BlockSpec((8, LANES), lambda i: (0, 0))],
        out_shape=[jax.ShapeDtypeStruct((n_rows, D_MODEL), F32), jax.ShapeDtypeStruct((n_rows * ROW_SUB, LANES), U32),
                   jax.ShapeDtypeStruct((n_rows, LANES), I32), jax.ShapeDtypeStruct((n_rows, LANES), F32),
                   jax.ShapeDtypeStruct((8, LANES), F32)],
        scratch_shapes=[pltpu.VMEM((1, LANES), F32)],
        name="merge_ln1_router",
        compiler_params=_cparams(("arbitrary",)),
    )(h, *ys, wts["w_merge"], wts["b_merge"], wts["w_branch"], wts["w_out"], wts["ln1_g"], wts["ln1_b"],
      wts["w_router"], wts["b_router"])


def _route_plan(route_i, cnt, n_tiles):
    ids = route_i[:, :TOP_K]
    rank = route_i[:, TOP_K:2 * TOP_K]
    counts = cnt[0, :N_EXPERTS].astype(I32)
    padded = (counts + EXPERT_TILE - 1) // EXPERT_TILE * EXPERT_TILE
    gend = jnp.cumsum(padded)
    slot = jnp.take(gend - padded, ids) + rank
    n_used = gend[-1] // EXPERT_TILE
    tstart = jnp.arange(n_tiles, dtype=I32) * EXPERT_TILE
    te = jnp.minimum(jnp.sum(gend[None, :] <= tstart[:, None], axis=1), N_EXPERTS - 1).astype(I32)
    te = jnp.where(tstart < gend[-1], te, te[jnp.maximum(n_used - 1, 0)])
    return slot.astype(I32), te, n_used.reshape(1).astype(I32)


def _dispatch_kernel(slot_ref, x_ref, xs_in_ref, xs_ref, sem):
    del xs_in_ref
    tm = x_ref.shape[0] // ROW_SUB

    def issue(r, _):
        src = x_ref.at[pl.ds(pl.multiple_of(ROW_SUB * r, ROW_SUB), ROW_SUB)]
        for k in range(TOP_K):
            dst_row = pl.multiple_of(slot_ref[0, 0, TOP_K * r + k], ROW_SUB)
            pltpu.make_async_copy(src, xs_ref.at[pl.ds(dst_row, ROW_SUB)], sem).start(priority=k % 2)
        return 0

    lax.fori_loop(0, tm, issue, 0, unroll=8)
    for _ in range(TOP_K):
        pltpu.make_async_copy(x_ref, xs_ref.at[pl.ds(0, tm * ROW_SUB)], sem).wait()


def _dispatch(slot3, h1p, n_slots, tm):
    n_rows = h1p.shape[0] // ROW_SUB
    return pl.pallas_call(
        _dispatch_kernel,
        grid=(n_rows // tm,),
        in_specs=[pl.BlockSpec((1, 1, TOP_K * tm), lambda i: (i, 0, 0), memory_space=pltpu.SMEM),
                  pl.BlockSpec((tm * ROW_SUB, LANES), lambda i: (i, 0)),
                  pl.BlockSpec(memory_space=pl.ANY)],
        out_specs=pl.BlockSpec(memory_space=pl.ANY),
        out_shape=jax.ShapeDtypeStruct((n_slots * ROW_SUB, LANES), U32),
        scratch_shapes=[pltpu.SemaphoreType.DMA(())],
        input_output_aliases={2: 0},
        name="moe_dispatch",
        compiler_params=_cparams(("arbitrary",)),
    )(slot3, h1p, jnp.zeros((n_slots * ROW_SUB, LANES), U32))


def _expert_kernel(te_ref, nu_ref, xs_ref, wg_ref, bg_ref, wu_ref, bu_ref, wd_ref, bd_ref, ys_ref, w16_ref):
    t = pl.program_id(0)

    @pl.when(t < nu_ref[0])
    def _():
        @pl.when((t == 0) | (te_ref[t] != te_ref[jnp.maximum(t - 1, 0)]))
        def _():
            w16_ref[0] = wg_ref[0].astype(BF16)
            w16_ref[1] = wu_ref[0].astype(BF16)
            w16_ref[2] = wd_ref[0].astype(BF16)

        x16 = _unpack_bf16_pairs(_load_rows(xs_ref)).astype(BF16)
        gate = jnp.minimum(_dot(x16, w16_ref[0]) + bg_ref[0], SWIGLU_LIMIT)
        up = jnp.clip(_dot(x16, w16_ref[1]) + bu_ref[0], -SWIGLU_LIMIT, SWIGLU_LIMIT)
        act = (up + 1.0) * gate * jax.nn.sigmoid(SWIGLU_ALPHA * gate)
        _store_rows(ys_ref, _pack_bf16_pairs(_dot(act.astype(BF16), w16_ref[2]) + bd_ref[0]))

    @pl.when(t >= nu_ref[0])
    def _():
        ys_ref[...] = jnp.zeros_like(ys_ref)


def _experts(te, n_used, xs, wts):
    n_slots = xs.shape[0] // ROW_SUB
    n_tiles = n_slots // EXPERT_TILE
    blk = (EXPERT_TILE * ROW_SUB, LANES)
    tile = lambda t, te_ref, nu_ref: (jnp.minimum(t, nu_ref[0] - 1), 0)
    wsel = lambda t, te_ref, nu_ref: (te_ref[t], 0, 0)
    grid_spec = pltpu.PrefetchScalarGridSpec(
        num_scalar_prefetch=2,
        grid=(n_tiles,),
        in_specs=[pl.BlockSpec(blk, tile),
                  pl.BlockSpec((1, D_MODEL, D_EXPERT), wsel), pl.BlockSpec((1, 1, D_EXPERT), wsel),
                  pl.BlockSpec((1, D_MODEL, D_EXPERT), wsel), pl.BlockSpec((1, 1, D_EXPERT), wsel),
                  pl.BlockSpec((1, D_EXPERT, D_MODEL), wsel), pl.BlockSpec((1, 1, D_MODEL), wsel)],
        out_specs=pl.BlockSpec(blk, lambda t, te_ref, nu_ref: (t, 0)),
        scratch_shapes=[pltpu.VMEM((3, D_MODEL, D_EXPERT), BF16)])
    return pl.pallas_call(
        _expert_kernel,
        grid_spec=grid_spec,
        out_shape=jax.ShapeDtypeStruct(xs.shape, U32),
        name="moe_experts",
        compiler_params=_cparams(("arbitrary",)),
    )(te, n_used, xs, *wts)


def _combine_kernel(slot_ref, x_ref, rw_ref, ys_ref, g_ref, b_ref, o_ref, buf_ref, sem, *, alpha, tiles_per_seq):
    tm = x_ref.shape[0]

    def issue(r, _):
        dst_row = pl.multiple_of(ROW_SUB * r, ROW_SUB)
        for k in range(TOP_K):
            src_row = pl.multiple_of(slot_ref[0, 0, TOP_K * r + k], ROW_SUB)
            pltpu.make_async_copy(ys_ref.at[pl.ds(src_row, ROW_SUB)], buf_ref.at[k, pl.ds(dst_row, ROW_SUB)],
                                  sem).start(priority=k % 2)
        return 0

    lax.fori_loop(0, tm, issue, 0, unroll=8)
    for k in range(TOP_K):
        pltpu.make_async_copy(ys_ref.at[pl.ds(0, tm * ROW_SUB)], buf_ref.at[k], sem).wait()
    rw = rw_ref[...]
    acc = None
    for k in range(TOP_K):
        term = rw[:, k:k + 1] * _unpack_bf16_pairs(_load_rows(buf_ref.at[k]))
        acc = term if acc is None else acc + term
    out = _ln_rows(alpha * x_ref[...] + acc, g_ref[...], b_ref[...])
    row = lax.broadcasted_iota(I32, out.shape, 0)
    first_tile = lax.rem(pl.program_id(0), tiles_per_seq) == 0
    o_ref[...] = jnp.where(first_tile & (row < PAD_FRONT), 0.0, out)


def _combine(slot3, h1, route_w, ys, wts, alpha, tm, tp):
    n_rows = h1.shape[0]
    assert tp % tm == 0 and tm >= PAD_FRONT
    return pl.pallas_call(
        functools.partial(_combine_kernel, alpha=alpha, tiles_per_seq=tp // tm),
        grid=(n_rows // tm,),
        in_specs=[pl.BlockSpec((1, 1, TOP_K * tm), lambda i: (i, 0, 0), memory_space=pltpu.SMEM),
                  pl.BlockSpec((tm, D_MODEL), lambda i: (i, 0)),
                  pl.BlockSpec((tm, LANES), lambda i: (i, 0)),
                  pl.BlockSpec(memory_space=pl.ANY),
                  _const_spec((1, D_MODEL)), _const_spec((1, D_MODEL))],
        out_specs=pl.BlockSpec((tm, D_MODEL), lambda i: (i, 0)),
        out_shape=jax.ShapeDtypeStruct((n_rows, D_MODEL), F32),
        scratch_shapes=[pltpu.VMEM((TOP_K, tm * ROW_SUB, LANES), U32), pltpu.SemaphoreType.DMA(())],
        name="moe_combine_ln2",
        compiler_params=_cparams(("arbitrary",)),
    )(slot3, h1, route_w, ys, wts["ln2_g"], wts["ln2_b"])


def _moe(h1, h1p, route_i, route_w, cnt, wts, expert_wts, layer, alpha, tm, tp):
    n_rows = h1.shape[0]
    n_tiles = -(-TOP_K * n_rows // EXPERT_TILE) + N_EXPERTS
    slot, te, n_used = _route_plan(route_i, cnt, n_tiles)
    slot3 = (slot * ROW_SUB).reshape(n_rows // tm, 1, TOP_K * tm)
    xs = _dispatch(slot3, h1p, n_tiles * EXPERT_TILE, tm)
    ys = _experts(te + layer * N_EXPERTS, n_used, xs, expert_wts)
    return _combine(slot3, h1, route_w, ys, wts, alpha, tm, tp)


def _rot_half_cols(w, d):
    n = w.shape[1] // d
    w3 = w.reshape(w.shape[0], n, d)
    return jnp.concatenate([-w3[..., d // 2:], w3[..., :d // 2]], -1).reshape(w.shape)


def _tables(tp):
    pos = jnp.maximum(jnp.arange(tp, dtype=I32) - PAD_FRONT, 0).astype(F32)

    def cs(d):
        inv = ROPE_BASE ** (-jnp.arange(0, d, 2, dtype=F32) / d)
        ang = pos[:, None] * inv[None, :]
        return jnp.cos(ang), jnp.sin(ang)

    rc, rs = cs(RET_DK)
    ret_cos = jnp.tile(jnp.concatenate([rc, rc], -1), (1, 2 * RET_HEADS))
    ret_sin = jnp.tile(jnp.concatenate([rs, rs], -1), (1, 2 * RET_HEADS))
    mc, ms = cs(MLA_ROPE)
    ones = jnp.ones((tp, MLA_NOPE), F32)
    tail1 = jnp.ones((tp, LANES - MLA_NOPE - MLA_ROPE), F32)
    mla_cos = jnp.concatenate([ones, mc, mc, tail1], -1)
    mla_sin = jnp.concatenate([0 * ones, ms, ms, 0 * tail1], -1)
    log_gamma = jnp.log1p(-jnp.exp2(-5.0 - jnp.arange(RET_HEADS, dtype=F32)))
    idx = jnp.arange(CHUNK, dtype=F32)
    rel = idx[:, None] - idx[None, :]
    dmat = jnp.where(rel >= 0, jnp.exp(log_gamma[:, None, None] * jnp.maximum(rel, 0.0)), 0.0)
    k_dec = jnp.exp(log_gamma[:, None] * (CHUNK - 1.0 - idx))
    q_dec = jnp.exp(log_gamma[:, None] * (idx + 1.0))
    rep = lambda t: jnp.repeat(t.T, RET_DK, axis=1)
    ret_dec = jnp.concatenate([rep(q_dec), rep(k_dec)], -1)
    ret_cdec = jnp.exp(log_gamma * CHUNK)
    return dict(ret_cos=ret_cos, ret_sin=ret_sin, mla_cos=mla_cos, mla_sin=mla_sin, ret_dmat=dmat, ret_dec=ret_dec,
                ret_cdec=ret_cdec)


def _layer_weights(w_in, w_merge, b_merge, g_cq, g_ckv, w_uq, w_ukv, conv_w, conv_b, w_lru_a, b_lru_a, w_lru_x,
                   b_lru_x, lru_lambda, w_gla_a2, b_gla_a2, w_branch, w_out, ln1_g, ln1_b, w_router, b_router,
                   ln2_g, ln2_b):
    D = D_MODEL
    z = lambda n: jnp.zeros((D, n), F32)
    rqk = jnp.concatenate([w_in[:, _O_RQ:_O_RK], w_in[:, _O_RK:_O_RV] * (RET_DK ** -0.5)], -1)
    kr = w_in[:, _O_KR:_O_LX]
    kr_slot = jnp.concatenate([z(MLA_NOPE), kr, z(LANES - MLA_NOPE - MLA_ROPE)], -1)
    kr_rot_slot = jnp.concatenate([z(MLA_NOPE), _rot_half_cols(kr, MLA_ROPE), z(LANES - MLA_NOPE - MLA_ROPE)], -1)
    gqk = jnp.concatenate([w_in[:, _O_GQ:_O_GK], w_in[:, _O_GK:_O_GV] * (GLA_DK ** -0.5)], -1)
    w16 = jnp.concatenate([rqk, _rot_half_cols(rqk, RET_DK), w_in[:, _O_RV:_O_RG],
                           w_in[:, _O_CQ:_O_KR], kr_slot, gqk, w_in[:, _O_GV:_O_GA], kr_rot_slot], -1).astype(BF16)
    w32 = jnp.concatenate([w_in[:, _O_RG:_O_CQ], w_in[:, _O_LX:_O_LG], w_in[:, _O_LG:_O_GQ], w_in[:, _O_GG:],
                           w_in[:, _O_GA:_O_GG], z(LANES - GLA_RANK)], -1).astype(BF16)
    hq = MLA_NOPE + MLA_ROPE
    uq3 = w_uq.reshape(MLA_Q_RANK, MLA_HEADS, hq)
    zq = jnp.zeros((MLA_Q_RANK, MLA_HEADS, LANES - hq), F32)
    uq = jnp.concatenate([uq3, zq], -1).reshape(MLA_Q_RANK, MLA_HEADS * LANES)
    uq_rope_rot = _rot_half_cols(uq3[..., MLA_NOPE:].reshape(MLA_Q_RANK, -1), MLA_ROPE)
    uq_rope_rot = uq_rope_rot.reshape(MLA_Q_RANK, MLA_HEADS, MLA_ROPE)
    uqr = jnp.concatenate([jnp.zeros((MLA_Q_RANK, MLA_HEADS, MLA_NOPE), F32), uq_rope_rot, zq], -1)
    uqr = uqr.reshape(MLA_Q_RANK, MLA_HEADS * LANES)
    eye = jnp.eye(LRU_BLOCKS, dtype=F32)
    bd = lambda w: jnp.einsum("ncd,nm->ncmd", w, eye).reshape(LRU_WIDTH, LRU_WIDTH)
    w_ab = jnp.concatenate([bd(w_lru_a), bd(w_lru_x)], -1).astype(BF16)
    wa2 = jnp.concatenate([w_gla_a2, jnp.zeros((LANES - GLA_RANK, GLA_HEADS * GLA_DK), F32)], 0).astype(BF16)
    wr = jnp.concatenate([w_router, jnp.zeros((D, LANES - N_EXPERTS), F32)], -1).astype(BF16)
    br = jnp.concatenate([b_router, jnp.full((LANES - N_EXPERTS,), -jnp.inf, F32)]).reshape(1, LANES)
    row = lambda v: v.reshape(1, -1).astype(F32)
    return dict(
        w16=w16, w32=w32,
        g_cq=row(g_cq), g_ckv=row(g_ckv), w_uq=uq.astype(BF16), w_uq_rot=uqr.astype(BF16), w_ukv=w_ukv.astype(BF16),
        conv_w=conv_w, conv_b=row(conv_b), w_lru_ab=w_ab, b_lru_ab=row(jnp.concatenate([b_lru_a, b_lru_x])),
        lru_ncsp=row(-LRU_C * jax.nn.softplus(-lru_lambda)),
        w_gla_a2=wa2, b_gla_a2=row(b_gla_a2),
        w_merge=w_merge.astype(BF16), b_merge=row(b_merge), w_branch=w_branch.astype(BF16), w_out=w_out.astype(BF16),
        ln1_g=row(ln1_g), ln1_b=row(ln1_b), w_router=wr, b_router=br,
        ln2_g=row(ln2_g), ln2_b=row(ln2_b))


def _row_tile(n_rows, want):
    best = CHUNK
    for t in range(CHUNK, want + 1, CHUNK):
        if n_rows % t == 0:
            best = t
    return best


def kernel(x, meta_tokens, ln0_g, ln0_b, w_in, w_merge, b_merge, g_cq, g_ckv, w_uq, w_ukv, conv_w, conv_b, w_lru_a, b_lru_a, w_lru_x, b_lru_x, lru_lambda, w_gla_a2, b_gla_a2, w_branch, w_out, ln1_g, ln1_b, w_router, b_router, w_exp_gate, b_exp_gate, w_exp_up, b_exp_up, w_exp_down, b_exp_down, ln2_g, ln2_b):
    B, S, D = x.shape
    depth = w_in.shape[0]
    alpha = (2.0 * depth) ** 0.25
    tp = S + CHUNK
    nc = tp // CHUNK
    n_rows = B * tp
    tabs = _tables(tp)
    per_layer = (w_in, w_merge, b_merge, g_cq, g_ckv, w_uq, w_ukv, conv_w, conv_b, w_lru_a, b_lru_a, w_lru_x, b_lru_x,
                 lru_lambda, w_gla_a2, b_gla_a2, w_branch, w_out, ln1_g, ln1_b, w_router, b_router, ln2_g, ln2_b)
    n_le = depth * N_EXPERTS
    expert_wts = (w_exp_gate.reshape(n_le, D, D_EXPERT), b_exp_gate.reshape(n_le, 1, D_EXPERT),
                  w_exp_up.reshape(n_le, D, D_EXPERT), b_exp_up.reshape(n_le, 1, D_EXPERT),
                  w_exp_down.reshape(n_le, D_EXPERT, D), b_exp_down.reshape(n_le, 1, D))
    tm = _row_tile(tp, 640)
    t_att = _row_tile(tp, 1664)
    h = _ln0(x, meta_tokens.astype(x.dtype), ln0_g, ln0_b, tm).reshape(n_rows, D)
    for l in range(depth):
        wts = _layer_weights(*(p[l] for p in per_layer))
        p16 = _proj(h, wts["w16"], BF16, tm)
        p32 = _proj(h, wts["w32"], F32, tm)
        y_ret = _retention(p16, p32, tabs, B, tp, tm)
        qt, kh, vt = _mla_prep(p16, tabs, wts, B, tp, tm)
        y_mla = _flash(qt, kh, vt, t_att)
        y_lru = _lru(p32, wts, B, tp, tm)
        y_gla = _gla(p16, p32, wts, B, nc)
        h1, h1p, route_i, route_w, cnt = _merge(h, (y_ret, y_mla, y_lru, y_gla), wts, alpha, tm)
        h = _moe(h1, h1p, route_i, route_w, cnt, wts, expert_wts, l, alpha, tm, tp)
    return h.reshape(B, tp, D)[:, CHUNK:]
```

```python
import functools

import jax
import jax.numpy as jnp
from jax import lax
from jax.experimental import pallas as pl
from jax.experimental.pallas import tpu as pltpu

F32 = jnp.float32
BF16 = jnp.bfloat16
U32 = jnp.uint32
I32 = jnp.int32

D_MODEL = 1024
N_META = 16
CHUNK = 128
PAD_FRONT = CHUNK - N_META
ROPE_BASE = 10000.0
NEG_BIG = -1e30

RET_HEADS, RET_DK, RET_DV = 4, 64, 128
MLA_HEADS, MLA_Q_RANK, MLA_KV_RANK, MLA_NOPE, MLA_ROPE, MLA_DV = 8, 256, 128, 64, 32, 64
LRU_WIDTH, LRU_BLOCKS, CONV_W, LRU_C = 512, 8, 4, 8.0
LRU_BLOCK = LRU_WIDTH // LRU_BLOCKS
GLA_HEADS, GLA_DK, GLA_DV, GLA_RANK, GLA_TAU, GLA_CHUNK = 4, 64, 128, 16, 16.0, 16
N_BRANCH, BRANCH_W = 4, 512
N_EXPERTS, TOP_K, D_EXPERT = 32, 4, 1024
SWIGLU_LIMIT, SWIGLU_ALPHA = 7.0, 1.702

LANES = 128
VMEM_LIMIT = 56 * 1024 * 1024
EXPERT_TILE = 512

_O_RQ, _O_RK, _O_RV, _O_RG = 0, 256, 512, 1024
_O_CQ, _O_CKV, _O_KR = 1536, 1792, 1920
_O_LX, _O_LG = 1952, 2464
_O_GQ, _O_GK, _O_GV, _O_GA, _O_GG = 2976, 3232, 3488, 4000, 4016


def _cparams(sem):
    return pltpu.CompilerParams(dimension_semantics=sem, vmem_limit_bytes=VMEM_LIMIT)


def _const_spec(shape):
    nd = len(shape)
    return pl.BlockSpec(shape, lambda *_: (0,) * nd, pipeline_mode=pl.Buffered(1))


def _ln_rows(x, g, b, eps=1e-5):
    mu = jnp.mean(x, -1, keepdims=True)
    xc = x - mu
    var = jnp.mean(xc * xc, -1, keepdims=True)
    return xc * lax.rsqrt(var + eps) * g + b


def _dot(a, b):
    return jnp.dot(a, b, preferred_element_type=F32)


def _dot_nt(a, b):
    return lax.dot_general(a, b, (((1,), (1,)), ((), ())), preferred_element_type=F32)


def _pack_bf16_pairs(x):
    n = x.shape[1] // 2
    bits = lax.bitcast_convert_type(x.astype(BF16).astype(F32), U32)
    return (bits[:, :n] & U32(0xFFFF0000)) | (bits[:, n:] >> 16)


def _unpack_bf16_pairs(w):
    hi = lax.bitcast_convert_type(w & U32(0xFFFF0000), F32)
    lo = lax.bitcast_convert_type(w << 16, F32)
    return jnp.concatenate([hi, lo], axis=1)


ROW_SUB = D_MODEL // 2 // LANES


def _store_rows(ref, words):
    rows = words.shape[0]
    for j in range(ROW_SUB):
        ref[pl.ds(j, rows, stride=ROW_SUB), :] = words[:, j * LANES:(j + 1) * LANES]


def _load_rows(ref):
    rows = ref.shape[0] // ROW_SUB
    return jnp.concatenate([ref[pl.ds(j, rows, stride=ROW_SUB), :] for j in range(ROW_SUB)], axis=1)


def _ln0_kernel(*refs):
    *x_refs, meta_ref, g_ref, b_ref, o_ref = refs
    i = pl.program_id(1)
    g = g_ref[...]
    b = b_ref[...]
    for j, x_ref in enumerate(x_refs):
        rows = slice(j * CHUNK, (j + 1) * CHUNK)
        if j == 0:
            @pl.when(i == 0)
            def _():
                o_ref[0, :PAD_FRONT, :] = jnp.zeros((PAD_FRONT, D_MODEL), F32)
                o_ref[0, PAD_FRONT:CHUNK, :] = _ln_rows(meta_ref[...], g, b)

            @pl.when(i > 0)
            def _():
                o_ref[0, rows, :] = _ln_rows(x_ref[0], g, b)
        else:
            o_ref[0, rows, :] = _ln_rows(x_ref[0], g, b)


def _ln0(x, meta, g, b, tr):
    B, S, D = x.shape
    tp = S + CHUNK
    n = tr // CHUNK
    xspec = lambda j: pl.BlockSpec((1, CHUNK, D), lambda bb, i: (bb, jnp.maximum(n * i + j - 1, 0), 0))
    return pl.pallas_call(
        _ln0_kernel,
        grid=(B, tp // tr),
        in_specs=[xspec(j) for j in range(n)] + [pl.BlockSpec((N_META, D), lambda bb, i: (0, 0)),
                                                   pl.BlockSpec((1, D), lambda bb, i: (0, 0)),
                                                   pl.BlockSpec((1, D), lambda bb, i: (0, 0))],
        out_specs=pl.BlockSpec((1, tr, D), lambda bb, i: (bb, i, 0)),
        out_shape=jax.ShapeDtypeStruct((B, tp, D), F32),
        name="ln0",
        compiler_params=_cparams(("parallel", "arbitrary")),
    )(*([x] * n), meta, g.reshape(1, D), b.reshape(1, D))


def _proj_kernel(h_ref, wa_ref, wb_ref, oa_ref, ob_ref, *, col_step):
    h = h_ref[...].astype(BF16)
    for w_ref, o_ref in ((wa_ref, oa_ref), (wb_ref, ob_ref)):
        n = w_ref.shape[1]
        for c0 in range(0, n, col_step):
            c1 = min(c0 + col_step, n)
            o_ref[:, c0:c1] = _dot(h, w_ref[:, c0:c1]).astype(o_ref.dtype)


def _proj(h, w16, w32, tm):
    n_rows, d = h.shape
    na, nb = w16.shape[1], w32.shape[1]
    return pl.pallas_call(
        functools.partial(_proj_kernel, col_step=512),
        grid=(n_rows // tm,),
        in_specs=[pl.BlockSpec((tm, d), lambda i: (i, 0)), _const_spec((d, na)), _const_spec((d, nb))],
        out_specs=[pl.BlockSpec((tm, na), lambda i: (i, 0)), pl.BlockSpec((tm, nb), lambda i: (i, 0))],
        out_shape=[jax.ShapeDtypeStruct((n_rows, na), BF16), jax.ShapeDtypeStruct((n_rows, nb), F32)],
        name="in_proj",
        compiler_params=_cparams(("parallel",)),
    )(h, w16, w32)


def _half_masks(dtype):
    lane = lax.broadcasted_iota(I32, (1, LANES), 1)
    lo = (lane < LANES // 2).astype(dtype)
    return lo, (1 - lo).astype(dtype)


def _ret_kernel(cdec_ref, qk_ref, rot_ref, v_ref, g_ref, cos_ref, sin_ref, dec_ref, dmat_ref, o_ref, s_ref):
    c = pl.program_id(1)

    @pl.when(c == 0)
    def _():
        s_ref[...] = jnp.zeros_like(s_ref)

    r_all = qk_ref[...].astype(F32) * cos_ref[...] + rot_ref[...].astype(F32) * sin_ref[...]
    masks = _half_masks(F32)
    states = [s_ref[hd] for hd in range(RET_HEADS)]
    for ch in range(qk_ref.shape[0] // CHUNK):
        rows = slice(ch * CHUNK, (ch + 1) * CHUNK)
        r = r_all[rows]
        rd = r * dec_ref[...]
        for hd in range(RET_HEADS):
            pair, half = divmod(hd, 2)
            m = masks[half]
            q_lo, k_lo = pair * LANES, 2 * LANES + pair * LANES
            qp = (r[:, q_lo:q_lo + LANES] * m).astype(BF16)
            kp = r[:, k_lo:k_lo + LANES].astype(BF16)
            vh = v_ref[rows, hd * RET_DV:(hd + 1) * RET_DV]
            scores = _dot_nt(qp, kp) * dmat_ref[hd]
            o = _dot(scores.astype(BF16), vh)
            qin = (rd[:, q_lo:q_lo + LANES] * m).astype(BF16)
            o = o + _dot(qin, states[hd].astype(BF16))
            kdec_t = (rd[:, k_lo:k_lo + LANES] * m).T.astype(BF16)
            states[hd] = states[hd] * cdec_ref[hd] + _dot(kdec_t, vh)
            mu = jnp.mean(o, -1, keepdims=True)
            oc = o - mu
            var = jnp.mean(oc * oc, -1, keepdims=True)
            on = oc * lax.rsqrt(var + 1e-5)
            gate = g_ref[rows, hd * RET_DV:(hd + 1) * RET_DV]
            o_ref[rows, hd * RET_DV:(hd + 1) * RET_DV] = (gate * jax.nn.sigmoid(gate) * on).astype(o_ref.dtype)
    for hd in range(RET_HEADS):
        s_ref[hd] = states[hd]


def _retention(p16, p32, tabs, B, tp, tr):
    n_rows = p16.shape[0]
    nc = tp // tr
    row = lambda b, c: b * nc + c
    return pl.pallas_call(
        _ret_kernel,
        grid=(B, nc),
        in_specs=[pl.BlockSpec(memory_space=pltpu.SMEM),
                  pl.BlockSpec((tr, 512), lambda b, c: (row(b, c), 0)),
                  pl.BlockSpec((tr, 512), lambda b, c: (row(b, c), 1)),
                  pl.BlockSpec((tr, 512), lambda b, c: (row(b, c), 2)),
                  pl.BlockSpec((tr, 512), lambda b, c: (row(b, c), 0)),
                  pl.BlockSpec((tr, 512), lambda b, c: (c, 0)),
                  pl.BlockSpec((tr, 512), lambda b, c: (c, 0)),
                  _const_spec((CHUNK, 512)),
                  _const_spec((RET_HEADS, CHUNK, CHUNK))],
        out_specs=pl.BlockSpec((tr, 512), lambda b, c: (row(b, c), 0)),
        out_shape=jax.ShapeDtypeStruct((n_rows, 512), BF16),
        scratch_shapes=[pltpu.VMEM((RET_HEADS, LANES, RET_DV), F32)],
        name="retention",
        compiler_params=_cparams(("parallel", "arbitrary")),
    )(tabs["ret_cdec"], p16, p16, p16, p32, tabs["ret_cos"], tabs["ret_sin"], tabs["ret_dec"], tabs["ret_dmat"])


MLA_BIAS_LANE = MLA_NOPE + MLA_ROPE
M_INIT = -3.0e38
LOG2_E = 1.4426950408889634


def _mla_prep_kernel(lat_ref, krot_ref, cos_ref, sin_ref, gq_ref, gkv_ref, wq_ref, wqr_ref, wkv_ref,
                     qt_ref, k_ref, vt_ref):
    i = pl.program_id(1)
    tr = lat_ref.shape[0]
    lat = lat_ref[...].astype(F32)
    cq = lat[:, :MLA_Q_RANK]
    ckv = lat[:, MLA_Q_RANK:MLA_Q_RANK + MLA_KV_RANK]
    kr = lat[:, MLA_Q_RANK + MLA_KV_RANK:]
    cos = cos_ref[...]
    sin = sin_ref[...]
    cqn = (cq * lax.rsqrt(jnp.mean(cq * cq, -1, keepdims=True) + 1e-6) * gq_ref[...]).astype(BF16)
    ckvn = (ckv * lax.rsqrt(jnp.mean(ckv * ckv, -1, keepdims=True) + 1e-6) * gkv_ref[...]).astype(BF16)
    krr = kr * cos + krot_ref[...].astype(F32) * sin
    scale = (MLA_NOPE + MLA_ROPE) ** -0.5 * LOG2_E
    lane = lax.broadcasted_iota(I32, (tr, LANES), 1)
    row = lax.broadcasted_iota(I32, (tr, LANES), 0)
    is_nope = lane < MLA_NOPE
    is_bias = lane == MLA_BIAS_LANE
    kbias = jnp.where(is_bias & (i == 0) & (row < PAD_FRONT), NEG_BIG, 0.0)
    for hd in range(MLA_HEADS):
        sl = slice(hd * LANES, (hd + 1) * LANES)
        q = (_dot(cqn, wq_ref[:, sl]) * cos + _dot(cqn, wqr_ref[:, sl]) * sin) * scale
        qt_ref[0, hd] = jnp.where(is_bias, 1.0, q).T.astype(qt_ref.dtype)
        kv = _dot(ckvn, wkv_ref[:, sl])
        k_ref[0, hd] = (jnp.where(is_nope, kv, krr) + kbias).astype(k_ref.dtype)
        vt_ref[0, hd] = kv.T[MLA_NOPE:, :].astype(vt_ref.dtype)


def _mla_prep(p16, tabs, wts, B, tp, tr):
    nt = tp // tr
    assert tr >= PAD_FRONT
    row = lambda b, i: b * nt + i
    hm = jax.ShapeDtypeStruct((B, MLA_HEADS, tp, LANES), BF16)
    hm_t = jax.ShapeDtypeStruct((B, MLA_HEADS, LANES, tp), BF16)
    hm_v = jax.ShapeDtypeStruct((B, MLA_HEADS, MLA_DV, tp), BF16)
    hspec = pl.BlockSpec((1, MLA_HEADS, tr, LANES), lambda b, i: (b, 0, i, 0))
    hspec_t = pl.BlockSpec((1, MLA_HEADS, LANES, tr), lambda b, i: (b, 0, 0, i))
    hspec_v = pl.BlockSpec((1, MLA_HEADS, MLA_DV, tr), lambda b, i: (b, 0, 0, i))
    return pl.pallas_call(
        _mla_prep_kernel,
        grid=(B, nt),
        in_specs=[pl.BlockSpec((tr, 512), lambda b, i: (row(b, i), 3)),
                  pl.BlockSpec((tr, LANES), lambda b, i: (row(b, i), 3072 // LANES)),
                  pl.BlockSpec((tr, LANES), lambda b, i: (i, 0)),
                  pl.BlockSpec((tr, LANES), lambda b, i: (i, 0)),
                  _const_spec((1, MLA_Q_RANK)), _const_spec((1, MLA_KV_RANK)),
                  _const_spec((MLA_Q_RANK, MLA_HEADS * LANES)), _const_spec((MLA_Q_RANK, MLA_HEADS * LANES)),
                  _const_spec((MLA_KV_RANK, MLA_HEADS * LANES))],
        out_specs=[hspec_t, hspec, hspec_v],
        out_shape=[hm_t, hm, hm_v],
        name="mla_prep",
        compiler_params=_cparams(("parallel", "arbitrary")),
    )(p16, p16, tabs["mla_cos"], tabs["mla_sin"], wts["g_cq"], wts["g_ckv"], wts["w_uq"], wts["w_uq_rot"],
      wts["w_ukv"])


def _flash_kernel(qt_ref, k_ref, vt_ref, o_ref, *, tb):
    tp = k_ref.shape[2]
    nb = tp // tb
    ng = tb // LANES
    for hh in range(2):
        def step(qts, j, stats, diag):
            k0 = pl.multiple_of(j * tb, tb)
            new = []
            for g in range(ng):
                m, l, acc = stats[g]
                nk = (g + 1) * LANES if diag else tb
                k = k_ref[0, hh, pl.ds(k0, nk), :]
                vt = vt_ref[0, hh, :, pl.ds(k0, nk)]
                vt = jnp.concatenate([vt, jnp.zeros_like(vt)], axis=0)
                s = _dot(k, qts[g])
                if diag:
                    key_i = lax.broadcasted_iota(I32, (nk, LANES), 0)
                    qry_i = lax.broadcasted_iota(I32, (nk, LANES), 1)
                    s = jnp.where(key_i <= qry_i + g * LANES, s, NEG_BIG)
                m_new = jnp.maximum(m, jnp.max(s, 0, keepdims=True))
                alpha = jnp.exp2(m - m_new)
                p = jnp.exp2(s - m_new)
                l_new = alpha * l + jnp.sum(p, 0, keepdims=True)
                acc_new = alpha * acc + _dot(vt, p.astype(BF16))
                new.append((m_new, l_new, acc_new))
            return tuple(new)

        def q_block(i, _):
            q0 = i * tb
            qts = [qt_ref[0, hh, :, pl.ds(pl.multiple_of(q0 + g * LANES, LANES), LANES)] for g in range(ng)]
            init = tuple((jnp.full((1, LANES), M_INIT, F32), jnp.zeros((1, LANES), F32),
                          jnp.zeros((LANES, LANES), F32)) for _ in range(ng))
            carry = lax.fori_loop(0, i, lambda j, c: step(qts, j, c, False), init)
            carry = step(qts, i, carry, True)
            for g in range(ng):
                m, l, acc = carry[g]
                rows = pl.ds(pl.multiple_of(q0 + g * LANES, LANES), LANES)
                o_ref[rows, hh * MLA_DV:(hh + 1) * MLA_DV] = (acc / l).T[:, :MLA_DV]
            return 0

        lax.fori_loop(0, nb, q_block, 0)


def _flash(qt, kh, vt, tb):
    B, H, tp, _ = kh.shape
    npair = H // 2
    spec = pl.BlockSpec((1, 2, tp, LANES), lambda b, j: (b, j, 0, 0))
    spec_t = pl.BlockSpec((1, 2, LANES, tp), lambda b, j: (b, j, 0, 0))
    spec_v = pl.BlockSpec((1, 2, MLA_DV, tp), lambda b, j: (b, j, 0, 0))
    return pl.pallas_call(
        functools.partial(_flash_kernel, tb=tb),
        grid=(B, npair),
        in_specs=[spec_t, spec, spec_v],
        out_specs=pl.BlockSpec((tp, LANES), lambda b, j: (b, j)),
        out_shape=jax.ShapeDtypeStruct((B * tp, npair * LANES), F32),
        name="mla_flash",
        compiler_params=_cparams(("parallel", "arbitrary")),
    )(qt, kh, vt)


def _lru_kernel(x_ref, g_ref, cw_ref, cb_ref, wab_ref, bab_ref, ncsp_ref, o_ref, hist_ref, h_ref):
    c = pl.program_id(1)
    tr = x_ref.shape[0]

    @pl.when(c == 0)
    def _():
        hist_ref[...] = jnp.zeros_like(hist_ref)
        h_ref[...] = jnp.zeros_like(h_ref)

    x = x_ref[...]
    xcat = jnp.concatenate([hist_ref[...], x], axis=0)
    hist_ref[...] = x[tr - 8:, :]
    xc = cb_ref[...] + cw_ref[CONV_W - 1:CONV_W, :] * x
    for j in range(CONV_W - 1):
        off = 8 - (CONV_W - 1) + j
        xc = xc + cw_ref[j:j + 1, :] * xcat[off:off + tr, :]
    z = _dot(xc.astype(BF16), wab_ref[...]) + bab_ref[...]
    r = jax.nn.sigmoid(z[:, :LRU_WIDTH])
    i = jax.nn.sigmoid(z[:, LRU_WIDTH:])
    log_a = ncsp_ref[...] * r
    a_all = jnp.exp(log_a)
    u_all = jnp.sqrt(1.0 - jnp.exp(2.0 * log_a)) * (i * xc)
    row = lax.broadcasted_iota(I32, (CHUNK, LRU_WIDTH), 0)
    h = h_ref[...]
    for sub in range(tr // CHUNK):
        rows = slice(sub * CHUNK, (sub + 1) * CHUNK)
        a = a_all[rows]
        u = u_all[rows]
        if sub == 0:
            u = jnp.where((c == 0) & (row < PAD_FRONT), 0.0, u)
        s = 1
        while s < CHUNK:
            keep = row >= s
            a_sh = jnp.where(keep, pltpu.roll(a, s, axis=0), 1.0)
            u_sh = jnp.where(keep, pltpu.roll(u, s, axis=0), 0.0)
            u = a * u_sh + u
            a = a * a_sh
            s *= 2
        hs = u + a * h
        h = hs[CHUNK - 1:CHUNK, :]
        o_ref[rows, :] = (hs * jax.nn.gelu(g_ref[rows, :])).astype(o_ref.dtype)
    h_ref[...] = h


def _lru(p32, wts, B, tp, tr):
    n_rows = p32.shape[0]
    nc = tp // tr
    row = lambda b, c: b * nc + c
    return pl.pallas_call(
        _lru_kernel,
        grid=(B, nc),
        in_specs=[pl.BlockSpec((tr, 512), lambda b, c: (row(b, c), 1)),
                  pl.BlockSpec((tr, 512), lambda b, c: (row(b, c), 2)),
                  _const_spec((CONV_W, LRU_WIDTH)), _const_spec((1, LRU_WIDTH)),
                  _const_spec((LRU_WIDTH, 2 * LRU_WIDTH)), _const_spec((1, 2 * LRU_WIDTH)),
                  _const_spec((1, LRU_WIDTH))],
        out_specs=pl.BlockSpec((tr, 512), lambda b, c: (row(b, c), 0)),
        out_shape=jax.ShapeDtypeStruct((n_rows, 512), BF16),
        scratch_shapes=[pltpu.VMEM((8, LRU_WIDTH), F32), pltpu.VMEM((1, LRU_WIDTH), F32)],
        name="rglru",
        compiler_params=_cparams(("parallel", "arbitrary")),
    )(p32, p32, wts["conv_w"], wts["conv_b"], wts["w_lru_ab"], wts["b_lru_ab"], wts["lru_ncsp"])


GLA_LEVELS = (64, 32, 16, 8, 4, 2, 1)


def _gla_kernel(qk_ref, v_ref, a_ref, g_ref, wa2_ref, ba2_ref, o_ref, st_ref):
    c = pl.program_id(1)
    tr = qk_ref.shape[0]
    hk = GLA_HEADS * GLA_DK

    @pl.when(c == 0)
    def _():
        st_ref[...] = jnp.zeros_like(st_ref)

    x = _dot(a_ref[...].astype(BF16), wa2_ref[...]) + ba2_ref[...]
    la = (jnp.minimum(x, 0.0) - jnp.log(1.0 + jnp.exp(-jnp.abs(x)))) * (1.0 / GLA_TAU)
    ri = lax.broadcasted_iota(I32, (tr, tr), 0)
    ci = lax.broadcasted_iota(I32, (tr, tr), 1)
    tri = jnp.where(ci <= ri, 1.0, 0.0).astype(BF16)
    la_hi = la.astype(BF16)
    rem = la - la_hi.astype(F32)
    la_mid = rem.astype(BF16)
    la_lo = (rem - la_mid.astype(F32)).astype(BF16)
    gcum = _dot(tri, la_hi) + _dot(tri, la_mid) + _dot(tri, la_lo)
    row = lax.broadcasted_iota(I32, (tr, hk), 0)
    own_end = {1: gcum}
    s = 1
    while s < GLA_LEVELS[0]:
        in_right = ((row >> (s.bit_length() - 1)) & 1) == 1
        own_end[2 * s] = jnp.where(in_right, own_end[s], pltpu.roll(own_end[s], tr - s, axis=0))
        s *= 2
    qk = qk_ref[...].astype(F32)
    q = qk[:, :hk]
    k = qk[:, hk:]
    terms = [(q.astype(BF16), k.astype(BF16), ri == ci)]
    for s in GLA_LEVELS:
        r_own = own_end[s]
        r_prev = pltpu.roll(r_own, s, axis=0)
        qs = (q * jnp.exp(jnp.minimum(gcum - r_prev, 0.0))).astype(BF16)
        ks = (k * jnp.exp(jnp.minimum(r_own - gcum, 0.0))).astype(BF16)
        sh = s.bit_length() - 1
        bi = ri >> sh
        terms.append((qs, ks, ((bi & 1) == 1) & ((ci >> sh) == bi - 1)))
    gend = gcum[tr - 1:tr, :]
    qg = (q * jnp.exp(gcum)).astype(BF16)
    kg = (k * jnp.exp(gend - gcum)).astype(BF16)
    dec = jnp.exp(gend)
    masks = _half_masks(BF16)
    for hd in range(GLA_HEADS):
        pair, half = divmod(hd, 2)
        lanes = slice(pair * LANES, (pair + 1) * LANES)
        m = masks[half]
        amat = None
        for qs, ks, keep in terms:
            term = jnp.where(keep, _dot_nt(qs[:, lanes] * m, ks[:, lanes]), 0.0)
            amat = term if amat is None else amat + term
        vh = v_ref[:, hd * GLA_DV:(hd + 1) * GLA_DV]
        st = st_ref[hd]
        o = _dot(amat.astype(BF16), vh) + _dot_nt(qg[:, lanes] * m, st.astype(BF16))
        st_ref[hd] = st * dec[:, lanes] + _dot(vh.astype(F32).T.astype(BF16), kg[:, lanes])
        on = o * lax.rsqrt(jnp.mean(o * o, -1, keepdims=True) + 1e-6)
        gate = g_ref[:, hd * GLA_DV:(hd + 1) * GLA_DV]
        o_ref[:, hd * GLA_DV:(hd + 1) * GLA_DV] = (gate * jax.nn.sigmoid(gate) * on).astype(o_ref.dtype)


def _gla(p16, p32, wts, B, nc):
    n_rows = p16.shape[0]
    row = lambda b, c: b * nc + c
    return pl.pallas_call(
        _gla_kernel,
        grid=(B, nc),
        in_specs=[pl.BlockSpec((CHUNK, 512), lambda b, c: (row(b, c), 4)),
                  pl.BlockSpec((CHUNK, 512), lambda b, c: (row(b, c), 5)),
                  pl.BlockSpec((CHUNK, LANES), lambda b, c: (row(b, c), 2048 // LANES)),
                  pl.BlockSpec((CHUNK, 512), lambda b, c: (row(b, c), 3)),
                  _const_spec((LANES, GLA_HEADS * GLA_DK)), _const_spec((1, GLA_HEADS * GLA_DK))],
        out_specs=pl.BlockSpec((CHUNK, 512), lambda b, c: (row(b, c), 0)),
        out_shape=jax.ShapeDtypeStruct((n_rows, 512), BF16),
        scratch_shapes=[pltpu.VMEM((GLA_HEADS, GLA_DV, LANES), F32)],
        name="gla",
        compiler_params=_cparams(("parallel", "arbitrary")),
    )(p16, p16, p32, p32, wts["w_gla_a2"], wts["b_gla_a2"])


def _merge_kernel(h_ref, y0_ref, y1_ref, y2_ref, y3_ref, wm_ref, bm_ref, wb_ref, wo_ref, g_ref, b_ref, wr_ref, br_ref,
                  h1_ref, h1p_ref, ri_ref, rw_ref, cnt_out_ref, cnt_ref, *, alpha):
    @pl.when(pl.program_id(0) == 0)
    def _():
        cnt_ref[...] = jnp.zeros_like(cnt_ref)

    h = h_ref[...]
    tm = h.shape[0]
    h16 = h.astype(BF16)
    ys = (y0_ref[...], y1_ref[...].astype(BF16), y2_ref[...], y3_ref[...])
    mixed = None
    for n in range(N_BRANCH):
        sl = slice(n * D_MODEL, (n + 1) * D_MODEL)
        gate = jax.nn.sigmoid(_dot(h16, wm_ref[:, sl]) + bm_ref[:, sl])
        term = gate * _dot(ys[n], wb_ref[n])
        mixed = term if mixed is None else mixed + term
    mix = _dot(mixed.astype(BF16), wo_ref[...])
    h1 = _ln_rows(alpha * h + mix, g_ref[...], b_ref[...])
    h1_ref[...] = h1
    _store_rows(h1p_ref, _pack_bf16_pairs(h1))
    logits = _dot(h1.astype(BF16), wr_ref[...]) + br_ref[...]
    lane = lax.broadcasted_iota(I32, logits.shape, 1)
    vals, sels, idxs = [], [], []
    cur = logits
    for _ in range(TOP_K):
        mx = jnp.max(cur, -1, keepdims=True)
        idx = jnp.min(jnp.where(cur == mx, lane, LANES), -1, keepdims=True)
        sel = lane == idx
        vals.append(mx)
        sels.append(sel)
        idxs.append(idx)
        cur = jnp.where(sel, -jnp.inf, cur)
    es = [jnp.exp(v - vals[0]) for v in vals]
    den = es[0] + es[1] + es[2] + es[3]
    onehot = jnp.zeros_like(logits)
    for sel in sels:
        onehot = onehot + jnp.where(sel, 1.0, 0.0)
    rr = lax.broadcasted_iota(I32, (tm, tm), 0)
    cc = lax.broadcasted_iota(I32, (tm, tm), 1)
    lower = jnp.where(cc < rr, 1.0, 0.0).astype(BF16)
    base = cnt_ref[...] + _dot(lower, onehot.astype(BF16))
    route_i = jnp.zeros(logits.shape, I32)
    route_w = jnp.zeros_like(logits)
    for k in range(TOP_K):
        rank = jnp.sum(jnp.where(sels[k], base, 0.0), -1, keepdims=True).astype(I32)
        route_i = jnp.where(lane == k, idxs[k], route_i)
        route_i = jnp.where(lane == TOP_K + k, rank, route_i)
        route_w = jnp.where(lane == k, es[k] / den, route_w)
    ri_ref[...] = route_i
    rw_ref[...] = route_w
    cnt_ref[...] = cnt_ref[...] + jnp.sum(onehot, 0, keepdims=True)
    cnt_out_ref[...] = jnp.broadcast_to(cnt_ref[...], cnt_out_ref.shape)


def _merge(h, ys, wts, alpha, tm):
    n_rows = h.shape[0]
    rspec = lambda w: pl.BlockSpec((tm, w), lambda i: (i, 0))
    return pl.pallas_call(
        functools.partial(_merge_kernel, alpha=alpha),
        grid=(n_rows // tm,),
        in_specs=[rspec(D_MODEL), rspec(512), rspec(512), rspec(512), rspec(512),
                  _const_spec((D_MODEL, N_BRANCH * D_MODEL)), _const_spec((1, N_BRANCH * D_MODEL)),
                  _const_spec((N_BRANCH, BRANCH_W, D_MODEL)), _const_spec((D_MODEL, D_MODEL)),
                  _const_spec((1, D_MODEL)), _const_spec((1, D_MODEL)),
                  _const_spec((D_MODEL, LANES)), _const_spec((1, LANES))],
        out_specs=[rspec(D_MODEL), pl.BlockSpec((tm * ROW_SUB, LANES), lambda i: (i, 0)), rspec(LANES), rspec(LANES),
                   pl.BlockSpec((8, LANES), lambda i: (0, 0))],
        out_shape=[jax.ShapeDtypeStruct((n_rows, D_MODEL), F32), jax.ShapeDtypeStruct((n_rows * ROW_SUB, LANES), U32),
                   jax.ShapeDtypeStruct((n_rows, LANES), I32), jax.ShapeDtypeStruct((n_rows, LANES), F32),
                   jax.ShapeDtypeStruct((8, LANES), F32)],
        scratch_shapes=[pltpu.VMEM((1, LANES), F32)],
        name="merge_ln1_router",
        compiler_params=_cparams(("arbitrary",)),
    )(h, *ys, wts["w_merge"], wts["b_merge"], wts["w_branch"], wts["w_out"], wts["ln1_g"], wts["ln1_b"],
      wts["w_router"], wts["b_router"])


def _route_plan(route_i, cnt, n_tiles):
    ids = route_i[:, :TOP_K]
    rank = route_i[:, TOP_K:2 * TOP_K]
    counts = cnt[0, :N_EXPERTS].astype(I32)
    padded = (counts + EXPERT_TILE - 1) // EXPERT_TILE * EXPERT_TILE
    gend = jnp.cumsum(padded)
    slot = jnp.take(gend - padded, ids) + rank
    n_used = gend[-1] // EXPERT_TILE
    tstart = jnp.arange(n_tiles, dtype=I32) * EXPERT_TILE
    te = jnp.minimum(jnp.sum(gend[None, :] <= tstart[:, None], axis=1), N_EXPERTS - 1).astype(I32)
    te = jnp.where(tstart < gend[-1], te, te[jnp.maximum(n_used - 1, 0)])
    return slot.astype(I32), te, n_used.reshape(1).astype(I32)


def _dispatch_kernel(slot_ref, x_ref, xs_in_ref, xs_ref, sem):
    del xs_in_ref
    tm = x_ref.shape[0] // ROW_SUB

    def issue(r, _):
        src = x_ref.at[pl.ds(pl.multiple_of(ROW_SUB * r, ROW_SUB), ROW_SUB)]
        for k in range(TOP_K):
            dst_row = pl.multiple_of(slot_ref[0, 0, TOP_K * r + k], ROW_SUB)
            pltpu.make_async_copy(src, xs_ref.at[pl.ds(dst_row, ROW_SUB)], sem).start(priority=k % 2)
        return 0

    lax.fori_loop(0, tm, issue, 0, unroll=8)
    for _ in range(TOP_K):
        pltpu.make_async_copy(x_ref, xs_ref.at[pl.ds(0, tm * ROW_SUB)], sem).wait()


def _dispatch(slot3, h1p, xs_init, tm):
    n_rows = h1p.shape[0] // ROW_SUB
    return pl.pallas_call(
        _dispatch_kernel,
        grid=(n_rows // tm,),
        in_specs=[pl.BlockSpec((1, 1, TOP_K * tm), lambda i: (i, 0, 0), memory_space=pltpu.SMEM),
                  pl.BlockSpec((tm * ROW_SUB, LANES), lambda i: (i, 0)),
                  pl.BlockSpec(memory_space=pl.ANY)],
        out_specs=pl.BlockSpec(memory_space=pl.ANY),
        out_shape=jax.ShapeDtypeStruct(xs_init.shape, U32),
        scratch_shapes=[pltpu.SemaphoreType.DMA(())],
        input_output_aliases={2: 0},
        name="moe_dispatch",
        compiler_params=_cparams(("arbitrary",)),
    )(slot3, h1p, xs_init)


def _expert_kernel(te_ref, nu_ref, xs_ref, wg_ref, bg_ref, wu_ref, bu_ref, wd_ref, bd_ref, ys_ref, w16_ref):
    t = pl.program_id(0)

    @pl.when(t < nu_ref[0])
    def _():
        @pl.when((t == 0) | (te_ref[t] != te_ref[jnp.maximum(t - 1, 0)]))
        def _():
            w16_ref[0] = wg_ref[0].astype(BF16)
            w16_ref[1] = wu_ref[0].astype(BF16)
            w16_ref[2] = wd_ref[0].astype(BF16)

        x16 = _unpack_bf16_pairs(_load_rows(xs_ref)).astype(BF16)
        gate = jnp.minimum(_dot(x16, w16_ref[0]) + bg_ref[0], SWIGLU_LIMIT)
        up = jnp.clip(_dot(x16, w16_ref[1]) + bu_ref[0], -SWIGLU_LIMIT, SWIGLU_LIMIT)
        act = (up + 1.0) * gate * jax.nn.sigmoid(SWIGLU_ALPHA * gate)
        _store_rows(ys_ref, _pack_bf16_pairs(_dot(act.astype(BF16), w16_ref[2]) + bd_ref[0]))

    @pl.when(t >= nu_ref[0])
    def _():
        ys_ref[...] = jnp.zeros_like(ys_ref)


def _experts(te, n_used, xs, wts):
    n_slots = xs.shape[0] // ROW_SUB
    n_tiles = n_slots // EXPERT_TILE
    blk = (EXPERT_TILE * ROW_SUB, LANES)
    tile = lambda t, te_ref, nu_ref: (jnp.minimum(t, nu_ref[0] - 1), 0)
    wsel = lambda t, te_ref, nu_ref: (te_ref[t], 0, 0)
    grid_spec = pltpu.PrefetchScalarGridSpec(
        num_scalar_prefetch=2,
        grid=(n_tiles,),
        in_specs=[pl.BlockSpec(blk, tile),
                  pl.BlockSpec((1, D_MODEL, D_EXPERT), wsel), pl.BlockSpec((1, 1, D_EXPERT), wsel),
                  pl.BlockSpec((1, D_MODEL, D_EXPERT), wsel), pl.BlockSpec((1, 1, D_EXPERT), wsel),
                  pl.BlockSpec((1, D_EXPERT, D_MODEL), wsel), pl.BlockSpec((1, 1, D_MODEL), wsel)],
        out_specs=pl.BlockSpec(blk, lambda t, te_ref, nu_ref: (t, 0)),
        scratch_shapes=[pltpu.VMEM((3, D_MODEL, D_EXPERT), BF16)])
    return pl.pallas_call(
        _expert_kernel,
        grid_spec=grid_spec,
        out_shape=jax.ShapeDtypeStruct(xs.shape, U32),
        name="moe_experts",
        compiler_params=_cparams(("arbitrary",)),
    )(te, n_used, xs, *wts)


def _combine_kernel(slot_ref, x_ref, rw_ref, ys_ref, g_ref, b_ref, o_ref, buf_ref, sem, *, alpha, tiles_per_seq):
    tm = x_ref.shape[0]

    def issue(r, _):
        dst_row = pl.multiple_of(ROW_SUB * r, ROW_SUB)
        for k in range(TOP_K):
            src_row = pl.multiple_of(slot_ref[0, 0, TOP_K * r + k], ROW_SUB)
            pltpu.make_async_copy(ys_ref.at[pl.ds(src_row, ROW_SUB)], buf_ref.at[k, pl.ds(dst_row, ROW_SUB)],
                                  sem).start(priority=k % 2)
        return 0

    lax.fori_loop(0, tm, issue, 0, unroll=8)
    for k in range(TOP_K):
        pltpu.make_async_copy(ys_ref.at[pl.ds(0, tm * ROW_SUB)], buf_ref.at[k], sem).wait()
    rw = rw_ref[...]
    acc = None
    for k in range(TOP_K):
        term = rw[:, k:k + 1] * _unpack_bf16_pairs(_load_rows(buf_ref.at[k]))
        acc = term if acc is None else acc + term
    out = _ln_rows(alpha * x_ref[...] + acc, g_ref[...], b_ref[...])
    row = lax.broadcasted_iota(I32, out.shape, 0)
    first_tile = lax.rem(pl.program_id(0), tiles_per_seq) == 0
    o_ref[...] = jnp.where(first_tile & (row < PAD_FRONT), 0.0, out)


def _combine(slot3, h1, route_w, ys, wts, alpha, tm, tp):
    n_rows = h1.shape[0]
    assert tp % tm == 0 and tm >= PAD_FRONT
    return pl.pallas_call(
        functools.partial(_combine_kernel, alpha=alpha, tiles_per_seq=tp // tm),
        grid=(n_rows // tm,),
        in_specs=[pl.BlockSpec((1, 1, TOP_K * tm), lambda i: (i, 0, 0), memory_space=pltpu.SMEM),
                  pl.BlockSpec((tm, D_MODEL), lambda i: (i, 0)),
                  pl.BlockSpec((tm, LANES), lambda i: (i, 0)),
                  pl.BlockSpec(memory_space=pl.ANY),
                  _const_spec((1, D_MODEL)), _const_spec((1, D_MODEL))],
        out_specs=pl.BlockSpec((tm, D_MODEL), lambda i: (i, 0)),
        out_shape=jax.ShapeDtypeStruct((n_rows, D_MODEL), F32),
        scratch_shapes=[pltpu.VMEM((TOP_K, tm * ROW_SUB, LANES), U32), pltpu.SemaphoreType.DMA(())],
        name="moe_combine_ln2",
        compiler_params=_cparams(("arbitrary",)),
    )(slot3, h1, route_w, ys, wts["ln2_g"], wts["ln2_b"])


def _moe(h1, h1p, route_i, route_w, cnt, wts, expert_wts, layer, xs_prev, alpha, tm, tp):
    n_rows = h1.shape[0]
    n_tiles = -(-TOP_K * n_rows // EXPERT_TILE) + N_EXPERTS
    slot, te, n_used = _route_plan(route_i, cnt, n_tiles)
    slot3 = (slot * ROW_SUB).reshape(n_rows // tm, 1, TOP_K * tm)
    if xs_prev is None:
        xs_prev = jnp.zeros((n_tiles * EXPERT_TILE * ROW_SUB, LANES), U32)
    xs = _dispatch(slot3, h1p, xs_prev, tm)
    ys = _experts(te + layer * N_EXPERTS, n_used, xs, expert_wts)
    return _combine(slot3, h1, route_w, ys, wts, alpha, tm, tp), xs


def _rot_half_cols(w, d):
    n = w.shape[1] // d
    w3 = w.reshape(w.shape[0], n, d)
    return jnp.concatenate([-w3[..., d // 2:], w3[..., :d // 2]], -1).reshape(w.shape)


def _tables(tp):
    pos = jnp.maximum(jnp.arange(tp, dtype=I32) - PAD_FRONT, 0).astype(F32)

    def cs(d):
        inv = ROPE_BASE ** (-jnp.arange(0, d, 2, dtype=F32) / d)
        ang = pos[:, None] * inv[None, :]
        return jnp.cos(ang), jnp.sin(ang)

    rc, rs = cs(RET_DK)
    ret_cos = jnp.tile(jnp.concatenate([rc, rc], -1), (1, 2 * RET_HEADS))
    ret_sin = jnp.tile(jnp.concatenate([rs, rs], -1), (1, 2 * RET_HEADS))
    mc, ms = cs(MLA_ROPE)
    ones = jnp.ones((tp, MLA_NOPE), F32)
    tail1 = jnp.ones((tp, LANES - MLA_NOPE - MLA_ROPE), F32)
    mla_cos = jnp.concatenate([ones, mc, mc, tail1], -1)
    mla_sin = jnp.concatenate([0 * ones, ms, ms, 0 * tail1], -1)
    log_gamma = jnp.log1p(-jnp.exp2(-5.0 - jnp.arange(RET_HEADS, dtype=F32)))
    idx = jnp.arange(CHUNK, dtype=F32)
    rel = idx[:, None] - idx[None, :]
    dmat = jnp.where(rel >= 0, jnp.exp(log_gamma[:, None, None] * jnp.maximum(rel, 0.0)), 0.0)
    k_dec = jnp.exp(log_gamma[:, None] * (CHUNK - 1.0 - idx))
    q_dec = jnp.exp(log_gamma[:, None] * (idx + 1.0))
    rep = lambda t: jnp.repeat(t.T, RET_DK, axis=1)
    ret_dec = jnp.concatenate([rep(q_dec), rep(k_dec)], -1)
    ret_cdec = jnp.exp(log_gamma * CHUNK)
    return dict(ret_cos=ret_cos, ret_sin=ret_sin, mla_cos=mla_cos, mla_sin=mla_sin, ret_dmat=dmat, ret_dec=ret_dec,
                ret_cdec=ret_cdec)


def _layer_weights(w_in, w_merge, b_merge, g_cq, g_ckv, w_uq, w_ukv, conv_w, conv_b, w_lru_a, b_lru_a, w_lru_x,
                   b_lru_x, lru_lambda, w_gla_a2, b_gla_a2, w_branch, w_out, ln1_g, ln1_b, w_router, b_router,
                   ln2_g, ln2_b):
    D = D_MODEL
    z = lambda n: jnp.zeros((D, n), F32)
    rqk = jnp.concatenate([w_in[:, _O_RQ:_O_RK], w_in[:, _O_RK:_O_RV] * (RET_DK ** -0.5)], -1)
    kr = w_in[:, _O_KR:_O_LX]
    kr_slot = jnp.concatenate([z(MLA_NOPE), kr, z(LANES - MLA_NOPE - MLA_ROPE)], -1)
    kr_rot_slot = jnp.concatenate([z(MLA_NOPE), _rot_half_cols(kr, MLA_ROPE), z(LANES - MLA_NOPE - MLA_ROPE)], -1)
    gqk = jnp.concatenate([w_in[:, _O_GQ:_O_GK], w_in[:, _O_GK:_O_GV] * (GLA_DK ** -0.5)], -1)
    w16 = jnp.concatenate([rqk, _rot_half_cols(rqk, RET_DK), w_in[:, _O_RV:_O_RG],
                           w_in[:, _O_CQ:_O_KR], kr_slot, gqk, w_in[:, _O_GV:_O_GA], kr_rot_slot], -1).astype(BF16)
    w32 = jnp.concatenate([w_in[:, _O_RG:_O_CQ], w_in[:, _O_LX:_O_LG], w_in[:, _O_LG:_O_GQ], w_in[:, _O_GG:],
                           w_in[:, _O_GA:_O_GG], z(LANES - GLA_RANK)], -1).astype(BF16)
    hq = MLA_NOPE + MLA_ROPE
    uq3 = w_uq.reshape(MLA_Q_RANK, MLA_HEADS, hq)
    zq = jnp.zeros((MLA_Q_RANK, MLA_HEADS, LANES - hq), F32)
    uq = jnp.concatenate([uq3, zq], -1).reshape(MLA_Q_RANK, MLA_HEADS * LANES)
    uq_rope_rot = _rot_half_cols(uq3[..., MLA_NOPE:].reshape(MLA_Q_RANK, -1), MLA_ROPE)
    uq_rope_rot = uq_rope_rot.reshape(MLA_Q_RANK, MLA_HEADS, MLA_ROPE)
    uqr = jnp.concatenate([jnp.zeros((MLA_Q_RANK, MLA_HEADS, MLA_NOPE), F32), uq_rope_rot, zq], -1)
    uqr = uqr.reshape(MLA_Q_RANK, MLA_HEADS * LANES)
    eye = jnp.eye(LRU_BLOCKS, dtype=F32)
    bd = lambda w: jnp.einsum("ncd,nm->ncmd", w, eye).reshape(LRU_WIDTH, LRU_WIDTH)
    w_ab = jnp.concatenate([bd(w_lru_a), bd(w_lru_x)], -1).astype(BF16)
    wa2 = jnp.concatenate([w_gla_a2, jnp.zeros((LANES - GLA_RANK, GLA_HEADS * GLA_DK), F32)], 0).astype(BF16)
    wr = jnp.concatenate([w_router, jnp.zeros((D, LANES - N_EXPERTS), F32)], -1).astype(BF16)
    br = jnp.concatenate([b_router, jnp.full((LANES - N_EXPERTS,), -jnp.inf, F32)]).reshape(1, LANES)
    row = lambda v: v.reshape(1, -1).astype(F32)
    return dict(
        w16=w16, w32=w32,
        g_cq=row(g_cq), g_ckv=row(g_ckv), w_uq=uq.astype(BF16), w_uq_rot=uqr.astype(BF16), w_ukv=w_ukv.astype(BF16),
        conv_w=conv_w, conv_b=row(conv_b), w_lru_ab=w_ab, b_lru_ab=row(jnp.concatenate([b_lru_a, b_lru_x])),
        lru_ncsp=row(-LRU_C * jax.nn.softplus(-lru_lambda)),
        w_gla_a2=wa2, b_gla_a2=row(b_gla_a2),
        w_merge=w_merge.astype(BF16), b_merge=row(b_merge), w_branch=w_branch.astype(BF16), w_out=w_out.astype(BF16),
        ln1_g=row(ln1_g), ln1_b=row(ln1_b), w_router=wr, b_router=br,
        ln2_g=row(ln2_g), ln2_b=row(ln2_b))


def _row_tile(n_rows, want):
    best = CHUNK
    for t in range(CHUNK, want + 1, CHUNK):
        if n_rows % t == 0:
            best = t
    return best


def kernel(x, meta_tokens, ln0_g, ln0_b, w_in, w_merge, b_merge, g_cq, g_ckv, w_uq, w_ukv, conv_w, conv_b, w_lru_a, b_lru_a, w_lru_x, b_lru_x, lru_lambda, w_gla_a2, b_gla_a2, w_branch, w_out, ln1_g, ln1_b, w_router, b_router, w_exp_gate, b_exp_gate, w_exp_up, b_exp_up, w_exp_down, b_exp_down, ln2_g, ln2_b):
    B, S, D = x.shape
    depth = w_in.shape[0]
    alpha = (2.0 * depth) ** 0.25
    tp = S + CHUNK
    nc = tp // CHUNK
    n_rows = B * tp
    tabs = _tables(tp)
    per_layer = (w_in, w_merge, b_merge, g_cq, g_ckv, w_uq, w_ukv, conv_w, conv_b, w_lru_a, b_lru_a, w_lru_x, b_lru_x,
                 lru_lambda, w_gla_a2, b_gla_a2, w_branch, w_out, ln1_g, ln1_b, w_router, b_router, ln2_g, ln2_b)
    n_le = depth * N_EXPERTS
    expert_wts = (w_exp_gate.reshape(n_le, D, D_EXPERT), b_exp_gate.reshape(n_le, 1, D_EXPERT),
                  w_exp_up.reshape(n_le, D, D_EXPERT), b_exp_up.reshape(n_le, 1, D_EXPERT),
                  w_exp_down.reshape(n_le, D_EXPERT, D), b_exp_down.reshape(n_le, 1, D))
    tm = _row_tile(tp, 640)
    t_att = _row_tile(tp, 1664)
    h = _ln0(x, meta_tokens.astype(x.dtype), ln0_g, ln0_b, tm).reshape(n_rows, D)
    xs_buf = None
    for l in range(depth):
        wts = _layer_weights(*(p[l] for p in per_layer))
        p16, p32 = _proj(h, wts["w16"], wts["w32"], tm)
        y_ret = _retention(p16, p32, tabs, B, tp, tm)
        qt, kh, vt = _mla_prep(p16, tabs, wts, B, tp, tm)
        y_mla = _flash(qt, kh, vt, t_att)
        y_lru = _lru(p32, wts, B, tp, tm)
        y_gla = _gla(p16, p32, wts, B, nc)
        h1, h1p, route_i, route_w, cnt = _merge(h, (y_ret, y_mla, y_lru, y_gla), wts, alpha, tm)
        h, xs_buf = _moe(h1, h1p, route_i, route_w, cnt, wts, expert_wts, l, xs_buf, alpha, tm, tp)
    return h.reshape(B, tp, D)[:, CHUNK:]
```
